```python
import jax, jax.numpy as jnp
from jax import lax
import numpy as np

D_MODEL = 1024
BATCH = 16
SEQ = 4096
DEPTH = 2

N_MIXERS = 2
N_GLA_LAYERS = (DEPTH + N_MIXERS - 1) // N_MIXERS
N_ATT_LAYERS = DEPTH // N_MIXERS
N_META = 16
GRID_W = 64
D_FF = 2816
NORM_EPS = 1e-6
MACARON_WEIGHT = 0.5

GLA_HEADS = 4
GLA_DK = D_MODEL // 2 // GLA_HEADS
GLA_DV = D_MODEL // GLA_HEADS
GLA_QK_W = GLA_HEADS * GLA_DK
GLA_V_W = GLA_HEADS * GLA_DV
GLA_GATE_RANK = 16
GLA_GATE_TAU = 16.0
GLA_CHUNK = 64
GLA_PAD = GLA_CHUNK - N_META

ATT_Q_HEADS = 8
ATT_KV_HEADS = 2
ATT_HEAD_DIM = D_MODEL // ATT_Q_HEADS
ATT_GROUP = ATT_Q_HEADS // ATT_KV_HEADS
ATT_Q_W = ATT_Q_HEADS * ATT_HEAD_DIM
ATT_KV_W = ATT_KV_HEADS * ATT_HEAD_DIM
ATT_BLOCK = 128
ROPE_THETA = 10000.0
ROPE_AXIS_DIM = ATT_HEAD_DIM // 2

kernel_name = "hybrid_gla_axial_gqa_macaron_encoder"


def rms_norm(x, gain):
    xf = x.astype(jnp.float32)
    y = xf * lax.rsqrt(jnp.mean(xf * xf, axis=-1, keepdims=True) + NORM_EPS)
    return (y * gain.astype(jnp.float32)).astype(x.dtype)


def swiglu(h, w_gate, w_up, w_down):
    return (jax.nn.silu(h @ w_gate) * (h @ w_up)) @ w_down


def gla_chunk_scan(q, k, v, logg):
    b = jnp.cumsum(logg, axis=3)
    b_last = b[:, :, :, -1:, :]
    q_dec = q * jnp.exp(b)
    k_inv = k * jnp.exp(-b)
    k_end = k * jnp.exp(b_last - b)
    t_len = q.shape[3]
    causal = jnp.tril(jnp.ones((t_len, t_len), dtype=bool))
    a = jnp.where(causal, jnp.einsum('bhctd,bhcsd->bhcts', q_dec, k_inv), 0.0)
    o_intra = jnp.einsum('bhcts,bhcsv->bhctv', a, v)
    decay = jnp.exp(b_last[:, :, :, 0, :])

    def step(state, xs):
        q_c, k_c, v_c, dec_c = xs
        out = jnp.einsum('bhtd,bhdv->bhtv', q_c, state)
        state = dec_c[..., None] * state + jnp.einsum('bhsd,bhsv->bhdv', k_c, v_c)
        return state, out

    xs = (jnp.moveaxis(q_dec, 2, 0), jnp.moveaxis(k_end, 2, 0),
          jnp.moveaxis(v, 2, 0), jnp.moveaxis(decay, 2, 0))
    state0 = jnp.zeros((q.shape[0], q.shape[1], q.shape[4], v.shape[4]), q.dtype)
    _, o_inter = lax.scan(step, state0, xs)
    return o_intra + jnp.moveaxis(o_inter, 0, 2)


def gla_mixer(h, w_in, gate_w1, gate_w2, gate_b, head_norm, w_out):
    bsz, seq_len, _ = h.shape
    q, k, v, r = jnp.split(h @ w_in, [GLA_QK_W, 2 * GLA_QK_W, 2 * GLA_QK_W + GLA_V_W], axis=-1)
    q = q * GLA_DK ** -0.5
    z = jnp.einsum('nblr,nrk->nblk', jnp.einsum('bld,ndr->nblr', h, gate_w1), gate_w2) + gate_b[:, None, None, :]
    logg = jax.nn.log_sigmoid(z.astype(jnp.float32)) / GLA_GATE_TAU

    def to_chunks(t, d):
        t = jnp.pad(t.astype(jnp.float32), ((0, 0), (GLA_PAD, 0), (0, 0)))
        n_chunks = t.shape[1] // GLA_CHUNK
        return t.reshape(bsz, n_chunks, GLA_CHUNK, GLA_HEADS, d).transpose(0, 3, 1, 2, 4)

    qc, kc, vc = to_chunks(q, GLA_DK), to_chunks(k, GLA_DK), to_chunks(v, GLA_DV)
    g_fwd, g_bwd = to_chunks(logg[0], GLA_DK), to_chunks(logg[1], GLA_DK)
    flip = lambda t: jnp.flip(t, axis=(2, 3))
    o_fwd = gla_chunk_scan(qc, kc, vc, g_fwd)
    o_bwd = flip(gla_chunk_scan(flip(qc), flip(kc), flip(vc), flip(g_bwd)))
    o = (o_fwd + o_bwd).transpose(0, 2, 3, 1, 4)
    o = o.reshape(bsz, -1, GLA_HEADS, GLA_DV)[:, GLA_PAD:]
    o = rms_norm(o, head_norm) * jax.nn.silu(r.astype(jnp.float32)).reshape(bsz, seq_len, GLA_HEADS, GLA_DV)
    return o.reshape(bsz, seq_len, GLA_V_W).astype(h.dtype) @ w_out


def axial_rope_tables(n_real):
    rows = n_real // GRID_W
    t_row = jnp.broadcast_to(jnp.arange(rows)[:, None], (rows, GRID_W)).reshape(-1)
    t_col = jnp.broadcast_to(jnp.arange(GRID_W)[None, :], (rows, GRID_W)).reshape(-1)
    meta = jnp.zeros((N_META,), t_row.dtype)
    t_row = jnp.concatenate([meta, t_row]).astype(jnp.float32)
    t_col = jnp.concatenate([meta, t_col]).astype(jnp.float32)
    inv_freq = ROPE_THETA ** (-jnp.arange(0, ROPE_AXIS_DIM, 2, dtype=jnp.float32) / ROPE_AXIS_DIM)
    ang_row = t_row[:, None] * inv_freq[None, :]
    ang_col = t_col[:, None] * inv_freq[None, :]
    return jnp.cos(ang_row), jnp.sin(ang_row), jnp.cos(ang_col), jnp.sin(ang_col)


def rotate_half_pairs(x, cos, sin):
    x1, x2 = jnp.split(x, 2, axis=-1)
    c, s = cos[None, :, None, :], sin[None, :, None, :]
    return jnp.concatenate([x1 * c - x2 * s, x2 * c + x1 * s], axis=-1)


def apply_axial_rope(x, tables):
    cos_r, sin_r, cos_c, sin_c = tables
    xf = x.astype(jnp.float32)
    out = jnp.concatenate([rotate_half_pairs(xf[..., :ROPE_AXIS_DIM], cos_r, sin_r),
                           rotate_half_pairs(xf[..., ROPE_AXIS_DIM:], cos_c, sin_c)], axis=-1)
    return out.astype(x.dtype)


def attn_mixer(h, w_in, q_norm, k_norm, w_out):
    bsz, seq_len, _ = h.shape
    n_real = seq_len - N_META
    q, k, v = jnp.split(h @ w_in, [ATT_Q_W, ATT_Q_W + ATT_KV_W], axis=-1)
    q = rms_norm(q.reshape(bsz, seq_len, ATT_Q_HEADS, ATT_HEAD_DIM), q_norm)
    k = rms_norm(k.reshape(bsz, seq_len, ATT_KV_HEADS, ATT_HEAD_DIM), k_norm)
    v = v.reshape(bsz, seq_len, ATT_KV_HEADS, ATT_HEAD_DIM)
    tables = axial_rope_tables(n_real)
    q = apply_axial_rope(q, tables) * ATT_HEAD_DIM ** -0.5
    k = apply_axial_rope(k, tables)
    q = q.reshape(bsz, seq_len, ATT_KV_HEADS, ATT_GROUP, ATT_HEAD_DIM)

    def attend(q_blk):
        s = jnp.einsum('bqkgd,bskd->bkgqs', q_blk, k).astype(jnp.float32)
        p = jax.nn.softmax(s, axis=-1).astype(v.dtype)
        return jnp.einsum('bkgqs,bskd->bqkgd', p, v)

    o_meta = attend(q[:, :N_META])
    n_blk = n_real // ATT_BLOCK
    q_blocks = jnp.moveaxis(q[:, N_META:].reshape(bsz, n_blk, ATT_BLOCK, ATT_KV_HEADS, ATT_GROUP, ATT_HEAD_DIM), 1, 0)
    o_real = jnp.moveaxis(lax.map(attend, q_blocks), 0, 1).reshape(bsz, n_real, ATT_KV_HEADS, ATT_GROUP, ATT_HEAD_DIM)
    o = jnp.concatenate([o_meta, o_real], axis=1).reshape(bsz, seq_len, ATT_Q_W)
    return o @ w_out


def setup_inputs(seed: int = 0) -> dict:
    key = jax.random.key(seed)
    ks = iter(jax.random.split(key, 32))
    nrm = lambda shape, fan_in: jax.random.normal(next(ks), shape, jnp.float32) * fan_in ** -0.5
    gain = lambda shape: 1.0 + 0.02 * jax.random.normal(next(ks), shape, jnp.float32)
    return {
        "x": jax.random.normal(next(ks), (BATCH, SEQ, D_MODEL), jnp.float32),
        "meta_tokens": jax.random.normal(next(ks), (N_META, D_MODEL), jnp.float32),
        "norm_ffn1": gain((DEPTH, D_MODEL)),
        "ffn1_w_gate": nrm((DEPTH, D_MODEL, D_FF), D_MODEL),
        "ffn1_w_up": nrm((DEPTH, D_MODEL, D_FF), D_MODEL),
        "ffn1_w_down": nrm((DEPTH, D_FF, D_MODEL), D_FF),
        "norm_mix": gain((DEPTH, D_MODEL)),
        "gla_w_in": nrm((N_GLA_LAYERS, D_MODEL, 2 * GLA_QK_W + 2 * GLA_V_W), D_MODEL),
        "gla_gate_w1": nrm((N_GLA_LAYERS, 2, D_MODEL, GLA_GATE_RANK), D_MODEL),
        "gla_gate_w2": nrm((N_GLA_LAYERS, 2, GLA_GATE_RANK, GLA_QK_W), GLA_GATE_RANK),
        "gla_gate_b": 0.1 * jax.random.normal(next(ks), (N_GLA_LAYERS, 2, GLA_QK_W), jnp.float32),
        "gla_head_norm": gain((N_GLA_LAYERS, GLA_DV)),
        "gla_w_out": nrm((N_GLA_LAYERS, GLA_V_W, D_MODEL), GLA_V_W),
        "attn_w_in": nrm((N_ATT_LAYERS, D_MODEL, ATT_Q_W + 2 * ATT_KV_W), D_MODEL),
        "attn_q_norm": gain((N_ATT_LAYERS, ATT_HEAD_DIM)),
        "attn_k_norm": gain((N_ATT_LAYERS, ATT_HEAD_DIM)),
        "attn_w_out": nrm((N_ATT_LAYERS, ATT_Q_W, D_MODEL), ATT_Q_W),
        "norm_ffn2": gain((DEPTH, D_MODEL)),
        "ffn2_w_gate": nrm((DEPTH, D_MODEL, D_FF), D_MODEL),
        "ffn2_w_up": nrm((DEPTH, D_MODEL, D_FF), D_MODEL),
        "ffn2_w_down": nrm((DEPTH, D_FF, D_MODEL), D_FF),
        "norm_final": gain((D_MODEL,)),
    }


def reference(x, meta_tokens, norm_ffn1, ffn1_w_gate, ffn1_w_up, ffn1_w_down, norm_mix,
              gla_w_in, gla_gate_w1, gla_gate_w2, gla_gate_b, gla_head_norm, gla_w_out,
              attn_w_in, attn_q_norm, attn_k_norm, attn_w_out,
              norm_ffn2, ffn2_w_gate, ffn2_w_up, ffn2_w_down, norm_final):
    bsz = x.shape[0]
    meta = jnp.broadcast_to(meta_tokens.astype(x.dtype)[None], (bsz, N_META, x.shape[-1]))
    h = jnp.concatenate([meta, x], axis=1)
    for i in range(DEPTH):
        h = h + MACARON_WEIGHT * swiglu(rms_norm(h, norm_ffn1[i]), ffn1_w_gate[i], ffn1_w_up[i], ffn1_w_down[i])
        hn = rms_norm(h, norm_mix[i])
        j = i // N_MIXERS
        if i % N_MIXERS == 0:
            h = h + gla_mixer(hn, gla_w_in[j], gla_gate_w1[j], gla_gate_w2[j], gla_gate_b[j], gla_head_norm[j], gla_w_out[j])
        else:
            h = h + attn_mixer(hn, attn_w_in[j], attn_q_norm[j], attn_k_norm[j], attn_w_out[j])
        h = h + MACARON_WEIGHT * swiglu(rms_norm(h, norm_ffn2[i]), ffn2_w_gate[i], ffn2_w_up[i], ffn2_w_down[i])
    return rms_norm(h[:, N_META:], norm_final)
```

```python
import functools

import jax
import jax.numpy as jnp
from jax import lax
from jax.experimental import pallas as pl
from jax.experimental.pallas import tpu as pltpu

F32 = jnp.float32
BF16 = jnp.bfloat16

D_MODEL = 1024
N_META = 16
GRID_W = 64
D_FF = 2816
NORM_EPS = 1e-6
MACARON_WEIGHT = 0.5

GLA_HEADS = 4
GLA_DK = D_MODEL // 2 // GLA_HEADS
GLA_DV = D_MODEL // GLA_HEADS
GLA_QK_W = GLA_HEADS * GLA_DK
GLA_V_W = GLA_HEADS * GLA_DV
GLA_GATE_RANK = 16
GLA_GATE_TAU = 16.0
GLA_CHUNK = 64

ATT_Q_HEADS = 8
ATT_KV_HEADS = 2
ATT_HEAD_DIM = D_MODEL // ATT_Q_HEADS
ATT_GROUP = ATT_Q_HEADS // ATT_KV_HEADS
ATT_Q_W = ATT_Q_HEADS * ATT_HEAD_DIM
ATT_KV_W = ATT_KV_HEADS * ATT_HEAD_DIM
ROPE_THETA = 10000.0
ROPE_AXIS_DIM = ATT_HEAD_DIM // 2

LANES = 128
ROW_TILE = 256
FF_CHUNK = 256
GLA_BLOCK = 512
ATT_Q_TILE = 256
ATT_KV_BLOCK = 512
VMEM_LIMIT = 56 * 1024 * 1024


def _params(semantics):
    return pltpu.CompilerParams(dimension_semantics=semantics, vmem_limit_bytes=VMEM_LIMIT)


def _const_spec(shape):
    nd = len(shape)
    return pl.BlockSpec(shape, lambda *_: (0,) * nd)


def _rms(x, gain):
    ms = jnp.mean(x * x, axis=-1, keepdims=True)
    return x * lax.rsqrt(ms + NORM_EPS) * gain


def _dot(a, b):
    return jnp.dot(a, b, preferred_element_type=F32)


def _dot_nt(a, b):
    return lax.dot_general(a, b, (((1,), (1,)), ((), ())), preferred_element_type=F32)


def _dot_tn(a, b):
    return lax.dot_general(a, b, (((0,), (0,)), ((), ())), preferred_element_type=F32)


def _ffn_body(final_norm, h_ref, gain_ref, wg_ref, wu_ref, wd_ref, *rest):
    if final_norm:
        fgain_ref, o_ref, a_ref = rest
    else:
        o_ref, a_ref = rest
    x = h_ref[...]
    xn = _rms(x, gain_ref[...]).astype(BF16)
    for c in range(D_FF // FF_CHUNK):
        cols = slice(c * FF_CHUNK, (c + 1) * FF_CHUNK)
        g = _dot(xn, wg_ref[:, cols])
        u = _dot(xn, wu_ref[:, cols])
        a_ref[:, cols] = (g * jax.nn.sigmoid(g) * u).astype(BF16)
    y = x + MACARON_WEIGHT * _dot(a_ref[...], wd_ref[...])
    if final_norm:
        y = _rms(y, fgain_ref[...])
    o_ref[...] = y


def _ffn(h, gain, wg, wu, wd, final_gain=None, n_rows=None):
    n_rows = h.shape[0] if n_rows is None else n_rows
    row_spec = pl.BlockSpec((ROW_TILE, D_MODEL), lambda i: (i, 0))
    in_specs = [row_spec, _const_spec((1, D_MODEL)), _const_spec((D_MODEL, D_FF)),
                _const_spec((D_MODEL, D_FF)), _const_spec((D_FF, D_MODEL))]
    args = [h, gain, wg, wu, wd]
    if final_gain is not None:
        in_specs.append(_const_spec((1, D_MODEL)))
        args.append(final_gain)
    return pl.pallas_call(
        functools.partial(_ffn_body, final_gain is not None),
        grid=(n_rows // ROW_TILE,),
        in_specs=in_specs,
        out_specs=row_spec,
        out_shape=jax.ShapeDtypeStruct((n_rows, D_MODEL), F32),
        scratch_shapes=[pltpu.VMEM((ROW_TILE, D_FF), BF16)],
        compiler_params=_params(("parallel",)),
        name="swiglu_half_step",
    )(*args)


def _gla_in_body(h_ref, gain_ref, win_ref, w1_ref, w2_ref, gb_ref,
                 q_ref, k_ref, v_ref, r_ref, lgf_ref, lgb_ref):
    xn = _rms(h_ref[...], gain_ref[...]).astype(BF16)
    q_ref[...] = _dot(xn, win_ref[:, 0:GLA_QK_W]) * (GLA_DK ** -0.5)
    k_ref[...] = _dot(xn, win_ref[:, GLA_QK_W:2 * GLA_QK_W])
    v_ref[...] = _dot(xn, win_ref[:, 2 * GLA_QK_W:2 * GLA_QK_W + GLA_V_W]).astype(BF16)
    r_ref[...] = _dot(xn, win_ref[:, 2 * GLA_QK_W + GLA_V_W:])
    t = _dot(xn, w1_ref[...]).astype(BF16)
    z = _dot(t, w2_ref[...]) + gb_ref[...]
    logg = -(jnp.maximum(-z, 0.0) + jnp.log1p(jnp.exp(-jnp.abs(z)))) / GLA_GATE_TAU
    lgf_ref[...] = logg[:, 0:GLA_QK_W]
    lgb_ref[...] = logg[:, GLA_QK_W:]


def _gla_in(h, gain, w_in, w1, w2, gb):
    n_rows = h.shape[0]
    spec = lambda w: pl.BlockSpec((ROW_TILE, w), lambda i: (i, 0))
    shp = lambda w, dt: jax.ShapeDtypeStruct((n_rows, w), dt)
    return pl.pallas_call(
        _gla_in_body,
        grid=(n_rows // ROW_TILE,),
        in_specs=[spec(D_MODEL), _const_spec((1, D_MODEL)), _const_spec(w_in.shape),
                  _const_spec(w1.shape), _const_spec(w2.shape), _const_spec(gb.shape)],
        out_specs=[spec(GLA_QK_W), spec(GLA_QK_W), spec(GLA_V_W), spec(GLA_V_W),
                   spec(GLA_QK_W), spec(GLA_QK_W)],
        out_shape=[shp(GLA_QK_W, F32), shp(GLA_QK_W, F32), shp(GLA_V_W, BF16), shp(GLA_V_W, F32),
                   shp(GLA_QK_W, F32), shp(GLA_QK_W, F32)],
        compiler_params=_params(("parallel",)),
        name="gla_in_proj",
    )(h, gain, w_in, w1, w2, gb)


def _gla_chunk(rev, q, k, v, lg, s_ref):
    t_len = q.shape[0]
    row = lax.broadcasted_iota(jnp.int32, (t_len, t_len), 0)
    col = lax.broadcasted_iota(jnp.int32, (t_len, t_len), 1)
    mask = (col >= row) if rev else (col <= row)
    b = jnp.dot(mask.astype(F32), lg, preferred_element_type=F32, precision=lax.Precision.HIGHEST)
    b_tot = b[0:1, :] if rev else b[t_len - 1:t_len, :]
    qd = (q * jnp.exp(b)).astype(BF16)
    ki = (k * jnp.exp(-b)).astype(BF16)
    ke = (k * jnp.exp(b_tot - b)).astype(BF16)
    dec = jnp.exp(b_tot)
    outs = []
    for h in range(GLA_HEADS):
        ks = slice(h * GLA_DK, (h + 1) * GLA_DK)
        vs = slice(h * GLA_DV, (h + 1) * GLA_DV)
        a = jnp.where(mask, _dot_nt(qd[:, ks], ki[:, ks]), 0.0).astype(BF16)
        st = s_ref[h]
        outs.append(_dot(a, v[:, vs]) + _dot_nt(qd[:, ks], st.astype(BF16)))
        s_ref[h] = dec[:, ks] * st + _dot_tn(v[:, vs], ke[:, ks])
    return jnp.concatenate(outs, axis=1)


def _gla_scan_body(rev, q_ref, k_ref, v_ref, lg_ref, qm_ref, km_ref, vm_ref, lgm_ref, *rest):
    if rev:
        of_ref, ofm_ref, o_ref, om_ref, s_ref = rest
    else:
        o_ref, om_ref, s_ref = rest
    j = pl.program_id(1)
    n_chunks = GLA_BLOCK // GLA_CHUNK
    pad = GLA_CHUNK - N_META

    @pl.when(j == 0)
    def _():
        s_ref[...] = jnp.zeros_like(s_ref)

    def meta_chunk():
        front = lambda ref, dt: jnp.concatenate([jnp.zeros((pad, ref.shape[1]), dt), ref[...]], axis=0)
        o = _gla_chunk(rev, front(qm_ref, F32), front(km_ref, F32), front(vm_ref, BF16),
                       front(lgm_ref, F32), s_ref)[pad:, :]
        om_ref[...] = o + ofm_ref[...] if rev else o

    if not rev:
        pl.when(j == 0)(meta_chunk)

    def step(ci, carry):
        c = n_chunks - 1 - ci if rev else ci
        rows = pl.ds(pl.multiple_of(c * GLA_CHUNK, GLA_CHUNK), GLA_CHUNK)
        o = _gla_chunk(rev, q_ref[rows, :], k_ref[rows, :], v_ref[rows, :], lg_ref[rows, :], s_ref)
        if rev:
            o = o + of_ref[rows, :]
        o_ref[rows, :] = o
        return carry

    lax.fori_loop(0, n_chunks, step, 0)

    if rev:
        pl.when(j == pl.num_programs(1) - 1)(meta_chunk)


def _gla_scan(rev, bsz, seq, q, k, v, lg, o_prev=None):
    n_blk = seq // GLA_BLOCK
    meta_blk0 = bsz * seq // N_META
    blk = (lambda j: n_blk - 1 - j) if rev else (lambda j: j)
    real = lambda w: pl.BlockSpec((GLA_BLOCK, w), lambda b, j: (b * n_blk + blk(j), 0))
    meta = lambda w: pl.BlockSpec((N_META, w), lambda b, j: (meta_blk0 + b, 0))
    in_specs = [real(GLA_QK_W), real(GLA_QK_W), real(GLA_V_W), real(GLA_QK_W),
                meta(GLA_QK_W), meta(GLA_QK_W), meta(GLA_V_W), meta(GLA_QK_W)]
    args = [q, k, v, lg, q, k, v, lg]
    if rev:
        in_specs += [real(GLA_V_W), pl.BlockSpec((N_META, GLA_V_W), lambda b, j: (b, 0))]
        args += list(o_prev)
    return pl.pallas_call(
        functools.partial(_gla_scan_body, rev),
        grid=(bsz, n_blk),
        in_specs=in_specs,
        out_specs=[real(GLA_V_W), pl.BlockSpec((N_META, GLA_V_W), lambda b, j: (b, 0))],
        out_shape=[jax.ShapeDtypeStruct((bsz * seq, GLA_V_W), F32),
                   jax.ShapeDtypeStruct((bsz * N_META, GLA_V_W), F32)],
        scratch_shapes=[pltpu.VMEM((GLA_HEADS, GLA_DV, GLA_DK), F32)],
        compiler_params=_params(("parallel", "arbitrary")),
        name="gla_scan_rev" if rev else "gla_scan_fwd",
    )(*args)


def _gla_out_body(n_real_tiles, o_ref, om_ref, r_ref, h_ref, hn_ref, wout_ref, out_ref):
    is_meta = pl.program_id(0) == n_real_tiles
    o = jnp.where(is_meta, om_ref[...], o_ref[...])
    r = r_ref[...]
    gate = r * jax.nn.sigmoid(r)
    cols = []
    for h in range(GLA_HEADS):
        vs = slice(h * GLA_DV, (h + 1) * GLA_DV)
        cols.append(_rms(o[:, vs], hn_ref[...]) * gate[:, vs])
    y = jnp.concatenate(cols, axis=1).astype(BF16)
    out_ref[...] = h_ref[...] + _dot(y, wout_ref[...])


def _gla_out(o_real, o_meta, r, h, head_norm, w_out):
    n_rows = h.shape[0]
    n_real_tiles = o_real.shape[0] // ROW_TILE
    row = pl.BlockSpec((ROW_TILE, D_MODEL), lambda i: (i, 0))
    return pl.pallas_call(
        functools.partial(_gla_out_body, n_real_tiles),
        grid=(n_rows // ROW_TILE,),
        in_specs=[pl.BlockSpec((ROW_TILE, GLA_V_W), lambda i: (jnp.minimum(i, n_real_tiles - 1), 0)),
                  _const_spec(o_meta.shape), row, row,
                  _const_spec((1, GLA_DV)), _const_spec(w_out.shape)],
        out_specs=row,
        out_shape=jax.ShapeDtypeStruct((n_rows, D_MODEL), F32),
        compiler_params=_params(("parallel",)),
        name="gla_out_proj",
    )(o_real, o_meta, r, h, head_norm, w_out)


def _rope(x, cos, sin_signed, first_half):
    partner = jnp.where(first_half, pltpu.roll(x, LANES - ROPE_AXIS_DIM // 2, 1),
                        pltpu.roll(x, ROPE_AXIS_DIM // 2, 1))
    return x * cos + partner * sin_signed


def _attn_in_body(h_ref, gain_ref, win_ref, qn_ref, kn_ref, cos_ref, sin_ref, q_ref, k_ref, v_ref):
    xn = _rms(h_ref[...], gain_ref[...]).astype(BF16)
    cos = cos_ref[...]
    sin = sin_ref[...]
    lane = lax.broadcasted_iota(jnp.int32, cos.shape, 1)
    first_half = (lane % ROPE_AXIS_DIM) < (ROPE_AXIS_DIM // 2)
    q = _dot(xn, win_ref[:, 0:ATT_Q_W])
    for h in range(ATT_Q_HEADS):
        hs = slice(h * ATT_HEAD_DIM, (h + 1) * ATT_HEAD_DIM)
        qh = _rope(_rms(q[:, hs], qn_ref[...]), cos, sin, first_half) * (ATT_HEAD_DIM ** -0.5)
        q_ref[:, hs] = qh.astype(BF16)
    k = _dot(xn, win_ref[:, ATT_Q_W:ATT_Q_W + ATT_KV_W])
    for h in range(ATT_KV_HEADS):
        hs = slice(h * ATT_HEAD_DIM, (h + 1) * ATT_HEAD_DIM)
        k_ref[:, hs] = _rope(_rms(k[:, hs], kn_ref[...]), cos, sin, first_half).astype(BF16)
    v_ref[...] = _dot(xn, win_ref[:, ATT_Q_W + ATT_KV_W:]).astype(BF16)


def _rope_tables(seq):
    pos = jnp.arange(seq)
    inv_freq = ROPE_THETA ** (-jnp.arange(0, ROPE_AXIS_DIM, 2, dtype=F32) / ROPE_AXIS_DIM)
    ang_row = (pos // GRID_W).astype(F32)[:, None] * inv_freq[None, :]
    ang_col = (pos % GRID_W).astype(F32)[:, None] * inv_freq[None, :]
    ang = jnp.concatenate([ang_row, ang_row, ang_col, ang_col], axis=1)
    ang = jnp.concatenate([ang, jnp.zeros((ROW_TILE, ATT_HEAD_DIM), F32)], axis=0)
    sign = jnp.tile(jnp.concatenate([-jnp.ones((ROPE_AXIS_DIM // 2,), F32),
                                     jnp.ones((ROPE_AXIS_DIM // 2,), F32)]), 2)
    return jnp.cos(ang), jnp.sin(ang) * sign[None, :]


def _attn_in(h, gain, w_in, q_norm, k_norm, cos, sin, seq):
    n_rows = h.shape[0]
    n_real_tiles = (n_rows - (cos.shape[0] - seq)) // ROW_TILE
    tiles_per_seq = seq // ROW_TILE
    spec = lambda w: pl.BlockSpec((ROW_TILE, w), lambda i: (i, 0))
    table = pl.BlockSpec((ROW_TILE, ATT_HEAD_DIM),
                         lambda i: (jnp.where(i < n_real_tiles, i % tiles_per_seq, tiles_per_seq), 0))
    shp = lambda w: jax.ShapeDtypeStruct((n_rows, w), BF16)
    return pl.pallas_call(
        _attn_in_body,
        grid=(n_rows // ROW_TILE,),
        in_specs=[spec(D_MODEL), _const_spec((1, D_MODEL)), _const_spec(w_in.shape),
                  _const_spec((1, ATT_HEAD_DIM)), _const_spec((1, ATT_HEAD_DIM)), table, table],
        out_specs=[spec(ATT_Q_W), spec(ATT_KV_W), spec(ATT_KV_W)],
        out_shape=[shp(ATT_Q_W), shp(ATT_KV_W), shp(ATT_KV_W)],
        compiler_params=_params(("parallel",)),
        name="attn_in_proj",
    )(h, gain, w_in, q_norm, k_norm, cos, sin)


def _attn_core_body(n_kv_blocks, q_ref, k_ref, v_ref, km_ref, vm_ref, o_ref, s_ref):
    tq = q_ref.shape[0]
    km = km_ref[...]
    vm = vm_ref[...]
    for g in range(ATT_GROUP):
        hs = slice(g * ATT_HEAD_DIM, (g + 1) * ATT_HEAD_DIM)
        q = q_ref[:, hs]

        def scores(kb, mx):
            rows = pl.ds(pl.multiple_of(kb * ATT_KV_BLOCK, ATT_KV_BLOCK), ATT_KV_BLOCK)
            s = _dot_nt(q, k_ref[rows, :])
            s_ref[kb] = s
            for c in range(ATT_KV_BLOCK // LANES):
                mx = jnp.maximum(mx, s[:, c * LANES:(c + 1) * LANES])
            return mx

        mx = lax.fori_loop(0, n_kv_blocks, scores, jnp.full((tq, LANES), -jnp.inf, F32))
        s_meta = _dot_nt(q, km)
        m = jnp.maximum(jnp.max(mx, axis=1, keepdims=True), jnp.max(s_meta, axis=1, keepdims=True))

        def weighted(kb, carry):
            l, acc = carry
            rows = pl.ds(pl.multiple_of(kb * ATT_KV_BLOCK, ATT_KV_BLOCK), ATT_KV_BLOCK)
            p = jnp.exp(s_ref[kb] - m)
            for c in range(ATT_KV_BLOCK // LANES):
                l = l + p[:, c * LANES:(c + 1) * LANES]
            return l, acc + _dot(p.astype(BF16), v_ref[rows, :])

        l, acc = lax.fori_loop(0, n_kv_blocks, weighted,
                               (jnp.zeros((tq, LANES), F32), jnp.zeros((tq, ATT_HEAD_DIM), F32)))
        p_meta = jnp.exp(s_meta - m)
        denom = jnp.sum(l, axis=1, keepdims=True) + jnp.sum(p_meta, axis=1, keepdims=True)
        acc = acc + _dot(p_meta.astype(BF16), vm)
        o_ref[:, hs] = (acc / denom).astype(BF16)


def _attn_core(bsz, seq, q, k, v, meta_queries):
    tq = N_META if meta_queries else ATT_Q_TILE
    n_q = 1 if meta_queries else seq // tq
    q_blk0 = bsz * seq // N_META if meta_queries else 0
    meta_blk0 = bsz * seq // N_META
    n_kv_blocks = seq // ATT_KV_BLOCK
    group_w = ATT_GROUP * ATT_HEAD_DIM
    kv_real = pl.BlockSpec((seq, ATT_HEAD_DIM), lambda b, kh, i: (b, kh))
    kv_meta = pl.BlockSpec((N_META, ATT_HEAD_DIM), lambda b, kh, i: (meta_blk0 + b, kh))
    q_spec = pl.BlockSpec((tq, group_w), lambda b, kh, i: (q_blk0 + b * n_q + i, kh))
    o_spec = pl.BlockSpec((tq, group_w), lambda b, kh, i: (b * n_q + i, kh))
    return pl.pallas_call(
        functools.partial(_attn_core_body, n_kv_blocks),
        grid=(bsz, ATT_KV_HEADS, n_q),
        in_specs=[q_spec, kv_real, kv_real, kv_meta, kv_meta],
        out_specs=o_spec,
        out_shape=jax.ShapeDtypeStruct((bsz * n_q * tq, ATT_Q_W), BF16),
        scratch_shapes=[pltpu.VMEM((n_kv_blocks, tq, ATT_KV_BLOCK), F32)],
        compiler_params=_params(("parallel", "parallel", "arbitrary")),
        name="attn_core_meta" if meta_queries else "attn_core",
    )(q, k, v, k, v)


def _attn_out_body(n_real_tiles, o_ref, om_ref, h_ref, wout_ref, out_ref):
    is_meta = pl.program_id(0) == n_real_tiles
    o = jnp.where(is_meta, om_ref[...], o_ref[...])
    out_ref[...] = h_ref[...] + _dot(o, wout_ref[...])


def _attn_out(o_real, o_meta, h, w_out):
    n_rows = h.shape[0]
    n_real_tiles = o_real.shape[0] // ROW_TILE
    row = pl.BlockSpec((ROW_TILE, D_MODEL), lambda i: (i, 0))
    return pl.pallas_call(
        functools.partial(_attn_out_body, n_real_tiles),
        grid=(n_rows // ROW_TILE,),
        in_specs=[pl.BlockSpec((ROW_TILE, ATT_Q_W), lambda i: (jnp.minimum(i, n_real_tiles - 1), 0)),
                  _const_spec(o_meta.shape), row, _const_spec(w_out.shape)],
        out_specs=row,
        out_shape=jax.ShapeDtypeStruct((n_rows, D_MODEL), F32),
        compiler_params=_params(("parallel",)),
        name="attn_out_proj",
    )(o_real, o_meta, h, w_out)


def kernel(x, meta_tokens, norm_ffn1, ffn1_w_gate, ffn1_w_up, ffn1_w_down, norm_mix, gla_w_in, gla_gate_w1, gla_gate_w2, gla_gate_b, gla_head_norm, gla_w_out, attn_w_in, attn_q_norm, attn_k_norm, attn_w_out, norm_ffn2, ffn2_w_gate, ffn2_w_up, ffn2_w_down, norm_final):
    bsz, seq, d = x.shape
    depth = norm_ffn1.shape[0]
    assert d == D_MODEL and bsz * N_META == ROW_TILE and seq % GLA_BLOCK == 0 and seq % ATT_KV_BLOCK == 0
    n_real = bsz * seq
    row = lambda g: g.reshape(1, -1).astype(F32)

    meta = jnp.broadcast_to(meta_tokens.astype(x.dtype)[None], (bsz, N_META, d))
    h = jnp.concatenate([x.reshape(n_real, d), meta.reshape(bsz * N_META, d)], axis=0)
    cos, sin = _rope_tables(seq)

    for i in range(depth):
        h = _ffn(h, row(norm_ffn1[i]), ffn1_w_gate[i].astype(BF16), ffn1_w_up[i].astype(BF16),
                 ffn1_w_down[i].astype(BF16))
        j = i // 2
        if i % 2 == 0:
            w1 = jnp.concatenate([gla_gate_w1[j, 0], gla_gate_w1[j, 1]], axis=1).astype(BF16)
            zero = jnp.zeros_like(gla_gate_w2[j, 0])
            w2 = jnp.concatenate([jnp.concatenate([gla_gate_w2[j, 0], zero], axis=1),
                                  jnp.concatenate([zero, gla_gate_w2[j, 1]], axis=1)], axis=0).astype(BF16)
            q, k, v, r, lgf, lgb = _gla_in(h, row(norm_mix[i]), gla_w_in[j].astype(BF16), w1, w2,
                                           gla_gate_b[j].reshape(1, -1).astype(F32))
            o_fwd = _gla_scan(False, bsz, seq, q, k, v, lgf)
            o_real, o_meta = _gla_scan(True, bsz, seq, q, k, v, lgb, o_fwd)
            h = _gla_out(o_real, o_meta, r, h, row(gla_head_norm[j]), gla_w_out[j].astype(BF16))
        else:
            q, k, v = _attn_in(h, row(norm_mix[i]), attn_w_in[j].astype(BF16), row(attn_q_norm[j]),
                               row(attn_k_norm[j]), cos, sin, seq)
            o_real = _attn_core(bsz, seq, q, k, v, meta_queries=False)
            o_meta = _attn_core(bsz, seq, q, k, v, meta_queries=True)
            h = _attn_out(o_real, o_meta, h, attn_w_out[j].astype(BF16))
        last = i == depth - 1
        h = _ffn(h, row(norm_ffn2[i]), ffn2_w_gate[i].astype(BF16), ffn2_w_up[i].astype(BF16),
                 ffn2_w_down[i].astype(BF16), final_gain=row(norm_final) if last else None,
                 n_rows=n_real if last else None)
    return h.reshape(bsz, seq, d)
```

```python
import functools

import jax
import jax.numpy as jnp
from jax import lax
from jax.experimental import pallas as pl
from jax.experimental.pallas import tpu as pltpu

F32 = jnp.float32
BF16 = jnp.bfloat16

D_MODEL = 1024
N_META = 16
GRID_W = 64
D_FF = 2816
NORM_EPS = 1e-6
MACARON_WEIGHT = 0.5

GLA_HEADS = 4
GLA_DK = D_MODEL // 2 // GLA_HEADS
GLA_DV = D_MODEL // GLA_HEADS
GLA_QK_W = GLA_HEADS * GLA_DK
GLA_V_W = GLA_HEADS * GLA_DV
GLA_GATE_RANK = 16
GLA_GATE_TAU = 16.0
GLA_CHUNK = 64

ATT_Q_HEADS = 8
ATT_KV_HEADS = 2
ATT_HEAD_DIM = D_MODEL // ATT_Q_HEADS
ATT_GROUP = ATT_Q_HEADS // ATT_KV_HEADS
ATT_Q_W = ATT_Q_HEADS * ATT_HEAD_DIM
ATT_KV_W = ATT_KV_HEADS * ATT_HEAD_DIM
ROPE_THETA = 10000.0
ROPE_AXIS_DIM = ATT_HEAD_DIM // 2
LOG2_E = 1.4426950408889634

LANES = 128
ROW_TILE = 256
FF_CHUNK = 256
GLA_BLOCK = 512
ATT_Q_TILE = 256
ATT_KV_BLOCK = 512
VMEM_LIMIT = 56 * 1024 * 1024


def _params(semantics):
    return pltpu.CompilerParams(dimension_semantics=semantics, vmem_limit_bytes=VMEM_LIMIT)


def _const_spec(shape):
    nd = len(shape)
    return pl.BlockSpec(shape, lambda *_: (0,) * nd)


def _rms(x, gain):
    ms = jnp.mean(x * x, axis=-1, keepdims=True)
    return x * lax.rsqrt(ms + NORM_EPS) * gain


def _dot(a, b):
    return jnp.dot(a, b, preferred_element_type=F32)


def _dot_nt(a, b):
    return lax.dot_general(a, b, (((1,), (1,)), ((), ())), preferred_element_type=F32)


def _dot_tn(a, b):
    return lax.dot_general(a, b, (((0,), (0,)), ((), ())), preferred_element_type=F32)


def _ffn_body(n_real_tiles, final_norm, h_ref, *refs):
    refs = list(refs)
    x = h_ref[...]
    if n_real_tiles is not None:
        x = jnp.where(pl.program_id(0) == n_real_tiles, refs.pop(0)[...], x)
    gain_ref, wg_ref, wu_ref, wd_ref = refs[:4]
    fgain_ref = refs[4] if final_norm else None
    o_ref, a_ref = refs[-2:]
    xn = _rms(x, gain_ref[...]).astype(BF16)
    for c in range(D_FF // FF_CHUNK):
        cols = slice(c * FF_CHUNK, (c + 1) * FF_CHUNK)
        g = _dot(xn, wg_ref[:, cols])
        u = _dot(xn, wu_ref[:, cols])
        a_ref[:, cols] = (g * jax.nn.sigmoid(g) * u).astype(BF16)
    y = x + MACARON_WEIGHT * _dot(a_ref[...], wd_ref[...])
    if final_norm:
        y = _rms(y, fgain_ref[...])
    o_ref[...] = y


def _ffn(h, gain, wg, wu, wd, h_meta=None, final_gain=None, n_rows=None):
    n_real_tiles = None
    n_rows = h.shape[0] if n_rows is None else n_rows
    row_spec = pl.BlockSpec((ROW_TILE, D_MODEL), lambda i: (i, 0))
    in_specs, args = [row_spec], [h]
    if h_meta is not None:
        assert h_meta.shape[0] == ROW_TILE
        n_real_tiles = h.shape[0] // ROW_TILE
        n_rows = h.shape[0] + ROW_TILE
        in_specs = [pl.BlockSpec((ROW_TILE, D_MODEL), lambda i: (jnp.minimum(i, n_real_tiles - 1), 0)),
                    _const_spec((ROW_TILE, D_MODEL))]
        args.append(h_meta)
    in_specs += [_const_spec((1, D_MODEL)), _const_spec((D_MODEL, D_FF)), _const_spec((D_MODEL, D_FF)),
                 _const_spec((D_FF, D_MODEL))]
    args += [gain, wg, wu, wd]
    if final_gain is not None:
        in_specs.append(_const_spec((1, D_MODEL)))
        args.append(final_gain)
    return pl.pallas_call(
        functools.partial(_ffn_body, n_real_tiles, final_gain is not None),
        grid=(n_rows // ROW_TILE,),
        in_specs=in_specs,
        out_specs=row_spec,
        out_shape=jax.ShapeDtypeStruct((n_rows, D_MODEL), F32),
        scratch_shapes=[pltpu.VMEM((ROW_TILE, D_FF), BF16)],
        compiler_params=_params(("parallel",)),
        name="swiglu_half_step",
    )(*args)


def _gla_in_body(h_ref, gain_ref, win_ref, w1_ref, w2_ref, gb_ref,
                 q_ref, k_ref, v_ref, r_ref, lgf_ref, lgb_ref):
    xn = _rms(h_ref[...], gain_ref[...]).astype(BF16)
    q_ref[...] = (_dot(xn, win_ref[:, 0:GLA_QK_W]) * (GLA_DK ** -0.5)).astype(BF16)
    k_ref[...] = _dot(xn, win_ref[:, GLA_QK_W:2 * GLA_QK_W]).astype(BF16)
    v_ref[...] = _dot(xn, win_ref[:, 2 * GLA_QK_W:2 * GLA_QK_W + GLA_V_W]).astype(BF16)
    r_ref[...] = _dot(xn, win_ref[:, 2 * GLA_QK_W + GLA_V_W:]).astype(BF16)
    t = _dot(xn, w1_ref[...]).astype(BF16)
    z = _dot(t, w2_ref[...]) + gb_ref[...]
    logg = -(jnp.maximum(-z, 0.0) + jnp.log1p(jnp.exp(-jnp.abs(z)))) / GLA_GATE_TAU
    lgf_ref[...] = logg[:, 0:GLA_QK_W]
    lgb_ref[...] = logg[:, GLA_QK_W:]


def _gla_in(h, gain, w_in, w1, w2, gb):
    n_rows = h.shape[0]
    spec = lambda w: pl.BlockSpec((ROW_TILE, w), lambda i: (i, 0))
    shp = lambda w, dt: jax.ShapeDtypeStruct((n_rows, w), dt)
    return pl.pallas_call(
        _gla_in_body,
        grid=(n_rows // ROW_TILE,),
        in_specs=[spec(D_MODEL), _const_spec((1, D_MODEL)), _const_spec(w_in.shape),
                  _const_spec(w1.shape), _const_spec(w2.shape), _const_spec(gb.shape)],
        out_specs=[spec(GLA_QK_W), spec(GLA_QK_W), spec(GLA_V_W), spec(GLA_V_W),
                   spec(GLA_QK_W), spec(GLA_QK_W)],
        out_shape=[shp(GLA_QK_W, BF16), shp(GLA_QK_W, BF16), shp(GLA_V_W, BF16), shp(GLA_V_W, BF16),
                   shp(GLA_QK_W, F32), shp(GLA_QK_W, F32)],
        compiler_params=_params(("parallel",)),
        name="gla_in_proj",
    )(h, gain, w_in, w1, w2, gb)


def _gla_chunk(rev, q, k, v, lg, states):
    t_len = q.shape[0]
    row = lax.broadcasted_iota(jnp.int32, (t_len, t_len), 0)
    col = lax.broadcasted_iota(jnp.int32, (t_len, t_len), 1)
    mask = (col >= row) if rev else (col <= row)
    row3 = lax.broadcasted_iota(jnp.int32, (t_len, 3 * t_len), 0)
    col3 = lax.broadcasted_iota(jnp.int32, (t_len, 3 * t_len), 1) % t_len
    mask3 = ((col3 >= row3) if rev else (col3 <= row3)).astype(BF16)
    hi = lg.astype(BF16)
    rest = lg - hi.astype(F32)
    mid = rest.astype(BF16)
    lo = (rest - mid.astype(F32)).astype(BF16)
    b = _dot(mask3, jnp.concatenate([hi, mid, lo], axis=0))
    b_tot = b[0:1, :] if rev else b[t_len - 1:t_len, :]
    qd = (q * jnp.exp(b)).astype(BF16)
    ki = (k * jnp.exp(-b)).astype(BF16)
    ke = (k * jnp.exp(b_tot - b)).astype(BF16)
    dec = jnp.exp(b_tot)
    outs, new_states = [], []
    for h in range(GLA_HEADS):
        ks = slice(h * GLA_DK, (h + 1) * GLA_DK)
        vs = slice(h * GLA_DV, (h + 1) * GLA_DV)
        a = jnp.where(mask, _dot_nt(qd[:, ks], ki[:, ks]), 0.0).astype(BF16)
        st = states[h]
        outs.append(_dot(a, v[:, vs]) + _dot_nt(qd[:, ks], st.astype(BF16)))
        new_states.append(dec[:, ks] * st + _dot_tn(v[:, vs], ke[:, ks]))
    return jnp.concatenate(outs, axis=1), new_states


def _gla_scan_body(rev, q_ref, k_ref, v_ref, lg_ref, qm_ref, km_ref, vm_ref, lgm_ref, *rest):
    if rev:
        of_ref, ofm_ref, o_ref, om_ref, s_ref = rest
    else:
        o_ref, om_ref, s_ref = rest
    j = pl.program_id(1)
    n_chunks = GLA_BLOCK // GLA_CHUNK
    pad = GLA_CHUNK - N_META

    @pl.when(j == 0)
    def _():
        s_ref[...] = jnp.zeros_like(s_ref)

    def load_states():
        return [s_ref[h] for h in range(GLA_HEADS)]

    def store_states(states):
        for h in range(GLA_HEADS):
            s_ref[h] = states[h]

    def meta_chunk():
        front = lambda ref: jnp.concatenate([jnp.zeros((pad, ref.shape[1]), ref.dtype), ref[...]], axis=0)
        o, states = _gla_chunk(rev, front(qm_ref), front(km_ref), front(vm_ref), front(lgm_ref), load_states())
        store_states(states)
        om_ref[...] = o[pad:, :] + ofm_ref[...] if rev else o[pad:, :]

    if not rev:
        pl.when(j == 0)(meta_chunk)

    states = load_states()
    for ci in range(n_chunks):
        c = n_chunks - 1 - ci if rev else ci
        rows = slice(c * GLA_CHUNK, (c + 1) * GLA_CHUNK)
        o, states = _gla_chunk(rev, q_ref[rows, :], k_ref[rows, :], v_ref[rows, :], lg_ref[rows, :], states)
        if rev:
            o = o + of_ref[rows, :]
        o_ref[rows, :] = o
    store_states(states)

    if rev:
        pl.when(j == pl.num_programs(1) - 1)(meta_chunk)


def _gla_scan(rev, bsz, seq, q, k, v, lg, o_prev=None):
    n_blk = seq // GLA_BLOCK
    meta_blk0 = bsz * seq // N_META
    blk = (lambda j: n_blk - 1 - j) if rev else (lambda j: j)
    real = lambda w: pl.BlockSpec((GLA_BLOCK, w), lambda b, j: (b * n_blk + blk(j), 0))
    meta = lambda w: pl.BlockSpec((N_META, w), lambda b, j: (meta_blk0 + b, 0))
    in_specs = [real(GLA_QK_W), real(GLA_QK_W), real(GLA_V_W), real(GLA_QK_W),
                meta(GLA_QK_W), meta(GLA_QK_W), meta(GLA_V_W), meta(GLA_QK_W)]
    args = [q, k, v, lg, q, k, v, lg]
    if rev:
        in_specs += [real(GLA_V_W), pl.BlockSpec((N_META, GLA_V_W), lambda b, j: (b, 0))]
        args += list(o_prev)
    return pl.pallas_call(
        functools.partial(_gla_scan_body, rev),
        grid=(bsz, n_blk),
        in_specs=in_specs,
        out_specs=[real(GLA_V_W), pl.BlockSpec((N_META, GLA_V_W), lambda b, j: (b, 0))],
        out_shape=[jax.ShapeDtypeStruct((bsz * seq, GLA_V_W), F32),
                   jax.ShapeDtypeStruct((bsz * N_META, GLA_V_W), F32)],
        scratch_shapes=[pltpu.VMEM((GLA_HEADS, GLA_DV, GLA_DK), F32)],
        compiler_params=_params(("parallel", "arbitrary")),
        name="gla_scan_rev" if rev else "gla_scan_fwd",
    )(*args)


def _gla_out_body(n_real_tiles, o_ref, om_ref, r_ref, h_ref, hn_ref, wout_ref, out_ref):
    is_meta = pl.program_id(0) == n_real_tiles
    o = jnp.where(is_meta, om_ref[...], o_ref[...])
    r = r_ref[...].astype(F32)
    gate = r * jax.nn.sigmoid(r)
    cols = []
    for h in range(GLA_HEADS):
        vs = slice(h * GLA_DV, (h + 1) * GLA_DV)
        cols.append(_rms(o[:, vs], hn_ref[...]) * gate[:, vs])
    y = jnp.concatenate(cols, axis=1).astype(BF16)
    out_ref[...] = h_ref[...] + _dot(y, wout_ref[...])


def _gla_out(o_real, o_meta, r, h, head_norm, w_out):
    n_rows = h.shape[0]
    n_real_tiles = o_real.shape[0] // ROW_TILE
    row = pl.BlockSpec((ROW_TILE, D_MODEL), lambda i: (i, 0))
    return pl.pallas_call(
        functools.partial(_gla_out_body, n_real_tiles),
        grid=(n_rows // ROW_TILE,),
        in_specs=[pl.BlockSpec((ROW_TILE, GLA_V_W), lambda i: (jnp.minimum(i, n_real_tiles - 1), 0)),
                  _const_spec(o_meta.shape), row, row,
                  _const_spec((1, GLA_DV)), _const_spec(w_out.shape)],
        out_specs=row,
        out_shape=jax.ShapeDtypeStruct((n_rows, D_MODEL), F32),
        compiler_params=_params(("parallel",)),
        name="gla_out_proj",
    )(o_real, o_meta, r, h, head_norm, w_out)


def _rope(x, cos, sin_signed, first_half):
    partner = jnp.where(first_half, pltpu.roll(x, LANES - ROPE_AXIS_DIM // 2, 1),
                        pltpu.roll(x, ROPE_AXIS_DIM // 2, 1))
    return x * cos + partner * sin_signed


def _attn_in_body(h_ref, gain_ref, win_ref, qn_ref, kn_ref, cos_ref, sin_ref, q_ref, k_ref, kt_ref, v_ref):
    xn = _rms(h_ref[...], gain_ref[...]).astype(BF16)
    cos = cos_ref[...]
    sin = sin_ref[...]
    lane = lax.broadcasted_iota(jnp.int32, cos.shape, 1)
    first_half = (lane % ROPE_AXIS_DIM) < (ROPE_AXIS_DIM // 2)
    q = _dot(xn, win_ref[:, 0:ATT_Q_W])
    for h in range(ATT_Q_HEADS):
        hs = slice(h * ATT_HEAD_DIM, (h + 1) * ATT_HEAD_DIM)
        qh = _rope(_rms(q[:, hs], qn_ref[...]), cos, sin, first_half) * (ATT_HEAD_DIM ** -0.5 * LOG2_E)
        q_ref[:, hs] = qh.astype(BF16)
    k = _dot(xn, win_ref[:, ATT_Q_W:ATT_Q_W + ATT_KV_W])
    for h in range(ATT_KV_HEADS):
        hs = slice(h * ATT_HEAD_DIM, (h + 1) * ATT_HEAD_DIM)
        kh = _rope(_rms(k[:, hs], kn_ref[...]), cos, sin, first_half)
        k_ref[:, hs] = kh.astype(BF16)
        kt_ref[hs, :] = kh.T.astype(BF16)
    v = _dot(xn, win_ref[:, ATT_Q_W + ATT_KV_W:]).astype(BF16)
    ones = jnp.ones((v.shape[0], ATT_HEAD_DIM), BF16)
    for h in range(ATT_KV_HEADS):
        v_ref[:, 2 * h * ATT_HEAD_DIM:(2 * h + 1) * ATT_HEAD_DIM] = v[:, h * ATT_HEAD_DIM:(h + 1) * ATT_HEAD_DIM]
        v_ref[:, (2 * h + 1) * ATT_HEAD_DIM:(2 * h + 2) * ATT_HEAD_DIM] = ones


def _rope_tables(seq):
    pos = jnp.arange(seq)
    inv_freq = ROPE_THETA ** (-jnp.arange(0, ROPE_AXIS_DIM, 2, dtype=F32) / ROPE_AXIS_DIM)
    ang_row = (pos // GRID_W).astype(F32)[:, None] * inv_freq[None, :]
    ang_col = (pos % GRID_W).astype(F32)[:, None] * inv_freq[None, :]
    ang = jnp.concatenate([ang_row, ang_row, ang_col, ang_col], axis=1)
    ang = jnp.concatenate([ang, jnp.zeros((ROW_TILE, ATT_HEAD_DIM), F32)], axis=0)
    sign = jnp.tile(jnp.concatenate([-jnp.ones((ROPE_AXIS_DIM // 2,), F32),
                                     jnp.ones((ROPE_AXIS_DIM // 2,), F32)]), 2)
    return jnp.cos(ang), jnp.sin(ang) * sign[None, :]


def _attn_in(h, gain, w_in, q_norm, k_norm, cos, sin, seq):
    n_rows = h.shape[0]
    n_real_tiles = (n_rows - (cos.shape[0] - seq)) // ROW_TILE
    tiles_per_seq = seq // ROW_TILE
    spec = lambda w: pl.BlockSpec((ROW_TILE, w), lambda i: (i, 0))
    table = pl.BlockSpec((ROW_TILE, ATT_HEAD_DIM),
                         lambda i: (jnp.where(i < n_real_tiles, i % tiles_per_seq, tiles_per_seq), 0))
    shp = lambda w: jax.ShapeDtypeStruct((n_rows, w), BF16)
    return pl.pallas_call(
        _attn_in_body,
        grid=(n_rows // ROW_TILE,),
        in_specs=[spec(D_MODEL), _const_spec((1, D_MODEL)), _const_spec(w_in.shape),
                  _const_spec((1, ATT_HEAD_DIM)), _const_spec((1, ATT_HEAD_DIM)), table, table],
        out_specs=[spec(ATT_Q_W), spec(ATT_KV_W), pl.BlockSpec((ATT_KV_W, ROW_TILE), lambda i: (0, i)),
                   spec(2 * ATT_KV_W)],
        out_shape=[shp(ATT_Q_W), shp(ATT_KV_W), jax.ShapeDtypeStruct((ATT_KV_W, n_rows), BF16),
                   shp(2 * ATT_KV_W)],
        compiler_params=_params(("parallel",)),
        name="attn_in_proj",
    )(h, gain, w_in, q_norm, k_norm, cos, sin)


def _attn_scores_block(q, kt_ref, kb, s_ref, mx):
    cols = slice(kb * ATT_KV_BLOCK, (kb + 1) * ATT_KV_BLOCK)
    s = _dot(q, kt_ref[:, cols])
    s_ref[kb] = s
    tiles = [s[:, c * LANES:(c + 1) * LANES] for c in range(ATT_KV_BLOCK // LANES)]
    while len(tiles) > 1:
        tiles = [jnp.maximum(tiles[i], tiles[i + 1]) for i in range(0, len(tiles), 2)]
    return tiles[0] if mx is None else jnp.maximum(mx, tiles[0])


def _attn_scores_finish(q, km, mx, sm_ref, m_ref):
    sm = _dot_nt(q, km)
    sm_ref[...] = sm
    m = jnp.maximum(jnp.max(mx, axis=1, keepdims=True), jnp.max(sm, axis=1, keepdims=True))
    m_ref[...] = jnp.broadcast_to(m, m_ref.shape)


def _attn_weighted_start(sm_ref, m_ref, vm):
    m = m_ref[...]
    return m, _dot(jnp.exp2(sm_ref[...] - m[:, 0:N_META]).astype(BF16), vm)


def _attn_weighted_block(s_ref, kb, m, v_ref, acc):
    rows = slice(kb * ATT_KV_BLOCK, (kb + 1) * ATT_KV_BLOCK)
    p = [jnp.exp2(s_ref[kb, :, c * LANES:(c + 1) * LANES] - m) for c in range(ATT_KV_BLOCK // LANES)]
    return acc + _dot(jnp.concatenate(p, axis=1).astype(BF16), v_ref[rows, :])


def _attn_core_body(n_kv_blocks, n_q, n_tiles, q_ref, qm_ref, kt_ref, km_ref, v_ref, vm_ref, o_ref, om_ref,
                    s0_ref, sm0_ref, m0_ref, s1_ref, sm1_ref, m1_ref, sq_ref, smq_ref, mq_ref):
    t = pl.program_id(0)
    km = km_ref[...]
    vm = vm_ref[...]

    def stack(ref):
        return jnp.concatenate([ref[:, g * ATT_HEAD_DIM:(g + 1) * ATT_HEAD_DIM] for g in range(ATT_GROUP)], axis=0)

    def unstack(acc, ref):
        o = acc[:, 0:ATT_HEAD_DIM] / acc[:, ATT_HEAD_DIM:]
        rows = ref.shape[0]
        for g in range(ATT_GROUP):
            ref[:, g * ATT_HEAD_DIM:(g + 1) * ATT_HEAD_DIM] = o[g * rows:(g + 1) * rows, :].astype(BF16)

    @pl.when(t == 0)
    def _():
        s1_ref[...] = jnp.zeros_like(s1_ref)
        sm1_ref[...] = jnp.zeros_like(sm1_ref)
        m1_ref[...] = jnp.zeros_like(m1_ref)

    def step(write, read):
        ws_ref, wsm_ref, wm_ref = write
        rs_ref, rsm_ref, rm_ref = read
        q = stack(q_ref)
        m, acc = _attn_weighted_start(rsm_ref, rm_ref, vm)
        mx = None
        for kb in range(n_kv_blocks):
            acc = _attn_weighted_block(rs_ref, kb, m, v_ref, acc)
            mx = _attn_scores_block(q, kt_ref, kb, ws_ref, mx)
        _attn_scores_finish(q, km, mx, wsm_ref, wm_ref)
        unstack(acc, o_ref)

    pl.when(t % 2 == 0)(lambda: step((s0_ref, sm0_ref, m0_ref), (s1_ref, sm1_ref, m1_ref)))
    pl.when(t % 2 == 1)(lambda: step((s1_ref, sm1_ref, m1_ref), (s0_ref, sm0_ref, m0_ref)))

    @pl.when(jnp.minimum(t, n_tiles - 1) % n_q == n_q - 1)
    def _():
        q = stack(qm_ref)
        mx = None
        for kb in range(n_kv_blocks):
            mx = _attn_scores_block(q, kt_ref, kb, sq_ref, mx)
        _attn_scores_finish(q, km, mx, smq_ref, mq_ref)
        m, acc = _attn_weighted_start(smq_ref, mq_ref, vm)
        for kb in range(n_kv_blocks):
            acc = _attn_weighted_block(sq_ref, kb, m, v_ref, acc)
        unstack(acc, om_ref)


def _attn_core(bsz, seq, q, k, kt, v):
    n_q = seq // ATT_Q_TILE
    assert n_q >= 2
    n_tiles = bsz * ATT_KV_HEADS * n_q
    meta_blk0 = bsz * seq // N_META
    n_kv_blocks = seq // ATT_KV_BLOCK
    group_w = ATT_GROUP * ATT_HEAD_DIM
    m_rows = ATT_GROUP * ATT_Q_TILE
    mq_rows = ATT_GROUP * N_META

    def coords(tile):
        return tile // (ATT_KV_HEADS * n_q), (tile // n_q) % ATT_KV_HEADS, tile % n_q

    def spec(shape, which, fn):
        tile = (lambda t: jnp.minimum(t, n_tiles - 1)) if which == "cur" else (lambda t: jnp.maximum(t - 1, 0))
        return pl.BlockSpec(shape, lambda t: fn(*coords(tile(t))))

    stash = lambda rows: [pltpu.VMEM((n_kv_blocks, rows, ATT_KV_BLOCK), F32), pltpu.VMEM((rows, N_META), F32),
                          pltpu.VMEM((rows, LANES), F32)]
    return pl.pallas_call(
        functools.partial(_attn_core_body, n_kv_blocks, n_q, n_tiles),
        grid=(n_tiles + 1,),
        in_specs=[spec((ATT_Q_TILE, group_w), "cur", lambda b, kh, i: (b * n_q + i, kh)),
                  spec((N_META, group_w), "cur", lambda b, kh, i: (meta_blk0 + b, kh)),
                  spec((ATT_HEAD_DIM, seq), "cur", lambda b, kh, i: (kh, b)),
                  spec((N_META, ATT_HEAD_DIM), "cur", lambda b, kh, i: (meta_blk0 + b, kh)),
                  spec((seq, 2 * ATT_HEAD_DIM), "prev", lambda b, kh, i: (b, kh)),
                  spec((N_META, 2 * ATT_HEAD_DIM), "prev", lambda b, kh, i: (meta_blk0 + b, kh))],
        out_specs=[spec((ATT_Q_TILE, group_w), "prev", lambda b, kh, i: (b * n_q + i, kh)),
                   spec((N_META, group_w), "cur", lambda b, kh, i: (b, kh))],
        out_shape=[jax.ShapeDtypeStruct((bsz * seq, ATT_Q_W), BF16),
                   jax.ShapeDtypeStruct((bsz * N_META, ATT_Q_W), BF16)],
        scratch_shapes=stash(m_rows) + stash(m_rows) + stash(mq_rows),
        compiler_params=_params(("arbitrary",)),
        name="attn_core",
    )(q, q, kt, k, v, v)


def _attn_out_body(n_real_tiles, o_ref, om_ref, h_ref, wout_ref, out_ref):
    is_meta = pl.program_id(0) == n_real_tiles
    o = jnp.where(is_meta, om_ref[...], o_ref[...])
    out_ref[...] = h_ref[...] + _dot(o, wout_ref[...])


def _attn_out(o_real, o_meta, h, w_out):
    n_rows = h.shape[0]
    n_real_tiles = o_real.shape[0] // ROW_TILE
    row = pl.BlockSpec((ROW_TILE, D_MODEL), lambda i: (i, 0))
    return pl.pallas_call(
        functools.partial(_attn_out_body, n_real_tiles),
        grid=(n_rows // ROW_TILE,),
        in_specs=[pl.BlockSpec((ROW_TILE, ATT_Q_W), lambda i: (jnp.minimum(i, n_real_tiles - 1), 0)),
                  _const_spec(o_meta.shape), row, _const_spec(w_out.shape)],
        out_specs=row,
        out_shape=jax.ShapeDtypeStruct((n_rows, D_MODEL), F32),
        compiler_params=_params(("parallel",)),
        name="attn_out_proj",
    )(o_real, o_meta, h, w_out)


def kernel(x, meta_tokens, norm_ffn1, ffn1_w_gate, ffn1_w_up, ffn1_w_down, norm_mix, gla_w_in, gla_gate_w1, gla_gate_w2, gla_gate_b, gla_head_norm, gla_w_out, attn_w_in, attn_q_norm, attn_k_norm, attn_w_out, norm_ffn2, ffn2_w_gate, ffn2_w_up, ffn2_w_down, norm_final):
    bsz, seq, d = x.shape
    depth = norm_ffn1.shape[0]
    assert d == D_MODEL and bsz * N_META == ROW_TILE and seq % GLA_BLOCK == 0 and seq % ATT_KV_BLOCK == 0
    n_real = bsz * seq
    row = lambda g: g.reshape(1, -1).astype(F32)

    h = x.reshape(n_real, d)
    h_meta = jnp.broadcast_to(meta_tokens.astype(x.dtype)[None], (bsz, N_META, d)).reshape(bsz * N_META, d)
    cos, sin = _rope_tables(seq)

    for i in range(depth):
        h = _ffn(h, row(norm_ffn1[i]), ffn1_w_gate[i].astype(BF16), ffn1_w_up[i].astype(BF16),
                 ffn1_w_down[i].astype(BF16), h_meta=h_meta if i == 0 else None)
        j = i // 2
        if i % 2 == 0:
            w1 = jnp.concatenate([gla_gate_w1[j, 0], gla_gate_w1[j, 1]], axis=1).astype(BF16)
            zero = jnp.zeros_like(gla_gate_w2[j, 0])
            w2 = jnp.concatenate([jnp.concatenate([gla_gate_w2[j, 0], zero], axis=1),
                                  jnp.concatenate([zero, gla_gate_w2[j, 1]], axis=1)], axis=0).astype(BF16)
            q, k, v, r, lgf, lgb = _gla_in(h, row(norm_mix[i]), gla_w_in[j].astype(BF16), w1, w2,
                                           gla_gate_b[j].reshape(1, -1).astype(F32))
            o_fwd = _gla_scan(False, bsz, seq, q, k, v, lgf)
            o_real, o_meta = _gla_scan(True, bsz, seq, q, k, v, lgb, o_fwd)
            h = _gla_out(o_real, o_meta, r, h, row(gla_head_norm[j]), gla_w_out[j].astype(BF16))
        else:
            q, k, kt, v = _attn_in(h, row(norm_mix[i]), attn_w_in[j].astype(BF16), row(attn_q_norm[j]),
                                   row(attn_k_norm[j]), cos, sin, seq)
            o_real, o_meta = _attn_core(bsz, seq, q, k, kt, v)
            h = _attn_out(o_real, o_meta, h, attn_w_out[j].astype(BF16))
        last = i == depth - 1
        h = _ffn(h, row(norm_ffn2[i]), ffn2_w_gate[i].astype(BF16), ffn2_w_up[i].astype(BF16),
                 ffn2_w_down[i].astype(BF16), final_gain=row(norm_final) if last else None,
                 n_rows=n_real if last else None)
    return h.reshape(bsz, seq, d)
```

```python
import functools

import jax
import jax.numpy as jnp
from jax import lax
from jax.experimental import pallas as pl
from jax.experimental.pallas import tpu as pltpu

F32 = jnp.float32
BF16 = jnp.bfloat16

D_MODEL = 1024
N_META = 16
GRID_W = 64
D_FF = 2816
NORM_EPS = 1e-6
MACARON_WEIGHT = 0.5
N_MIXERS = 2

GLA_HEADS = 4
GLA_DK = D_MODEL // 2 // GLA_HEADS
GLA_DV = D_MODEL // GLA_HEADS
GLA_QK_W = GLA_HEADS * GLA_DK
GLA_V_W = GLA_HEADS * GLA_DV
GLA_GATE_RANK = 16
GLA_GATE_TAU = 16.0
GLA_CHUNK = 64

ATT_Q_HEADS = 8
ATT_KV_HEADS = 2
ATT_HEAD_DIM = D_MODEL // ATT_Q_HEADS
ATT_GROUP = ATT_Q_HEADS // ATT_KV_HEADS
ATT_Q_W = ATT_Q_HEADS * ATT_HEAD_DIM
ATT_KV_W = ATT_KV_HEADS * ATT_HEAD_DIM
ROPE_THETA = 10000.0
ROPE_AXIS_DIM = ATT_HEAD_DIM // 2
LOG2_E = 1.4426950408889634

LANES = 128
ROW_TILE = 256
STEP_TILES = 2
STEP_ROWS = STEP_TILES * ROW_TILE
FF_CHUNK = 256
GLA_BLOCK = 512
ATT_Q_TILE = 256
ATT_KV_BLOCK = 512
VMEM_LIMIT = 56 * 1024 * 1024


def _params(semantics):
    return pltpu.CompilerParams(dimension_semantics=semantics, vmem_limit_bytes=VMEM_LIMIT)


def _const_spec(shape):
    nd = len(shape)
    return pl.BlockSpec(shape, lambda *_: (0,) * nd, pipeline_mode=pl.Buffered(1))


def _rms(x, gain):
    ms = jnp.mean(x * x, axis=-1, keepdims=True)
    return x * lax.rsqrt(ms + NORM_EPS) * gain


def _dot(a, b):
    return jnp.dot(a, b, preferred_element_type=F32)


def _dot_nt(a, b):
    return lax.dot_general(a, b, (((1,), (1,)), ((), ())), preferred_element_type=F32)


def _dot_tn(a, b):
    return lax.dot_general(a, b, (((0,), (0,)), ((), ())), preferred_element_type=F32)


def _call(body, grid, ins, outs, scratch, semantics, name):
    keys = [k for k, _, _ in ins] + [k for k, _, _ in outs] + [k for k, _ in scratch]

    def kernel_body(*refs):
        body(dict(zip(keys, refs, strict=True)))

    results = pl.pallas_call(
        kernel_body,
        grid=grid,
        in_specs=[s for _, _, s in ins],
        out_specs=[s for _, _, s in outs],
        out_shape=[a for _, a, _ in outs],
        scratch_shapes=[s for _, s in scratch],
        compiler_params=_params(semantics),
        name=name,
    )(*[a for _, a, _ in ins])
    return dict(zip([k for k, _, _ in outs], results, strict=True))


def _row_spec(width, clamp_steps=None):
    if clamp_steps is None:
        return pl.BlockSpec((STEP_ROWS, width), lambda i: (i, 0))
    return pl.BlockSpec((STEP_ROWS, width), lambda i: (jnp.minimum(i, clamp_steps - 1), 0))


def _tile_rows(tile):
    return slice(tile * ROW_TILE, (tile + 1) * ROW_TILE)


def _meta_tile(tile, real, meta_ref, n_real_steps):
    if tile > 0 or meta_ref is None:
        return real
    return jnp.where(pl.program_id(0) == n_real_steps, meta_ref[...], real)


def _swiglu(x, refs, a_ref):
    xn = _rms(x, refs["ffn_gain"][...]).astype(BF16)
    for c in range(D_FF // FF_CHUNK):
        cols = slice(c * FF_CHUNK, (c + 1) * FF_CHUNK)
        g = _dot(xn, refs["w_gate"][:, cols])
        u = _dot(xn, refs["w_up"][:, cols])
        a_ref[:, cols] = (g * jax.nn.sigmoid(g) * u).astype(BF16)
    return x + MACARON_WEIGHT * _dot(a_ref[...], refs["w_down"][...])


def _gla_in(y, refs, rows):
    xn = _rms(y, refs["mix_gain"][...]).astype(BF16)
    win_ref = refs["w_in"]
    refs["q"][rows, :] = (_dot(xn, win_ref[:, 0:GLA_QK_W]) * (GLA_DK ** -0.5)).astype(BF16)
    refs["k"][rows, :] = _dot(xn, win_ref[:, GLA_QK_W:2 * GLA_QK_W]).astype(BF16)
    refs["v"][rows, :] = _dot(xn, win_ref[:, 2 * GLA_QK_W:2 * GLA_QK_W + GLA_V_W]).astype(BF16)
    refs["r"][rows, :] = _dot(xn, win_ref[:, 2 * GLA_QK_W + GLA_V_W:]).astype(BF16)
    t = _dot(xn, refs["gate_w1"][...]).astype(BF16)
    z = _dot(t, refs["gate_w2"][...]) + refs["gate_b"][...]
    logg = -(jnp.maximum(-z, 0.0) + jnp.log1p(jnp.exp(-jnp.abs(z)))) / GLA_GATE_TAU
    refs["lg_fwd"][rows, :] = logg[:, 0:GLA_QK_W]
    refs["lg_rev"][rows, :] = logg[:, GLA_QK_W:]


def _gla_out(o, r, h, refs):
    gate = r * jax.nn.sigmoid(r)
    cols = []
    for hd in range(GLA_HEADS):
        vs = slice(hd * GLA_DV, (hd + 1) * GLA_DV)
        cols.append(_rms(o[:, vs], refs["head_gain"][...]) * gate[:, vs])
    return h + _dot(jnp.concatenate(cols, axis=1).astype(BF16), refs["w_out"][...])


def _rope(x, cos, sin_signed, first_half):
    partner = jnp.where(first_half, pltpu.roll(x, LANES - ROPE_AXIS_DIM // 2, 1),
                        pltpu.roll(x, ROPE_AXIS_DIM // 2, 1))
    return x * cos + partner * sin_signed


def _attn_in(y, refs, rows):
    xn = _rms(y, refs["mix_gain"][...]).astype(BF16)
    win_ref = refs["w_in"]
    cos = refs["cos"][rows, :]
    sin = refs["sin"][rows, :]
    lane = lax.broadcasted_iota(jnp.int32, cos.shape, 1)
    first_half = (lane % ROPE_AXIS_DIM) < (ROPE_AXIS_DIM // 2)
    q = _dot(xn, win_ref[:, 0:ATT_Q_W])
    for hd in range(ATT_Q_HEADS):
        hs = slice(hd * ATT_HEAD_DIM, (hd + 1) * ATT_HEAD_DIM)
        qh = _rope(_rms(q[:, hs], refs["q_gain"][...]), cos, sin, first_half) * (ATT_HEAD_DIM ** -0.5 * LOG2_E)
        refs["q"][rows, hs] = qh.astype(BF16)
    k = _dot(xn, win_ref[:, ATT_Q_W:ATT_Q_W + ATT_KV_W])
    v = _dot(xn, win_ref[:, ATT_Q_W + ATT_KV_W:]).astype(BF16)
    ones = jnp.ones((v.shape[0], ATT_HEAD_DIM), BF16)
    for hd in range(ATT_KV_HEADS):
        hs = slice(hd * ATT_HEAD_DIM, (hd + 1) * ATT_HEAD_DIM)
        kh = _rope(_rms(k[:, hs], refs["k_gain"][...]), cos, sin, first_half)
        refs["k"][rows, hs] = kh.astype(BF16)
        refs["kt"][hs, rows] = kh.T.astype(BF16)
        refs["v"][rows, 2 * hd * ATT_HEAD_DIM:(2 * hd + 1) * ATT_HEAD_DIM] = v[:, hs]
        refs["v"][rows, (2 * hd + 1) * ATT_HEAD_DIM:(2 * hd + 2) * ATT_HEAD_DIM] = ones


def _rope_tables(seq):
    pos = jnp.arange(seq)
    inv_freq = ROPE_THETA ** (-jnp.arange(0, ROPE_AXIS_DIM, 2, dtype=F32) / ROPE_AXIS_DIM)
    ang_row = (pos // GRID_W).astype(F32)[:, None] * inv_freq[None, :]
    ang_col = (pos % GRID_W).astype(F32)[:, None] * inv_freq[None, :]
    ang = jnp.concatenate([ang_row, ang_row, ang_col, ang_col], axis=1)
    ang = jnp.concatenate([ang, jnp.zeros((STEP_ROWS, ATT_HEAD_DIM), F32)], axis=0)
    sign = jnp.tile(jnp.concatenate([-jnp.ones((ROPE_AXIS_DIM // 2,), F32),
                                     jnp.ones((ROPE_AXIS_DIM // 2,), F32)]), 2)
    return jnp.cos(ang), jnp.sin(ang) * sign[None, :]


def _act_scratch():
    return [(f"act{tile}", pltpu.VMEM((ROW_TILE, D_FF), BF16)) for tile in range(STEP_TILES)]


def _ffn_operands(gain, wg, wu, wd):
    return [("ffn_gain", gain, _const_spec((1, D_MODEL))), ("w_gate", wg, _const_spec((D_MODEL, D_FF))),
            ("w_up", wu, _const_spec((D_MODEL, D_FF))), ("w_down", wd, _const_spec((D_FF, D_MODEL)))]


def _stage_in_body(mixer, n_real_steps, refs):
    for tile in range(STEP_TILES):
        rows = _tile_rows(tile)
        x = _meta_tile(tile, refs["h"][rows, :], refs.get("h_meta"), n_real_steps)
        y = _swiglu(x, refs, refs[f"act{tile}"])
        refs["h_out"][rows, :] = y
        (_gla_in if mixer == "gla" else _attn_in)(y, refs, rows)


def _token_stage_in(mixer, h, h_meta, ffn, mix_gain, mixer_ins, seq):
    n_real_steps = h.shape[0] // STEP_ROWS
    n_rows = h.shape[0] + (h_meta.shape[0] if h_meta is not None else 0)
    ins = [("h", h, _row_spec(D_MODEL, n_real_steps if h_meta is not None else None))]
    if h_meta is not None:
        assert h_meta.shape[0] == ROW_TILE
        ins.append(("h_meta", h_meta, _const_spec((ROW_TILE, D_MODEL))))
    ins += _ffn_operands(*ffn) + [("mix_gain", mix_gain, _const_spec((1, D_MODEL)))]
    ins += [(key, arr, _const_spec(arr.shape)) for key, arr in mixer_ins]
    shp = lambda w, dt: jax.ShapeDtypeStruct((n_rows, w), dt)
    outs = [("h_out", shp(D_MODEL, F32), _row_spec(D_MODEL))]
    if mixer == "gla":
        outs += [("q", shp(GLA_QK_W, BF16), _row_spec(GLA_QK_W)), ("k", shp(GLA_QK_W, BF16), _row_spec(GLA_QK_W)),
                 ("v", shp(GLA_V_W, BF16), _row_spec(GLA_V_W)), ("r", shp(GLA_V_W, BF16), _row_spec(GLA_V_W)),
                 ("lg_fwd", shp(GLA_QK_W, F32), _row_spec(GLA_QK_W)),
                 ("lg_rev", shp(GLA_QK_W, F32), _row_spec(GLA_QK_W))]
    else:
        n_seq_steps = seq // STEP_ROWS
        real_steps = (n_rows - ROW_TILE) // STEP_ROWS
        table = pl.BlockSpec((STEP_ROWS, ATT_HEAD_DIM),
                             lambda i: (jnp.where(i < real_steps, i % n_seq_steps, n_seq_steps), 0))
        cos, sin = _rope_tables(seq)
        ins += [("cos", cos, table), ("sin", sin, table)]
        outs += [("q", shp(ATT_Q_W, BF16), _row_spec(ATT_Q_W)), ("k", shp(ATT_KV_W, BF16), _row_spec(ATT_KV_W)),
                 ("kt", jax.ShapeDtypeStruct((ATT_KV_W, n_rows), BF16),
                  pl.BlockSpec((ATT_KV_W, STEP_ROWS), lambda i: (0, i))),
                 ("v", shp(2 * ATT_KV_W, BF16), _row_spec(2 * ATT_KV_W))]
    return _call(functools.partial(_stage_in_body, mixer, n_real_steps), (pl.cdiv(n_rows, STEP_ROWS),), ins, outs,
                 _act_scratch(), ("parallel",), f"token_stage_in_{mixer}")


def _stage_out_body(mixer, n_real_steps, final_norm, refs):
    for tile in range(STEP_TILES):
        rows = _tile_rows(tile)
        h = refs["h"][rows, :]
        if mixer == "gla":
            o = (_meta_tile(tile, refs["o_fwd"][rows, :], refs.get("o_fwd_meta"), n_real_steps)
                 + _meta_tile(tile, refs["o_rev"][rows, :], refs.get("o_rev_meta"), n_real_steps))
            x = _gla_out(o, refs["r"][rows, :].astype(F32), h, refs)
        else:
            o = _meta_tile(tile, refs["o"][rows, :], refs.get("o_meta"), n_real_steps)
            x = h + _dot(o, refs["w_out"][...])
        y = _swiglu(x, refs, refs[f"act{tile}"])
        refs["h_out"][rows, :] = _rms(y, refs["final_gain"][...]) if final_norm else y


def _token_stage_out(mixer, core, h, w_out, ffn, extra, final_gain, n_real):
    last = final_gain is not None
    n_real_steps = n_real // STEP_ROWS
    n_rows = n_real if last else h.shape[0]
    has_meta = not last
    ins = [("h", h, _row_spec(D_MODEL))]
    if mixer == "gla":
        ins += [("o_fwd", core["o_fwd"], _row_spec(GLA_V_W, n_real_steps)),
                ("o_rev", core["o_rev"], _row_spec(GLA_V_W, n_real_steps)),
                ("r", extra["r"], _row_spec(GLA_V_W)), ("head_gain", extra["head_gain"], _const_spec((1, GLA_DV)))]
        if has_meta:
            ins += [("o_fwd_meta", core["o_fwd_meta"], _const_spec((ROW_TILE, GLA_V_W))),
                    ("o_rev_meta", core["o_rev_meta"], _const_spec((ROW_TILE, GLA_V_W)))]
    else:
        ins += [("o", core["o"], _row_spec(ATT_Q_W, n_real_steps))]
        if has_meta:
            ins += [("o_meta", core["o_meta"], _const_spec((ROW_TILE, ATT_Q_W)))]
    ins += [("w_out", w_out, _const_spec(w_out.shape))] + _ffn_operands(*ffn)
    if last:
        ins.append(("final_gain", final_gain, _const_spec((1, D_MODEL))))
    outs = [("h_out", jax.ShapeDtypeStruct((n_rows, D_MODEL), F32), _row_spec(D_MODEL))]
    return _call(functools.partial(_stage_out_body, mixer, n_real_steps, last), (pl.cdiv(n_rows, STEP_ROWS),),
                 ins, outs, _act_scratch(), ("parallel",), f"token_stage_out_{mixer}")["h_out"]


def _gla_chunk(rev, q, k, v, lg, states):
    t_len = q.shape[0]
    row = lax.broadcasted_iota(jnp.int32, (t_len, t_len), 0)
    col = lax.broadcasted_iota(jnp.int32, (t_len, t_len), 1)
    mask = (col >= row) if rev else (col <= row)
    row3 = lax.broadcasted_iota(jnp.int32, (t_len, 3 * t_len), 0)
    col3 = lax.broadcasted_iota(jnp.int32, (t_len, 3 * t_len), 1) % t_len
    mask3 = ((col3 >= row3) if rev else (col3 <= row3)).astype(BF16)
    hi = lg.astype(BF16)
    rest = lg - hi.astype(F32)
    mid = rest.astype(BF16)
    lo = (rest - mid.astype(F32)).astype(BF16)
    b = _dot(mask3, jnp.concatenate([hi, mid, lo], axis=0))
    b_tot = b[0:1, :] if rev else b[t_len - 1:t_len, :]
    qd = (q * jnp.exp(b)).astype(BF16)
    ki = (k * jnp.exp(-b)).astype(BF16)
    ke = (k * jnp.exp(b_tot - b)).astype(BF16)
    dec = jnp.exp(b_tot)
    outs, new_states = [], []
    for h in range(GLA_HEADS):
        ks = slice(h * GLA_DK, (h + 1) * GLA_DK)
        vs = slice(h * GLA_DV, (h + 1) * GLA_DV)
        a = jnp.where(mask, _dot_nt(qd[:, ks], ki[:, ks]), 0.0).astype(BF16)
        st = states[h]
        outs.append(_dot(a, v[:, vs]) + _dot_nt(qd[:, ks], st.astype(BF16)))
        new_states.append(dec[:, ks] * st + _dot_tn(v[:, vs], ke[:, ks]))
    return jnp.concatenate(outs, axis=1), new_states


def _gla_scan_body(refs):
    j = pl.program_id(1)
    n_chunks = GLA_BLOCK // GLA_CHUNK
    pad = GLA_CHUNK - N_META
    state_refs = {False: refs["state_fwd"], True: refs["state_rev"]}

    @pl.when(j == 0)
    def _():
        for s_ref in state_refs.values():
            s_ref[...] = jnp.zeros_like(s_ref)

    def load_states(rev):
        return [state_refs[rev][h] for h in range(GLA_HEADS)]

    def store_states(rev, states):
        for h in range(GLA_HEADS):
            state_refs[rev][h] = states[h]

    def meta_chunk(rev):
        front = lambda ref: jnp.concatenate([jnp.zeros((pad, ref.shape[1]), ref.dtype), ref[...]], axis=0)
        lg_ref = refs["lg_rev_meta"] if rev else refs["lg_fwd_meta"]
        o, states = _gla_chunk(rev, front(refs["q_meta"]), front(refs["k_meta"]), front(refs["v_meta"]),
                               front(lg_ref), load_states(rev))
        store_states(rev, states)
        refs["o_rev_meta" if rev else "o_fwd_meta"][...] = o[pad:, :]

    pl.when(j == 0)(lambda: meta_chunk(False))

    states = {rev: load_states(rev) for rev in (False, True)}
    for ci in range(n_chunks):
        for rev in (False, True):
            c = n_chunks - 1 - ci if rev else ci
            rows = slice(c * GLA_CHUNK, (c + 1) * GLA_CHUNK)
            sfx = "_rev" if rev else "_fwd"
            o, states[rev] = _gla_chunk(rev, refs["q" + sfx][rows, :], refs["k" + sfx][rows, :],
                                        refs["v" + sfx][rows, :], refs["lg" + sfx][rows, :], states[rev])
            refs["o" + sfx][rows, :] = o
    for rev in (False, True):
        store_states(rev, states[rev])

    pl.when(j == pl.num_programs(1) - 1)(lambda: meta_chunk(True))


def _gla_scan(bsz, seq, t):
    n_blk = seq // GLA_BLOCK
    meta_blk0 = bsz * seq // N_META
    fwd = lambda w: pl.BlockSpec((GLA_BLOCK, w), lambda b, j: (b * n_blk + j, 0))
    rev = lambda w: pl.BlockSpec((GLA_BLOCK, w), lambda b, j: (b * n_blk + n_blk - 1 - j, 0))
    meta = lambda w: pl.BlockSpec((N_META, w), lambda b, j: (meta_blk0 + b, 0))
    meta_out = pl.BlockSpec((N_META, GLA_V_W), lambda b, j: (b, 0))
    ins = []
    for sfx, spec in (("_fwd", fwd), ("_rev", rev)):
        ins += [("q" + sfx, t["q"], spec(GLA_QK_W)), ("k" + sfx, t["k"], spec(GLA_QK_W)),
                ("v" + sfx, t["v"], spec(GLA_V_W)), ("lg" + sfx, t["lg" + sfx], spec(GLA_QK_W))]
    ins += [("q_meta", t["q"], meta(GLA_QK_W)), ("k_meta", t["k"], meta(GLA_QK_W)), ("v_meta", t["v"], meta(GLA_V_W)),
            ("lg_fwd_meta", t["lg_fwd"], meta(GLA_QK_W)), ("lg_rev_meta", t["lg_rev"], meta(GLA_QK_W))]
    real_shape = jax.ShapeDtypeStruct((bsz * seq, GLA_V_W), F32)
    meta_shape = jax.ShapeDtypeStruct((bsz * N_META, GLA_V_W), F32)
    outs = [("o_fwd", real_shape, fwd(GLA_V_W)), ("o_rev", real_shape, rev(GLA_V_W)),
            ("o_fwd_meta", meta_shape, meta_out), ("o_rev_meta", meta_shape, meta_out)]
    state = pltpu.VMEM((GLA_HEADS, GLA_DV, GLA_DK), F32)
    return _call(_gla_scan_body, (bsz, n_blk), ins, outs, [("state_fwd", state), ("state_rev", state)],
                 ("parallel", "arbitrary"), "gla_scan")


def _attn_scores_block(q, kt_ref, kb, s_ref, mx):
    cols = slice(kb * ATT_KV_BLOCK, (kb + 1) * ATT_KV_BLOCK)
    s = _dot(q, kt_ref[:, cols])
    s_ref[kb] = s
    tiles = [s[:, c * LANES:(c + 1) * LANES] for c in range(ATT_KV_BLOCK // LANES)]
    while len(tiles) > 1:
        tiles = [jnp.maximum(tiles[i], tiles[i + 1]) for i in range(0, len(tiles), 2)]
    return tiles[0] if mx is None else jnp.maximum(mx, tiles[0])


def _attn_scores_finish(q, km, mx, sm_ref, m_ref):
    sm = _dot_nt(q, km)
    sm_ref[...] = sm
    m = jnp.maximum(jnp.max(mx, axis=1, keepdims=True), jnp.max(sm, axis=1, keepdims=True))
    m_ref[...] = jnp.broadcast_to(m, m_ref.shape)


def _attn_weighted_start(sm_ref, m_ref, vm):
    m = m_ref[...]
    return m, _dot(jnp.exp2(sm_ref[...] - m[:, 0:N_META]).astype(BF16), vm)


def _attn_weighted_block(s_ref, kb, m, v_ref, acc):
    rows = slice(kb * ATT_KV_BLOCK, (kb + 1) * ATT_KV_BLOCK)
    p = [jnp.exp2(s_ref[kb, :, c * LANES:(c + 1) * LANES] - m) for c in range(ATT_KV_BLOCK // LANES)]
    return acc + _dot(jnp.concatenate(p, axis=1).astype(BF16), v_ref[rows, :])


def _attn_core_body(n_kv_blocks, n_q, n_tiles, meta_queries, refs):
    t = pl.program_id(0)
    kt_ref, v_ref = refs["kt"], refs["v"]
    km = refs["k_meta"][...]
    vm = refs["v_meta"][...]
    stash = [(refs[f"s{i}"], refs[f"sm{i}"], refs[f"m{i}"]) for i in range(2)]

    def stack(ref):
        return jnp.concatenate([ref[:, g * ATT_HEAD_DIM:(g + 1) * ATT_HEAD_DIM] for g in range(ATT_GROUP)], axis=0)

    def unstack(acc, ref):
        o = acc[:, 0:ATT_HEAD_DIM] / acc[:, ATT_HEAD_DIM:]
        rows = ref.shape[0]
        for g in range(ATT_GROUP):
            ref[:, g * ATT_HEAD_DIM:(g + 1) * ATT_HEAD_DIM] = o[g * rows:(g + 1) * rows, :].astype(BF16)

    @pl.when(t == 0)
    def _():
        for ref in stash[1]:
            ref[...] = jnp.zeros_like(ref)

    def step(write, read):
        ws_ref, wsm_ref, wm_ref = write
        rs_ref, rsm_ref, rm_ref = read
        q = stack(refs["q"])
        m, acc = _attn_weighted_start(rsm_ref, rm_ref, vm)
        mx = None
        for kb in range(n_kv_blocks):
            acc = _attn_weighted_block(rs_ref, kb, m, v_ref, acc)
            mx = _attn_scores_block(q, kt_ref, kb, ws_ref, mx)
        _attn_scores_finish(q, km, mx, wsm_ref, wm_ref)
        unstack(acc, refs["o"])

    pl.when(t % 2 == 0)(lambda: step(stash[0], stash[1]))
    pl.when(t % 2 == 1)(lambda: step(stash[1], stash[0]))

    if meta_queries:
        @pl.when(jnp.minimum(t, n_tiles - 1) % n_q == n_q - 1)
        def _():
            sq_ref, smq_ref, mq_ref = refs["sq"], refs["smq"], refs["mq"]
            q = stack(refs["q_meta"])
            mx = None
            for kb in range(n_kv_blocks):
                mx = _attn_scores_block(q, kt_ref, kb, sq_ref, mx)
            _attn_scores_finish(q, km, mx, smq_ref, mq_ref)
            m, acc = _attn_weighted_start(smq_ref, mq_ref, vm)
            for kb in range(n_kv_blocks):
                acc = _attn_weighted_block(sq_ref, kb, m, v_ref, acc)
            unstack(acc, refs["o_meta"])


def _attn_core(bsz, seq, t, meta_queries):
    n_q = seq // ATT_Q_TILE
    assert n_q >= 2
    n_tiles = bsz * ATT_KV_HEADS * n_q
    meta_blk0 = bsz * seq // N_META
    n_kv_blocks = seq // ATT_KV_BLOCK
    group_w = ATT_GROUP * ATT_HEAD_DIM

    def coords(tile):
        return tile // (ATT_KV_HEADS * n_q), (tile // n_q) % ATT_KV_HEADS, tile % n_q

    def spec(shape, which, fn):
        tile = (lambda s: jnp.minimum(s, n_tiles - 1)) if which == "cur" else (lambda s: jnp.maximum(s - 1, 0))
        return pl.BlockSpec(shape, lambda s: fn(*coords(tile(s))))

    def stash(sfx, rows):
        return [("s" + sfx, pltpu.VMEM((n_kv_blocks, rows, ATT_KV_BLOCK), F32)),
                ("sm" + sfx, pltpu.VMEM((rows, N_META), F32)), ("m" + sfx, pltpu.VMEM((rows, LANES), F32))]

    ins = [("q", t["q"], spec((ATT_Q_TILE, group_w), "cur", lambda b, kh, i: (b * n_q + i, kh))),
           ("kt", t["kt"], spec((ATT_HEAD_DIM, seq), "cur", lambda b, kh, i: (kh, b))),
           ("k_meta", t["k"], spec((N_META, ATT_HEAD_DIM), "cur", lambda b, kh, i: (meta_blk0 + b, kh))),
           ("v", t["v"], spec((seq, 2 * ATT_HEAD_DIM), "prev", lambda b, kh, i: (b, kh))),
           ("v_meta", t["v"], spec((N_META, 2 * ATT_HEAD_DIM), "prev", lambda b, kh, i: (meta_blk0 + b, kh)))]
    outs = [("o", jax.ShapeDtypeStruct((bsz * seq, ATT_Q_W), BF16),
             spec((ATT_Q_TILE, group_w), "prev", lambda b, kh, i: (b * n_q + i, kh)))]
    scratch = stash("0", ATT_GROUP * ATT_Q_TILE) + stash("1", ATT_GROUP * ATT_Q_TILE)
    if meta_queries:
        ins.append(("q_meta", t["q"], spec((N_META, group_w), "cur", lambda b, kh, i: (meta_blk0 + b, kh))))
        outs.append(("o_meta", jax.ShapeDtypeStruct((bsz * N_META, ATT_Q_W), BF16),
                     spec((N_META, group_w), "cur", lambda b, kh, i: (b, kh))))
        scratch += stash("q", ATT_GROUP * N_META)
    return _call(functools.partial(_attn_core_body, n_kv_blocks, n_q, n_tiles, meta_queries), (n_tiles + 1,),
                 ins, outs, scratch, ("arbitrary",), "attn_core")


def kernel(x, meta_tokens, norm_ffn1, ffn1_w_gate, ffn1_w_up, ffn1_w_down, norm_mix, gla_w_in, gla_gate_w1, gla_gate_w2, gla_gate_b, gla_head_norm, gla_w_out, attn_w_in, attn_q_norm, attn_k_norm, attn_w_out, norm_ffn2, ffn2_w_gate, ffn2_w_up, ffn2_w_down, norm_final):
    bsz, seq, d = x.shape
    depth = norm_ffn1.shape[0]
    assert d == D_MODEL and bsz * N_META == ROW_TILE and seq % STEP_ROWS == 0 and seq % ATT_KV_BLOCK == 0
    n_real = bsz * seq
    row = lambda g: g.reshape(1, -1).astype(F32)
    bf = lambda w: w.astype(BF16)

    h = x.reshape(n_real, d)
    h_meta = jnp.broadcast_to(meta_tokens.astype(x.dtype)[None], (bsz, N_META, d)).reshape(bsz * N_META, d)

    for i in range(depth):
        j = i // N_MIXERS
        last = i == depth - 1
        ffn1 = (row(norm_ffn1[i]), bf(ffn1_w_gate[i]), bf(ffn1_w_up[i]), bf(ffn1_w_down[i]))
        ffn2 = (row(norm_ffn2[i]), bf(ffn2_w_gate[i]), bf(ffn2_w_up[i]), bf(ffn2_w_down[i]))
        final_gain = row(norm_final) if last else None
        if i % N_MIXERS == 0:
            zero = jnp.zeros_like(gla_gate_w2[j, 0])
            gate_w2 = jnp.concatenate([jnp.concatenate([gla_gate_w2[j, 0], zero], axis=1),
                                       jnp.concatenate([zero, gla_gate_w2[j, 1]], axis=1)], axis=0)
            mixer_ins = [("w_in", bf(gla_w_in[j])),
                         ("gate_w1", bf(jnp.concatenate([gla_gate_w1[j, 0], gla_gate_w1[j, 1]], axis=1))),
                         ("gate_w2", bf(gate_w2)), ("gate_b", gla_gate_b[j].reshape(1, -1).astype(F32))]
            t = _token_stage_in("gla", h, h_meta, ffn1, row(norm_mix[i]), mixer_ins, seq)
            core = _gla_scan(bsz, seq, t)
            h = _token_stage_out("gla", core, t["h_out"], bf(gla_w_out[j]), ffn2,
                                 {"r": t["r"], "head_gain": row(gla_head_norm[j])}, final_gain, n_real)
        else:
            mixer_ins = [("w_in", bf(attn_w_in[j])), ("q_gain", row(attn_q_norm[j])), ("k_gain", row(attn_k_norm[j]))]
            t = _token_stage_in("attn", h, h_meta, ffn1, row(norm_mix[i]), mixer_ins, seq)
            core = _attn_core(bsz, seq, t, meta_queries=not last)
            h = _token_stage_out("attn", core, t["h_out"], bf(attn_w_out[j]), ffn2, {}, final_gain, n_real)
        h_meta = None
    return h.reshape(bsz, seq, d)
```

```python
import functools

import jax
import jax.numpy as jnp
from jax import lax
from jax.experimental import pallas as pl
from jax.experimental.pallas import tpu as pltpu

F32 = jnp.float32
BF16 = jnp.bfloat16

D_MODEL = 1024
N_META = 16
GRID_W = 64
D_FF = 2816
NORM_EPS = 1e-6
MACARON_WEIGHT = 0.5
N_MIXERS = 2

GLA_HEADS = 4
GLA_DK = D_MODEL // 2 // GLA_HEADS
GLA_DV = D_MODEL // GLA_HEADS
GLA_QK_W = GLA_HEADS * GLA_DK
GLA_V_W = GLA_HEADS * GLA_DV
GLA_GATE_RANK = 16
GLA_GATE_TAU = 16.0
GLA_CHUNK = 64

ATT_Q_HEADS = 8
ATT_KV_HEADS = 2
ATT_HEAD_DIM = D_MODEL // ATT_Q_HEADS
ATT_GROUP = ATT_Q_HEADS // ATT_KV_HEADS
ATT_Q_W = ATT_Q_HEADS * ATT_HEAD_DIM
ATT_KV_W = ATT_KV_HEADS * ATT_HEAD_DIM
ROPE_THETA = 10000.0
ROPE_AXIS_DIM = ATT_HEAD_DIM // 2
LOG2_E = 1.4426950408889634

LANES = 128
ROW_TILE = 256
STEP_TILES = 2
STEP_ROWS = STEP_TILES * ROW_TILE
FF_CHUNK = 256
GLA_BLOCK = 256
GLA_BATCHES = 2
ATT_Q_TILE = 256
ATT_KV_BLOCK = 512
VMEM_LIMIT = 56 * 1024 * 1024


def _params(semantics):
    return pltpu.CompilerParams(dimension_semantics=semantics, vmem_limit_bytes=VMEM_LIMIT)


def _const_spec(shape):
    nd = len(shape)
    return pl.BlockSpec(shape, lambda *_: (0,) * nd, pipeline_mode=pl.Buffered(1))


def _rms(x, gain):
    ms = jnp.mean(x * x, axis=-1, keepdims=True)
    return x * lax.rsqrt(ms + NORM_EPS) * gain


def _dot(a, b):
    return jnp.dot(a, b, preferred_element_type=F32)


def _dot_nt(a, b):
    return lax.dot_general(a, b, (((1,), (1,)), ((), ())), preferred_element_type=F32)


def _dot_tn(a, b):
    return lax.dot_general(a, b, (((0,), (0,)), ((), ())), preferred_element_type=F32)


def _call(body, grid, ins, outs, scratch, semantics, name):
    keys = [k for k, _, _ in ins] + [k for k, _, _ in outs] + [k for k, _ in scratch]

    def kernel_body(*refs):
        body(dict(zip(keys, refs, strict=True)))

    results = pl.pallas_call(
        kernel_body,
        grid=grid,
        in_specs=[s for _, _, s in ins],
        out_specs=[s for _, _, s in outs],
        out_shape=[a for _, a, _ in outs],
        scratch_shapes=[s for _, s in scratch],
        compiler_params=_params(semantics),
        name=name,
    )(*[a for _, a, _ in ins])
    return dict(zip([k for k, _, _ in outs], results, strict=True))


def _row_spec(width, clamp_steps=None):
    if clamp_steps is None:
        return pl.BlockSpec((STEP_ROWS, width), lambda i: (i, 0))
    return pl.BlockSpec((STEP_ROWS, width), lambda i: (jnp.minimum(i, clamp_steps - 1), 0))


def _tile_rows(tile):
    return slice(tile * ROW_TILE, (tile + 1) * ROW_TILE)


def _meta_tile(tile, real, meta_ref, n_real_steps):
    if tile > 0 or meta_ref is None:
        return real
    return jnp.where(pl.program_id(0) == n_real_steps, meta_ref[...], real)


def _swiglu(x, refs, a_ref):
    xn = _rms(x, refs["ffn_gain"][...]).astype(BF16)
    for c in range(D_FF // FF_CHUNK):
        cols = slice(c * FF_CHUNK, (c + 1) * FF_CHUNK)
        g = _dot(xn, refs["w_gate"][:, cols])
        u = _dot(xn, refs["w_up"][:, cols])
        a_ref[:, cols] = (g * jax.nn.sigmoid(g) * u).astype(BF16)
    return x + MACARON_WEIGHT * _dot(a_ref[...], refs["w_down"][...])


def _gla_in(y, refs, rows):
    xn = _rms(y, refs["mix_gain"][...]).astype(BF16)
    win_ref = refs["w_in"]
    refs["q"][rows, :] = (_dot(xn, win_ref[:, 0:GLA_QK_W]) * (GLA_DK ** -0.5)).astype(BF16)
    refs["k"][rows, :] = _dot(xn, win_ref[:, GLA_QK_W:2 * GLA_QK_W]).astype(BF16)
    refs["v"][rows, :] = _dot(xn, win_ref[:, 2 * GLA_QK_W:2 * GLA_QK_W + GLA_V_W]).astype(BF16)
    refs["r"][rows, :] = _dot(xn, win_ref[:, 2 * GLA_QK_W + GLA_V_W:]).astype(BF16)
    t = _dot(xn, refs["gate_w1"][...]).astype(BF16)
    z = _dot(t, refs["gate_w2"][...]) + refs["gate_b"][...]
    logg = -(jnp.maximum(-z, 0.0) + jnp.log1p(jnp.exp(-jnp.abs(z)))) / GLA_GATE_TAU
    refs["lg_fwd"][rows, :] = logg[:, 0:GLA_QK_W]
    refs["lg_rev"][rows, :] = logg[:, GLA_QK_W:]


def _gla_out(o, r, h, refs):
    gate = r * jax.nn.sigmoid(r)
    cols = []
    for hd in range(GLA_HEADS):
        vs = slice(hd * GLA_DV, (hd + 1) * GLA_DV)
        cols.append(_rms(o[:, vs], refs["head_gain"][...]) * gate[:, vs])
    return h + _dot(jnp.concatenate(cols, axis=1).astype(BF16), refs["w_out"][...])


def _rope(x, cos, sin_signed, first_half):
    partner = jnp.where(first_half, pltpu.roll(x, LANES - ROPE_AXIS_DIM // 2, 1),
                        pltpu.roll(x, ROPE_AXIS_DIM // 2, 1))
    return x * cos + partner * sin_signed


def _attn_in(y, refs, rows):
    xn = _rms(y, refs["mix_gain"][...]).astype(BF16)
    win_ref = refs["w_in"]
    cos = refs["cos"][rows, :]
    sin = refs["sin"][rows, :]
    lane = lax.broadcasted_iota(jnp.int32, cos.shape, 1)
    first_half = (lane % ROPE_AXIS_DIM) < (ROPE_AXIS_DIM // 2)
    q = _dot(xn, win_ref[:, 0:ATT_Q_W])
    for hd in range(ATT_Q_HEADS):
        hs = slice(hd * ATT_HEAD_DIM, (hd + 1) * ATT_HEAD_DIM)
        qh = _rope(_rms(q[:, hs], refs["q_gain"][...]), cos, sin, first_half) * (ATT_HEAD_DIM ** -0.5 * LOG2_E)
        refs["q"][rows, hs] = qh.astype(BF16)
    k = _dot(xn, win_ref[:, ATT_Q_W:ATT_Q_W + ATT_KV_W])
    v = _dot(xn, win_ref[:, ATT_Q_W + ATT_KV_W:]).astype(BF16)
    ones = jnp.ones((v.shape[0], ATT_HEAD_DIM), BF16)
    for hd in range(ATT_KV_HEADS):
        hs = slice(hd * ATT_HEAD_DIM, (hd + 1) * ATT_HEAD_DIM)
        kh = _rope(_rms(k[:, hs], refs["k_gain"][...]), cos, sin, first_half)
        refs["k"][rows, hs] = kh.astype(BF16)
        refs["kt"][hs, rows] = kh.T.astype(BF16)
        refs["v"][rows, 2 * hd * ATT_HEAD_DIM:(2 * hd + 1) * ATT_HEAD_DIM] = v[:, hs]
        refs["v"][rows, (2 * hd + 1) * ATT_HEAD_DIM:(2 * hd + 2) * ATT_HEAD_DIM] = ones


def _rope_tables(seq):
    pos = jnp.arange(seq)
    inv_freq = ROPE_THETA ** (-jnp.arange(0, ROPE_AXIS_DIM, 2, dtype=F32) / ROPE_AXIS_DIM)
    ang_row = (pos // GRID_W).astype(F32)[:, None] * inv_freq[None, :]
    ang_col = (pos % GRID_W).astype(F32)[:, None] * inv_freq[None, :]
    ang = jnp.concatenate([ang_row, ang_row, ang_col, ang_col], axis=1)
    ang = jnp.concatenate([ang, jnp.zeros((STEP_ROWS, ATT_HEAD_DIM), F32)], axis=0)
    sign = jnp.tile(jnp.concatenate([-jnp.ones((ROPE_AXIS_DIM // 2,), F32),
                                     jnp.ones((ROPE_AXIS_DIM // 2,), F32)]), 2)
    return jnp.cos(ang), jnp.sin(ang) * sign[None, :]


def _act_scratch():
    return [(f"act{tile}", pltpu.VMEM((ROW_TILE, D_FF), BF16)) for tile in range(STEP_TILES)]


def _ffn_operands(gain, wg, wu, wd):
    return [("ffn_gain", gain, _const_spec((1, D_MODEL))), ("w_gate", wg, _const_spec((D_MODEL, D_FF))),
            ("w_up", wu, _const_spec((D_MODEL, D_FF))), ("w_down", wd, _const_spec((D_FF, D_MODEL)))]


def _stage_in_body(mixer, n_real_steps, refs):
    for tile in range(STEP_TILES):
        rows = _tile_rows(tile)
        x = _meta_tile(tile, refs["h"][rows, :], refs.get("h_meta"), n_real_steps)
        y = _swiglu(x, refs, refs[f"act{tile}"])
        refs["h_out"][rows, :] = y
        (_gla_in if mixer == "gla" else _attn_in)(y, refs, rows)


def _token_stage_in(mixer, h, h_meta, ffn, mix_gain, mixer_ins, seq):
    n_real_steps = h.shape[0] // STEP_ROWS
    n_rows = h.shape[0] + (h_meta.shape[0] if h_meta is not None else 0)
    ins = [("h", h, _row_spec(D_MODEL, n_real_steps if h_meta is not None else None))]
    if h_meta is not None:
        assert h_meta.shape[0] == ROW_TILE
        ins.append(("h_meta", h_meta, _const_spec((ROW_TILE, D_MODEL))))
    ins += _ffn_operands(*ffn) + [("mix_gain", mix_gain, _const_spec((1, D_MODEL)))]
    ins += [(key, arr, _const_spec(arr.shape)) for key, arr in mixer_ins]
    shp = lambda w, dt: jax.ShapeDtypeStruct((n_rows, w), dt)
    outs = [("h_out", shp(D_MODEL, F32), _row_spec(D_MODEL))]
    if mixer == "gla":
        outs += [("q", shp(GLA_QK_W, BF16), _row_spec(GLA_QK_W)), ("k", shp(GLA_QK_W, BF16), _row_spec(GLA_QK_W)),
                 ("v", shp(GLA_V_W, BF16), _row_spec(GLA_V_W)), ("r", shp(GLA_V_W, BF16), _row_spec(GLA_V_W)),
                 ("lg_fwd", shp(GLA_QK_W, F32), _row_spec(GLA_QK_W)),
                 ("lg_rev", shp(GLA_QK_W, F32), _row_spec(GLA_QK_W))]
    else:
        n_seq_steps = seq // STEP_ROWS
        real_steps = (n_rows - ROW_TILE) // STEP_ROWS
        table = pl.BlockSpec((STEP_ROWS, ATT_HEAD_DIM),
                             lambda i: (jnp.where(i < real_steps, i % n_seq_steps, n_seq_steps), 0))
        cos, sin = _rope_tables(seq)
        ins += [("cos", cos, table), ("sin", sin, table)]
        outs += [("q", shp(ATT_Q_W, BF16), _row_spec(ATT_Q_W)), ("k", shp(ATT_KV_W, BF16), _row_spec(ATT_KV_W)),
                 ("kt", jax.ShapeDtypeStruct((ATT_KV_W, n_rows), BF16),
                  pl.BlockSpec((ATT_KV_W, STEP_ROWS), lambda i: (0, i))),
                 ("v", shp(2 * ATT_KV_W, BF16), _row_spec(2 * ATT_KV_W))]
    return _call(functools.partial(_stage_in_body, mixer, n_real_steps), (pl.cdiv(n_rows, STEP_ROWS),), ins, outs,
                 _act_scratch(), ("parallel",), f"token_stage_in_{mixer}")


def _stage_out_body(mixer, n_real_steps, final_norm, refs):
    for tile in range(STEP_TILES):
        rows = _tile_rows(tile)
        h = refs["h"][rows, :]
        if mixer == "gla":
            o = (_meta_tile(tile, refs["o_fwd"][rows, :], refs.get("o_fwd_meta"), n_real_steps)
                 + _meta_tile(tile, refs["o_rev"][rows, :], refs.get("o_rev_meta"), n_real_steps))
            x = _gla_out(o, refs["r"][rows, :].astype(F32), h, refs)
        else:
            o = _meta_tile(tile, refs["o"][rows, :], refs.get("o_meta"), n_real_steps)
            x = h + _dot(o, refs["w_out"][...])
        y = _swiglu(x, refs, refs[f"act{tile}"])
        refs["h_out"][rows, :] = _rms(y, refs["final_gain"][...]) if final_norm else y


def _token_stage_out(mixer, core, h, w_out, ffn, extra, final_gain, n_real):
    last = final_gain is not None
    n_real_steps = n_real // STEP_ROWS
    n_rows = n_real if last else h.shape[0]
    has_meta = not last
    ins = [("h", h, _row_spec(D_MODEL))]
    if mixer == "gla":
        ins += [("o_fwd", core["o_fwd"], _row_spec(GLA_V_W, n_real_steps)),
                ("o_rev", core["o_rev"], _row_spec(GLA_V_W, n_real_steps)),
                ("r", extra["r"], _row_spec(GLA_V_W)), ("head_gain", extra["head_gain"], _const_spec((1, GLA_DV)))]
        if has_meta:
            ins += [("o_fwd_meta", core["o_fwd_meta"], _const_spec((ROW_TILE, GLA_V_W))),
                    ("o_rev_meta", core["o_rev_meta"], _const_spec((ROW_TILE, GLA_V_W)))]
    else:
        ins += [("o", core["o"], _row_spec(ATT_Q_W, n_real_steps))]
        if has_meta:
            ins += [("o_meta", core["o_meta"], _const_spec((ROW_TILE, ATT_Q_W)))]
    ins += [("w_out", w_out, _const_spec(w_out.shape))] + _ffn_operands(*ffn)
    if last:
        ins.append(("final_gain", final_gain, _const_spec((1, D_MODEL))))
    outs = [("h_out", jax.ShapeDtypeStruct((n_rows, D_MODEL), F32), _row_spec(D_MODEL))]
    return _call(functools.partial(_stage_out_body, mixer, n_real_steps, last), (pl.cdiv(n_rows, STEP_ROWS),),
                 ins, outs, _act_scratch(), ("parallel",), f"token_stage_out_{mixer}")["h_out"]


def _gla_chunks_local(items):
    t_len = items[0][1].shape[0]
    row = lax.broadcasted_iota(jnp.int32, (t_len, t_len), 0)
    col = lax.broadcasted_iota(jnp.int32, (t_len, t_len), 1)
    row3 = lax.broadcasted_iota(jnp.int32, (t_len, 3 * t_len), 0)
    col3 = lax.broadcasted_iota(jnp.int32, (t_len, 3 * t_len), 1) % t_len
    mask = {False: col <= row, True: col >= row}
    mask3 = {False: (col3 <= row3).astype(BF16), True: (col3 >= row3).astype(BF16)}
    pieces = []
    for _, _, _, _, lg in items:
        hi = lg.astype(BF16)
        rest = lg - hi.astype(F32)
        mid = rest.astype(BF16)
        pieces.append(jnp.concatenate([hi, mid, (rest - mid.astype(F32)).astype(BF16)], axis=0))
    bs = [_dot(mask3[item[0]], p) for item, p in zip(items, pieces)]
    scaled = []
    for (rev, q, k, _, _), b in zip(items, bs):
        b_tot = b[0:1, :] if rev else b[t_len - 1:t_len, :]
        scaled.append(((q * jnp.exp(b)).astype(BF16), (k * jnp.exp(-b)).astype(BF16),
                       (k * jnp.exp(b_tot - b)).astype(BF16), jnp.exp(b_tot)))
    heads = [slice(h * GLA_DK, (h + 1) * GLA_DK) for h in range(GLA_HEADS)]
    scores = [[_dot_nt(qd[:, ks], ki[:, ks]) for ks in heads] for qd, ki, _, _ in scaled]
    lhs = [[jnp.concatenate([jnp.where(mask[item[0]], s, 0.0).astype(BF16), ke[:, ks].T], axis=0)
            for s, ks in zip(ss, heads)] for item, ss, (_, _, ke, _) in zip(items, scores, scaled)]
    local = [[_dot(l, item[3][:, h * GLA_DV:(h + 1) * GLA_DV]) for h, l in enumerate(ls)]
             for item, ls in zip(items, lhs)]
    return [(qd, dec, loc) for (qd, _, _, dec), loc in zip(scaled, local)]


def _gla_chunks_carry(chunks, states):
    t_len = chunks[0][0].shape[0]
    heads = [slice(h * GLA_DK, (h + 1) * GLA_DK) for h in range(GLA_HEADS)]
    carried = [[_dot(qd[:, ks], st.astype(BF16)) for ks, st in zip(heads, sts)]
               for (qd, _, _), sts in zip(chunks, states)]
    outs = [jnp.concatenate([loc[0:t_len, :] + c for loc, c in zip(local, cs)], axis=1)
            for (_, _, local), cs in zip(chunks, carried)]
    dec_cols = [[jnp.broadcast_to(dec[:, ks], (GLA_DK, GLA_DK)).T for ks in heads] for _, dec, _ in chunks]
    new_states = [[jnp.concatenate([dc] * (GLA_DV // GLA_DK), axis=1) * st + loc[t_len:, :]
                   for dc, st, loc in zip(dcs, sts, local)]
                  for dcs, sts, (_, _, local) in zip(dec_cols, states, chunks)]
    return outs, new_states


def _gla_scan_body(refs):
    j = pl.program_id(1)
    n_chunks = GLA_BLOCK // GLA_CHUNK
    pad = GLA_CHUNK - N_META
    streams = [(slot, rev) for slot in range(GLA_BATCHES) for rev in (False, True)]
    tag = lambda slot, rev: ("_rev" if rev else "_fwd") + str(slot)

    @pl.when(j == 0)
    def _():
        for s in streams:
            refs["state" + tag(*s)][...] = jnp.zeros_like(refs["state" + tag(*s)])

    def load_states(s):
        return [refs["state" + tag(*s)][h] for h in range(GLA_HEADS)]

    def store_states(s, states):
        for h in range(GLA_HEADS):
            refs["state" + tag(*s)][h] = states[h]

    def meta_chunks(rev):
        front = lambda ref: jnp.concatenate([jnp.zeros((pad, ref.shape[1]), ref.dtype), ref[...]], axis=0)
        for slot in range(GLA_BATCHES):
            m = "_meta" + str(slot)
            (qd, dec, local), = _gla_chunks_local([(rev, front(refs["q" + m]), front(refs["k" + m]),
                                                    front(refs["v" + m]),
                                                    front(refs[("lg_rev" if rev else "lg_fwd") + m]))])
            (o,), (states,) = _gla_chunks_carry([(qd, dec, local)], [load_states((slot, rev))])
            store_states((slot, rev), states)
            refs["o_rev_meta" if rev else "o_fwd_meta"][slot] = o[pad:, :]

    pl.when(j == 0)(lambda: meta_chunks(False))

    rows = lambda c: slice(c * GLA_CHUNK, (c + 1) * GLA_CHUNK)
    states = {s: load_states(s) for s in streams}
    for ci in range(n_chunks):
        items = []
        for s in streams:
            c = n_chunks - 1 - ci if s[1] else ci
            items.append((s[1], refs["q" + tag(*s)][rows(c), :], refs["k" + tag(*s)][rows(c), :],
                          refs["v" + tag(*s)][rows(c), :], refs["lg" + tag(*s)][rows(c), :]))
        outs, new_states = _gla_chunks_carry(_gla_chunks_local(items), [states[s] for s in streams])
        for s, o, st in zip(streams, outs, new_states):
            c = n_chunks - 1 - ci if s[1] else ci
            states[s] = st
            refs["o_rev" if s[1] else "o_fwd"][s[0], rows(c), :] = o
    for s in streams:
        store_states(s, states[s])

    pl.when(j == pl.num_programs(1) - 1)(lambda: meta_chunks(True))


def _gla_scan(bsz, seq, t):
    assert bsz % GLA_BATCHES == 0
    n_blk = seq // GLA_BLOCK
    meta_blk0 = bsz * seq // N_META
    ins, scratch = [], []
    for slot in range(GLA_BATCHES):
        batch = lambda p, slot=slot: p * GLA_BATCHES + slot
        fwd = lambda w, batch=batch: pl.BlockSpec((GLA_BLOCK, w), lambda p, j: (batch(p) * n_blk + j, 0))
        rev = lambda w, batch=batch: pl.BlockSpec((GLA_BLOCK, w), lambda p, j: (batch(p) * n_blk + n_blk - 1 - j, 0))
        meta = lambda w, batch=batch: pl.BlockSpec((N_META, w), lambda p, j: (meta_blk0 + batch(p), 0))
        for sfx, spec in (("_fwd", fwd), ("_rev", rev)):
            sfx_slot = sfx + str(slot)
            ins += [("q" + sfx_slot, t["q"], spec(GLA_QK_W)), ("k" + sfx_slot, t["k"], spec(GLA_QK_W)),
                    ("v" + sfx_slot, t["v"], spec(GLA_V_W)), ("lg" + sfx_slot, t["lg" + sfx], spec(GLA_QK_W))]
            scratch.append(("state" + sfx_slot, pltpu.VMEM((GLA_HEADS, GLA_DK, GLA_DV), F32)))
        m = "_meta" + str(slot)
        ins += [("q" + m, t["q"], meta(GLA_QK_W)), ("k" + m, t["k"], meta(GLA_QK_W)), ("v" + m, t["v"], meta(GLA_V_W)),
                ("lg_fwd" + m, t["lg_fwd"], meta(GLA_QK_W)), ("lg_rev" + m, t["lg_rev"], meta(GLA_QK_W))]
    real_shape = jax.ShapeDtypeStruct((bsz, seq, GLA_V_W), F32)
    meta_shape = jax.ShapeDtypeStruct((bsz, N_META, GLA_V_W), F32)
    meta_out = pl.BlockSpec((GLA_BATCHES, N_META, GLA_V_W), lambda p, j: (p, 0, 0))
    outs = [("o_fwd", real_shape, pl.BlockSpec((GLA_BATCHES, GLA_BLOCK, GLA_V_W), lambda p, j: (p, j, 0))),
            ("o_rev", real_shape, pl.BlockSpec((GLA_BATCHES, GLA_BLOCK, GLA_V_W), lambda p, j: (p, n_blk - 1 - j, 0))),
            ("o_fwd_meta", meta_shape, meta_out), ("o_rev_meta", meta_shape, meta_out)]
    res = _call(_gla_scan_body, (bsz // GLA_BATCHES, n_blk), ins, outs, scratch, ("parallel", "arbitrary"), "gla_scan")
    return {"o_fwd": res["o_fwd"].reshape(bsz * seq, GLA_V_W), "o_rev": res["o_rev"].reshape(bsz * seq, GLA_V_W),
            "o_fwd_meta": res["o_fwd_meta"].reshape(bsz * N_META, GLA_V_W),
            "o_rev_meta": res["o_rev_meta"].reshape(bsz * N_META, GLA_V_W)}


def _attn_scores_block(q, kt_ref, kb, s_ref, mx):
    cols = slice(kb * ATT_KV_BLOCK, (kb + 1) * ATT_KV_BLOCK)
    s = _dot(q, kt_ref[:, cols])
    s_ref[kb] = s
    tiles = [s[:, c * LANES:(c + 1) * LANES] for c in range(ATT_KV_BLOCK // LANES)]
    while len(tiles) > 1:
        tiles = [jnp.maximum(tiles[i], tiles[i + 1]) for i in range(0, len(tiles), 2)]
    return tiles[0] if mx is None else jnp.maximum(mx, tiles[0])


def _attn_scores_finish(q, km, mx, sm_ref, m_ref):
    sm = _dot_nt(q, km)
    sm_ref[...] = sm
    m = jnp.maximum(jnp.max(mx, axis=1, keepdims=True), jnp.max(sm, axis=1, keepdims=True))
    m_ref[...] = jnp.broadcast_to(m, m_ref.shape)


def _attn_weighted_start(sm_ref, m_ref, vm):
    m = m_ref[...]
    return m, _dot(jnp.exp2(sm_ref[...] - m[:, 0:N_META]).astype(BF16), vm)


def _attn_weighted_block(s_ref, kb, m, v_ref, acc):
    rows = slice(kb * ATT_KV_BLOCK, (kb + 1) * ATT_KV_BLOCK)
    p = [jnp.exp2(s_ref[kb, :, c * LANES:(c + 1) * LANES] - m) for c in range(ATT_KV_BLOCK // LANES)]
    return acc + _dot(jnp.concatenate(p, axis=1).astype(BF16), v_ref[rows, :])


def _attn_core_body(n_kv_blocks, n_q, n_tiles, meta_queries, refs):
    t = pl.program_id(0)
    kt_ref, v_ref = refs["kt"], refs["v"]
    km = refs["k_meta"][...]
    vm = refs["v_meta"][...]
    stash = [(refs[f"s{i}"], refs[f"sm{i}"], refs[f"m{i}"]) for i in range(2)]

    def stack(ref):
        return jnp.concatenate([ref[:, g * ATT_HEAD_DIM:(g + 1) * ATT_HEAD_DIM] for g in range(ATT_GROUP)], axis=0)

    def unstack(acc, ref):
        o = acc[:, 0:ATT_HEAD_DIM] / acc[:, ATT_HEAD_DIM:]
        rows = ref.shape[0]
        for g in range(ATT_GROUP):
            ref[:, g * ATT_HEAD_DIM:(g + 1) * ATT_HEAD_DIM] = o[g * rows:(g + 1) * rows, :].astype(BF16)

    @pl.when(t == 0)
    def _():
        for ref in stash[1]:
            ref[...] = jnp.zeros_like(ref)

    def step(write, read):
        ws_ref, wsm_ref, wm_ref = write
        rs_ref, rsm_ref, rm_ref = read
        q = stack(refs["q"])
        m, acc = _attn_weighted_start(rsm_ref, rm_ref, vm)
        mx = None
        for kb in range(n_kv_blocks):
            acc = _attn_weighted_block(rs_ref, kb, m, v_ref, acc)
            mx = _attn_scores_block(q, kt_ref, kb, ws_ref, mx)
        _attn_scores_finish(q, km, mx, wsm_ref, wm_ref)
        unstack(acc, refs["o"])

    pl.when(t % 2 == 0)(lambda: step(stash[0], stash[1]))
    pl.when(t % 2 == 1)(lambda: step(stash[1], stash[0]))

    if meta_queries:
        @pl.when(jnp.minimum(t, n_tiles - 1) % n_q == n_q - 1)
        def _():
            sq_ref, smq_ref, mq_ref = refs["sq"], refs["smq"], refs["mq"]
            q = stack(refs["q_meta"])
            mx = None
            for kb in range(n_kv_blocks):
                mx = _attn_scores_block(q, kt_ref, kb, sq_ref, mx)
            _attn_scores_finish(q, km, mx, smq_ref, mq_ref)
            m, acc = _attn_weighted_start(smq_ref, mq_ref, vm)
            for kb in range(n_kv_blocks):
                acc = _attn_weighted_block(sq_ref, kb, m, v_ref, acc)
            unstack(acc, refs["o_meta"])


def _attn_core(bsz, seq, t, meta_queries):
    n_q = seq // ATT_Q_TILE
    assert n_q >= 2
    n_tiles = bsz * ATT_KV_HEADS * n_q
    meta_blk0 = bsz * seq // N_META
    n_kv_blocks = seq // ATT_KV_BLOCK
    group_w = ATT_GROUP * ATT_HEAD_DIM

    def coords(tile):
        return tile // (ATT_KV_HEADS * n_q), (tile // n_q) % ATT_KV_HEADS, tile % n_q

    def spec(shape, which, fn):
        tile = (lambda s: jnp.minimum(s, n_tiles - 1)) if which == "cur" else (lambda s: jnp.maximum(s - 1, 0))
        return pl.BlockSpec(shape, lambda s: fn(*coords(tile(s))))

    def stash(sfx, rows):
        return [("s" + sfx, pltpu.VMEM((n_kv_blocks, rows, ATT_KV_BLOCK), F32)),
                ("sm" + sfx, pltpu.VMEM((rows, N_META), F32)), ("m" + sfx, pltpu.VMEM((rows, LANES), F32))]

    ins = [("q", t["q"], spec((ATT_Q_TILE, group_w), "cur", lambda b, kh, i: (b * n_q + i, kh))),
           ("kt", t["kt"], spec((ATT_HEAD_DIM, seq), "cur", lambda b, kh, i: (kh, b))),
           ("k_meta", t["k"], spec((N_META, ATT_HEAD_DIM), "cur", lambda b, kh, i: (meta_blk0 + b, kh))),
           ("v", t["v"], spec((seq, 2 * ATT_HEAD_DIM), "prev", lambda b, kh, i: (b, kh))),
           ("v_meta", t["v"], spec((N_META, 2 * ATT_HEAD_DIM), "prev", lambda b, kh, i: (meta_blk0 + b, kh)))]
    outs = [("o", jax.ShapeDtypeStruct((bsz * seq, ATT_Q_W), BF16),
             spec((ATT_Q_TILE, group_w), "prev", lambda b, kh, i: (b * n_q + i, kh)))]
    scratch = stash("0", ATT_GROUP * ATT_Q_TILE) + stash("1", ATT_GROUP * ATT_Q_TILE)
    if meta_queries:
        ins.append(("q_meta", t["q"], spec((N_META, group_w), "cur", lambda b, kh, i: (meta_blk0 + b, kh))))
        outs.append(("o_meta", jax.ShapeDtypeStruct((bsz * N_META, ATT_Q_W), BF16),
                     spec((N_META, group_w), "cur", lambda b, kh, i: (b, kh))))
        scratch += stash("q", ATT_GROUP * N_META)
    return _call(functools.partial(_attn_core_body, n_kv_blocks, n_q, n_tiles, meta_queries), (n_tiles + 1,),
                 ins, outs, scratch, ("arbitrary",), "attn_core")


def kernel(x, meta_tokens, norm_ffn1, ffn1_w_gate, ffn1_w_up, ffn1_w_down, norm_mix, gla_w_in, gla_gate_w1, gla_gate_w2, gla_gate_b, gla_head_norm, gla_w_out, attn_w_in, attn_q_norm, attn_k_norm, attn_w_out, norm_ffn2, ffn2_w_gate, ffn2_w_up, ffn2_w_down, norm_final):
    bsz, seq, d = x.shape
    depth = norm_ffn1.shape[0]
    assert d == D_MODEL and bsz * N_META == ROW_TILE and seq % STEP_ROWS == 0 and seq % ATT_KV_BLOCK == 0
    n_real = bsz * seq
    row = lambda g: g.reshape(1, -1).astype(F32)
    bf = lambda w: w.astype(BF16)

    h = x.reshape(n_real, d)
    h_meta = jnp.broadcast_to(meta_tokens.astype(x.dtype)[None], (bsz, N_META, d)).reshape(bsz * N_META, d)

    for i in range(depth):
        j = i // N_MIXERS
        last = i == depth - 1
        ffn1 = (row(norm_ffn1[i]), bf(ffn1_w_gate[i]), bf(ffn1_w_up[i]), bf(ffn1_w_down[i]))
        ffn2 = (row(norm_ffn2[i]), bf(ffn2_w_gate[i]), bf(ffn2_w_up[i]), bf(ffn2_w_down[i]))
        final_gain = row(norm_final) if last else None
        if i % N_MIXERS == 0:
            zero = jnp.zeros_like(gla_gate_w2[j, 0])
            gate_w2 = jnp.concatenate([jnp.concatenate([gla_gate_w2[j, 0], zero], axis=1),
                                       jnp.concatenate([zero, gla_gate_w2[j, 1]], axis=1)], axis=0)
            mixer_ins = [("w_in", bf(gla_w_in[j])),
                         ("gate_w1", bf(jnp.concatenate([gla_gate_w1[j, 0], gla_gate_w1[j, 1]], axis=1))),
                         ("gate_w2", bf(gate_w2)), ("gate_b", gla_gate_b[j].reshape(1, -1).astype(F32))]
            t = _token_stage_in("gla", h, h_meta, ffn1, row(norm_mix[i]), mixer_ins, seq)
            core = _gla_scan(bsz, seq, t)
            h = _token_stage_out("gla", core, t["h_out"], bf(gla_w_out[j]), ffn2,
                                 {"r": t["r"], "head_gain": row(gla_head_norm[j])}, final_gain, n_real)
        else:
            mixer_ins = [("w_in", bf(attn_w_in[j])), ("q_gain", row(attn_q_norm[j])), ("k_gain", row(attn_k_norm[j]))]
            t = _token_stage_in("attn", h, h_meta, ffn1, row(norm_mix[i]), mixer_ins, seq)
            core = _attn_core(bsz, seq, t, meta_queries=not last)
            h = _token_stage_out("attn", core, t["h_out"], bf(attn_w_out[j]), ffn2, {}, final_gain, n_real)
        h_meta = None
    return h.reshape(bsz, seq, d)
```

```python
import functools

import jax
import jax.numpy as jnp
from jax import lax
from jax.experimental import pallas as pl
from jax.experimental.pallas import tpu as pltpu

F32 = jnp.float32
BF16 = jnp.bfloat16

D_MODEL = 1024
N_META = 16
GRID_W = 64
D_FF = 2816
NORM_EPS = 1e-6
MACARON_WEIGHT = 0.5
N_MIXERS = 2

GLA_HEADS = 4
GLA_DK = D_MODEL // 2 // GLA_HEADS
GLA_DV = D_MODEL // GLA_HEADS
GLA_QK_W = GLA_HEADS * GLA_DK
GLA_V_W = GLA_HEADS * GLA_DV
GLA_GATE_RANK = 16
GLA_GATE_TAU = 16.0
GLA_CHUNK = 64

ATT_Q_HEADS = 8
ATT_KV_HEADS = 2
ATT_HEAD_DIM = D_MODEL // ATT_Q_HEADS
ATT_GROUP = ATT_Q_HEADS // ATT_KV_HEADS
ATT_Q_W = ATT_Q_HEADS * ATT_HEAD_DIM
ATT_KV_W = ATT_KV_HEADS * ATT_HEAD_DIM
ROPE_THETA = 10000.0
ROPE_AXIS_DIM = ATT_HEAD_DIM // 2
LOG2_E = 1.4426950408889634

LANES = 128
ROW_TILE = 256
STEP_TILES = 2
STEP_ROWS = STEP_TILES * ROW_TILE
STAGE_LAG = 5
FF_CHUNK = 256
GLA_BLOCK = 256
GLA_BATCHES = 2
ATT_Q_TILE = 256
ATT_KV_BLOCK = 512
VMEM_LIMIT = 56 * 1024 * 1024


def _params(semantics):
    return pltpu.CompilerParams(dimension_semantics=semantics, vmem_limit_bytes=VMEM_LIMIT)


def _const_spec(shape):
    nd = len(shape)
    return pl.BlockSpec(shape, lambda *_: (0,) * nd, pipeline_mode=pl.Buffered(1))


def _rms(x, gain):
    ms = jnp.mean(x * x, axis=-1, keepdims=True)
    return x * lax.rsqrt(ms + NORM_EPS) * gain


def _dot(a, b):
    return jnp.dot(a, b, preferred_element_type=F32)


def _dot_nt(a, b):
    return lax.dot_general(a, b, (((1,), (1,)), ((), ())), preferred_element_type=F32)


def _dot_tn(a, b):
    return lax.dot_general(a, b, (((0,), (0,)), ((), ())), preferred_element_type=F32)


def _call(body, grid, ins, outs, scratch, semantics, name):
    keys = [k for k, _, _ in ins] + [k for k, _, _ in outs] + [k for k, _ in scratch]

    def kernel_body(*refs):
        body(dict(zip(keys, refs, strict=True)))

    results = pl.pallas_call(
        kernel_body,
        grid=grid,
        in_specs=[s for _, _, s in ins],
        out_specs=[s for _, _, s in outs],
        out_shape=[a for _, a, _ in outs],
        scratch_shapes=[s for _, s in scratch],
        compiler_params=_params(semantics),
        name=name,
    )(*[a for _, a, _ in ins])
    return dict(zip([k for k, _, _ in outs], results, strict=True))


def _row_spec(width, clamp_steps=None):
    if clamp_steps is None:
        return pl.BlockSpec((STEP_ROWS, width), lambda i: (i, 0))
    return pl.BlockSpec((STEP_ROWS, width), lambda i: (jnp.minimum(i, clamp_steps - 1), 0))


def _tile_rows(tile):
    return slice(tile * ROW_TILE, (tile + 1) * ROW_TILE)


def _meta_tile(tile, real, meta_ref, n_real_steps):
    if tile > 0 or meta_ref is None:
        return real
    return jnp.where(pl.program_id(0) == n_real_steps, meta_ref[...], real)


def _swiglu_pieces(env, refs, a_ref):
    def norm():
        env["xn"] = _rms(env["x"], refs["ffn_gain"][...]).astype(BF16)

    def chunk(c):
        cols = slice(c * FF_CHUNK, (c + 1) * FF_CHUNK)
        g = _dot(env["xn"], refs["w_gate"][:, cols])
        u = _dot(env["xn"], refs["w_up"][:, cols])
        a_ref[:, cols] = (g * jax.nn.sigmoid(g) * u).astype(BF16)

    def down():
        env["y"] = env["x"] + MACARON_WEIGHT * _dot(a_ref[...], refs["w_down"][...])

    return [norm] + [functools.partial(chunk, c) for c in range(D_FF // FF_CHUNK)] + [down]


def _interleave(pipelines, lag):
    n = max(len(p) for p in pipelines)
    for step in range(n + lag * (len(pipelines) - 1)):
        for i, p in enumerate(pipelines):
            k = step - i * lag
            if 0 <= k < len(p):
                p[k]()


def _gla_in_pieces(env, refs, rows):
    win_ref = refs["w_in"]

    def norm():
        env["yn"] = _rms(env["y"], refs["mix_gain"][...]).astype(BF16)

    def q():
        refs["q"][rows, :] = (_dot(env["yn"], win_ref[:, 0:GLA_QK_W]) * (GLA_DK ** -0.5)).astype(BF16)

    def k():
        refs["k"][rows, :] = _dot(env["yn"], win_ref[:, GLA_QK_W:2 * GLA_QK_W]).astype(BF16)

    def v(half):
        cols = slice(half * GLA_QK_W, (half + 1) * GLA_QK_W)
        refs["v"][rows, cols] = _dot(env["yn"], win_ref[:, 2 * GLA_QK_W + half * GLA_QK_W:
                                                        2 * GLA_QK_W + (half + 1) * GLA_QK_W]).astype(BF16)

    def r(half):
        cols = slice(half * GLA_QK_W, (half + 1) * GLA_QK_W)
        base = 2 * GLA_QK_W + GLA_V_W
        refs["r"][rows, cols] = _dot(env["yn"], win_ref[:, base + half * GLA_QK_W:
                                                        base + (half + 1) * GLA_QK_W]).astype(BF16)

    def gates():
        t = _dot(env["yn"], refs["gate_w1"][...]).astype(BF16)
        z = _dot(t, refs["gate_w2"][...]) + refs["gate_b"][...]
        logg = (jnp.minimum(z, 0.0) - jnp.log(1.0 + jnp.exp(-jnp.abs(z)))) * (1.0 / GLA_GATE_TAU)
        refs["lg_fwd"][rows, :] = logg[:, 0:GLA_QK_W]
        refs["lg_rev"][rows, :] = logg[:, GLA_QK_W:]

    return [norm, gates, q, k, functools.partial(v, 0), functools.partial(v, 1),
            functools.partial(r, 0), functools.partial(r, 1)]


def _gla_out(o, r, h, refs):
    gate = r * jax.nn.sigmoid(r)
    cols = []
    for hd in range(GLA_HEADS):
        vs = slice(hd * GLA_DV, (hd + 1) * GLA_DV)
        cols.append(_rms(o[:, vs], refs["head_gain"][...]) * gate[:, vs])
    return h + _dot(jnp.concatenate(cols, axis=1).astype(BF16), refs["w_out"][...])


def _rope(x, cos, sin_signed, first_half):
    partner = jnp.where(first_half, pltpu.roll(x, LANES - ROPE_AXIS_DIM // 2, 1),
                        pltpu.roll(x, ROPE_AXIS_DIM // 2, 1))
    return x * cos + partner * sin_signed


def _attn_in_pieces(env, refs, rows):
    win_ref = refs["w_in"]

    def norm():
        env["yn"] = _rms(env["y"], refs["mix_gain"][...]).astype(BF16)
        cos = refs["cos"][rows, :]
        lane = lax.broadcasted_iota(jnp.int32, cos.shape, 1)
        env["rope"] = (cos, refs["sin"][rows, :], (lane % ROPE_AXIS_DIM) < (ROPE_AXIS_DIM // 2))

    def q(pair):
        cols = slice(2 * pair * ATT_HEAD_DIM, (2 * pair + 2) * ATT_HEAD_DIM)
        qq = _dot(env["yn"], win_ref[:, cols])
        for i in range(2):
            hs = slice(i * ATT_HEAD_DIM, (i + 1) * ATT_HEAD_DIM)
            qh = _rope(_rms(qq[:, hs], refs["q_gain"][...]), *env["rope"]) * (ATT_HEAD_DIM ** -0.5 * LOG2_E)
            refs["q"][rows, (2 * pair + i) * ATT_HEAD_DIM:(2 * pair + i + 1) * ATT_HEAD_DIM] = qh.astype(BF16)

    def kv():
        k = _dot(env["yn"], win_ref[:, ATT_Q_W:ATT_Q_W + ATT_KV_W])
        v = _dot(env["yn"], win_ref[:, ATT_Q_W + ATT_KV_W:]).astype(BF16)
        ones = jnp.ones((v.shape[0], ATT_HEAD_DIM), BF16)
        for hd in range(ATT_KV_HEADS):
            hs = slice(hd * ATT_HEAD_DIM, (hd + 1) * ATT_HEAD_DIM)
            kh = _rope(_rms(k[:, hs], refs["k_gain"][...]), *env["rope"])
            refs["k"][rows, hs] = kh.astype(BF16)
            refs["kt"][hs, rows] = kh.T.astype(BF16)
            refs["v"][rows, 2 * hd * ATT_HEAD_DIM:(2 * hd + 1) * ATT_HEAD_DIM] = v[:, hs]
            refs["v"][rows, (2 * hd + 1) * ATT_HEAD_DIM:(2 * hd + 2) * ATT_HEAD_DIM] = ones

    return [norm] + [functools.partial(q, pair) for pair in range(ATT_Q_HEADS // 2)] + [kv]


def _rope_tables(seq):
    pos = jnp.arange(seq)
    inv_freq = ROPE_THETA ** (-jnp.arange(0, ROPE_AXIS_DIM, 2, dtype=F32) / ROPE_AXIS_DIM)
    ang_row = (pos // GRID_W).astype(F32)[:, None] * inv_freq[None, :]
    ang_col = (pos % GRID_W).astype(F32)[:, None] * inv_freq[None, :]
    ang = jnp.concatenate([ang_row, ang_row, ang_col, ang_col], axis=1)
    ang = jnp.concatenate([ang, jnp.zeros((STEP_ROWS, ATT_HEAD_DIM), F32)], axis=0)
    sign = jnp.tile(jnp.concatenate([-jnp.ones((ROPE_AXIS_DIM // 2,), F32),
                                     jnp.ones((ROPE_AXIS_DIM // 2,), F32)]), 2)
    return jnp.cos(ang), jnp.sin(ang) * sign[None, :]


def _act_scratch():
    return [(f"act{tile}", pltpu.VMEM((ROW_TILE, D_FF), BF16)) for tile in range(STEP_TILES)]


def _ffn_operands(gain, wg, wu, wd):
    return [("ffn_gain", gain, _const_spec((1, D_MODEL))), ("w_gate", wg, _const_spec((D_MODEL, D_FF))),
            ("w_up", wu, _const_spec((D_MODEL, D_FF))), ("w_down", wd, _const_spec((D_FF, D_MODEL)))]


def _stage_in_body(mixer, n_real_steps, refs):
    def pipeline(tile):
        rows, env = _tile_rows(tile), {}

        def load():
            env["x"] = _meta_tile(tile, refs["h"][rows, :], refs.get("h_meta"), n_real_steps)

        def store():
            refs["h_out"][rows, :] = env["y"]

        mixer_in = _gla_in_pieces if mixer == "gla" else _attn_in_pieces
        return [load] + _swiglu_pieces(env, refs, refs[f"act{tile}"]) + [store] + mixer_in(env, refs, rows)

    _interleave([pipeline(tile) for tile in range(STEP_TILES)], STAGE_LAG)


def _token_stage_in(mixer, h, h_meta, ffn, mix_gain, mixer_ins, seq):
    n_real_steps = h.shape[0] // STEP_ROWS
    n_rows = h.shape[0] + (h_meta.shape[0] if h_meta is not None else 0)
    ins = [("h", h, _row_spec(D_MODEL, n_real_steps if h_meta is not None else None))]
    if h_meta is not None:
        assert h_meta.shape[0] == ROW_TILE
        ins.append(("h_meta", h_meta, _const_spec((ROW_TILE, D_MODEL))))
    ins += _ffn_operands(*ffn) + [("mix_gain", mix_gain, _const_spec((1, D_MODEL)))]
    ins += [(key, arr, _const_spec(arr.shape)) for key, arr in mixer_ins]
    shp = lambda w, dt: jax.ShapeDtypeStruct((n_rows, w), dt)
    outs = [("h_out", shp(D_MODEL, F32), _row_spec(D_MODEL))]
    if mixer == "gla":
        outs += [("q", shp(GLA_QK_W, BF16), _row_spec(GLA_QK_W)), ("k", shp(GLA_QK_W, BF16), _row_spec(GLA_QK_W)),
                 ("v", shp(GLA_V_W, BF16), _row_spec(GLA_V_W)), ("r", shp(GLA_V_W, BF16), _row_spec(GLA_V_W)),
                 ("lg_fwd", shp(GLA_QK_W, F32), _row_spec(GLA_QK_W)),
                 ("lg_rev", shp(GLA_QK_W, F32), _row_spec(GLA_QK_W))]
    else:
        n_seq_steps = seq // STEP_ROWS
        real_steps = (n_rows - ROW_TILE) // STEP_ROWS
        table = pl.BlockSpec((STEP_ROWS, ATT_HEAD_DIM),
                             lambda i: (jnp.where(i < real_steps, i % n_seq_steps, n_seq_steps), 0))
        cos, sin = _rope_tables(seq)
        ins += [("cos", cos, table), ("sin", sin, table)]
        outs += [("q", shp(ATT_Q_W, BF16), _row_spec(ATT_Q_W)), ("k", shp(ATT_KV_W, BF16), _row_spec(ATT_KV_W)),
                 ("kt", jax.ShapeDtypeStruct((ATT_KV_W, n_rows), BF16),
                  pl.BlockSpec((ATT_KV_W, STEP_ROWS), lambda i: (0, i))),
                 ("v", shp(2 * ATT_KV_W, BF16), _row_spec(2 * ATT_KV_W))]
    return _call(functools.partial(_stage_in_body, mixer, n_real_steps), (pl.cdiv(n_rows, STEP_ROWS),), ins, outs,
                 _act_scratch(), ("parallel",), f"token_stage_in_{mixer}")


def _stage_out_body(mixer, n_real_steps, final_norm, refs):
    def pipeline(tile):
        rows, env = _tile_rows(tile), {}

        def mix_out():
            h = refs["h"][rows, :]
            if mixer == "gla":
                o = (_meta_tile(tile, refs["o_fwd"][rows, :], refs.get("o_fwd_meta"), n_real_steps)
                     + _meta_tile(tile, refs["o_rev"][rows, :], refs.get("o_rev_meta"), n_real_steps))
                env["x"] = _gla_out(o, refs["r"][rows, :].astype(F32), h, refs)
            else:
                o = _meta_tile(tile, refs["o"][rows, :], refs.get("o_meta"), n_real_steps)
                env["x"] = h + _dot(o, refs["w_out"][...])

        def store():
            refs["h_out"][rows, :] = _rms(env["y"], refs["final_gain"][...]) if final_norm else env["y"]

        return [mix_out] + _swiglu_pieces(env, refs, refs[f"act{tile}"]) + [store]

    _interleave([pipeline(tile) for tile in range(STEP_TILES)], STAGE_LAG)


def _token_stage_out(mixer, core, h, w_out, ffn, extra, final_gain, n_real):
    last = final_gain is not None
    n_real_steps = n_real // STEP_ROWS
    n_rows = n_real if last else h.shape[0]
    has_meta = not last
    ins = [("h", h, _row_spec(D_MODEL))]
    if mixer == "gla":
        ins += [("o_fwd", core["o_fwd"], _row_spec(GLA_V_W, n_real_steps)),
                ("o_rev", core["o_rev"], _row_spec(GLA_V_W, n_real_steps)),
                ("r", extra["r"], _row_spec(GLA_V_W)), ("head_gain", extra["head_gain"], _const_spec((1, GLA_DV)))]
        if has_meta:
            ins += [("o_fwd_meta", core["o_fwd_meta"], _const_spec((ROW_TILE, GLA_V_W))),
                    ("o_rev_meta", core["o_rev_meta"], _const_spec((ROW_TILE, GLA_V_W)))]
    else:
        ins += [("o", core["o"], _row_spec(ATT_Q_W, n_real_steps))]
        if has_meta:
            ins += [("o_meta", core["o_meta"], _const_spec((ROW_TILE, ATT_Q_W)))]
    ins += [("w_out", w_out, _const_spec(w_out.shape))] + _ffn_operands(*ffn)
    if last:
        ins.append(("final_gain", final_gain, _const_spec((1, D_MODEL))))
    outs = [("h_out", jax.ShapeDtypeStruct((n_rows, D_MODEL), F32), _row_spec(D_MODEL))]
    return _call(functools.partial(_stage_out_body, mixer, n_real_steps, last), (pl.cdiv(n_rows, STEP_ROWS),),
                 ins, outs, _act_scratch(), ("parallel",), f"token_stage_out_{mixer}")["h_out"]


def _gla_chunks_local(items):
    t_len = items[0][1].shape[0]
    row = lax.broadcasted_iota(jnp.int32, (t_len, t_len), 0)
    col = lax.broadcasted_iota(jnp.int32, (t_len, t_len), 1)
    row3 = lax.broadcasted_iota(jnp.int32, (t_len, 3 * t_len), 0)
    col3 = lax.broadcasted_iota(jnp.int32, (t_len, 3 * t_len), 1) % t_len
    mask = {False: col <= row, True: col >= row}
    mask3 = {False: (col3 <= row3).astype(BF16), True: (col3 >= row3).astype(BF16)}
    pieces = []
    for _, _, _, _, lg in items:
        hi = lg.astype(BF16)
        rest = lg - hi.astype(F32)
        mid = rest.astype(BF16)
        pieces.append(jnp.concatenate([hi, mid, (rest - mid.astype(F32)).astype(BF16)], axis=0))
    bs = [_dot(mask3[item[0]], p) for item, p in zip(items, pieces)]
    scaled = []
    for (rev, q, k, _, _), b in zip(items, bs):
        b_tot = b[0:1, :] if rev else b[t_len - 1:t_len, :]
        scaled.append(((q * jnp.exp(b)).astype(BF16), (k * jnp.exp(-b)).astype(BF16),
                       (k * jnp.exp(b_tot - b)).astype(BF16), jnp.exp(b_tot)))
    heads = [slice(h * GLA_DK, (h + 1) * GLA_DK) for h in range(GLA_HEADS)]
    scores = [[_dot_nt(qd[:, ks], ki[:, ks]) for ks in heads] for qd, ki, _, _ in scaled]
    lhs = [[jnp.concatenate([jnp.where(mask[item[0]], s, 0.0).astype(BF16), ke[:, ks].T], axis=0)
            for s, ks in zip(ss, heads)] for item, ss, (_, _, ke, _) in zip(items, scores, scaled)]
    local = [[_dot(l, item[3][:, h * GLA_DV:(h + 1) * GLA_DV]) for h, l in enumerate(ls)]
             for item, ls in zip(items, lhs)]
    return [(qd, dec, loc) for (qd, _, _, dec), loc in zip(scaled, local)]


def _gla_chunks_carry(chunks, states):
    t_len = chunks[0][0].shape[0]
    heads = [slice(h * GLA_DK, (h + 1) * GLA_DK) for h in range(GLA_HEADS)]
    carried = [[_dot(qd[:, ks], st.astype(BF16)) for ks, st in zip(heads, sts)]
               for (qd, _, _), sts in zip(chunks, states)]
    outs = [jnp.concatenate([loc[0:t_len, :] + c for loc, c in zip(local, cs)], axis=1)
            for (_, _, local), cs in zip(chunks, carried)]
    dec_cols = [[jnp.broadcast_to(dec[:, ks], (GLA_DK, GLA_DK)).T for ks in heads] for _, dec, _ in chunks]
    new_states = [[jnp.concatenate([dc] * (GLA_DV // GLA_DK), axis=1) * st + loc[t_len:, :]
                   for dc, st, loc in zip(dcs, sts, local)]
                  for dcs, sts, (_, _, local) in zip(dec_cols, states, chunks)]
    return outs, new_states


def _gla_scan_body(refs):
    j = pl.program_id(1)
    n_chunks = GLA_BLOCK // GLA_CHUNK
    pad = GLA_CHUNK - N_META
    streams = [(slot, rev) for slot in range(GLA_BATCHES) for rev in (False, True)]
    tag = lambda slot, rev: ("_rev" if rev else "_fwd") + str(slot)

    @pl.when(j == 0)
    def _():
        for s in streams:
            refs["state" + tag(*s)][...] = jnp.zeros_like(refs["state" + tag(*s)])

    def load_states(s):
        return [refs["state" + tag(*s)][h] for h in range(GLA_HEADS)]

    def store_states(s, states):
        for h in range(GLA_HEADS):
            refs["state" + tag(*s)][h] = states[h]

    def meta_chunks(rev):
        front = lambda ref: jnp.concatenate([jnp.zeros((pad, ref.shape[1]), ref.dtype), ref[...]], axis=0)
        for slot in range(GLA_BATCHES):
            m = "_meta" + str(slot)
            (qd, dec, local), = _gla_chunks_local([(rev, front(refs["q" + m]), front(refs["k" + m]),
                                                    front(refs["v" + m]),
                                                    front(refs[("lg_rev" if rev else "lg_fwd") + m]))])
            (o,), (states,) = _gla_chunks_carry([(qd, dec, local)], [load_states((slot, rev))])
            store_states((slot, rev), states)
            refs["o_rev_meta" if rev else "o_fwd_meta"][slot] = o[pad:, :]

    pl.when(j == 0)(lambda: meta_chunks(False))

    rows = lambda c: slice(c * GLA_CHUNK, (c + 1) * GLA_CHUNK)
    states = {s: load_states(s) for s in streams}
    for ci in range(n_chunks):
        items = []
        for s in streams:
            c = n_chunks - 1 - ci if s[1] else ci
            items.append((s[1], refs["q" + tag(*s)][rows(c), :], refs["k" + tag(*s)][rows(c), :],
                          refs["v" + tag(*s)][rows(c), :], refs["lg" + tag(*s)][rows(c), :]))
        outs, new_states = _gla_chunks_carry(_gla_chunks_local(items), [states[s] for s in streams])
        for s, o, st in zip(streams, outs, new_states):
            c = n_chunks - 1 - ci if s[1] else ci
            states[s] = st
            refs["o_rev" if s[1] else "o_fwd"][s[0], rows(c), :] = o
    for s in streams:
        store_states(s, states[s])

    pl.when(j == pl.num_programs(1) - 1)(lambda: meta_chunks(True))


def _gla_scan(bsz, seq, t):
    assert bsz % GLA_BATCHES == 0
    n_blk = seq // GLA_BLOCK
    meta_blk0 = bsz * seq // N_META
    ins, scratch = [], []
    for slot in range(GLA_BATCHES):
        batch = lambda p, slot=slot: p * GLA_BATCHES + slot
        fwd = lambda w, batch=batch: pl.BlockSpec((GLA_BLOCK, w), lambda p, j: (batch(p) * n_blk + j, 0))
        rev = lambda w, batch=batch: pl.BlockSpec((GLA_BLOCK, w), lambda p, j: (batch(p) * n_blk + n_blk - 1 - j, 0))
        meta = lambda w, batch=batch: pl.BlockSpec((N_META, w), lambda p, j: (meta_blk0 + batch(p), 0))
        for sfx, spec in (("_fwd", fwd), ("_rev", rev)):
            sfx_slot = sfx + str(slot)
            ins += [("q" + sfx_slot, t["q"], spec(GLA_QK_W)), ("k" + sfx_slot, t["k"], spec(GLA_QK_W)),
                    ("v" + sfx_slot, t["v"], spec(GLA_V_W)), ("lg" + sfx_slot, t["lg" + sfx], spec(GLA_QK_W))]
            scratch.append(("state" + sfx_slot, pltpu.VMEM((GLA_HEADS, GLA_DK, GLA_DV), F32)))
        m = "_meta" + str(slot)
        ins += [("q" + m, t["q"], meta(GLA_QK_W)), ("k" + m, t["k"], meta(GLA_QK_W)), ("v" + m, t["v"], meta(GLA_V_W)),
                ("lg_fwd" + m, t["lg_fwd"], meta(GLA_QK_W)), ("lg_rev" + m, t["lg_rev"], meta(GLA_QK_W))]
    real_shape = jax.ShapeDtypeStruct((bsz, seq, GLA_V_W), F32)
    meta_shape = jax.ShapeDtypeStruct((bsz, N_META, GLA_V_W), F32)
    meta_out = pl.BlockSpec((GLA_BATCHES, N_META, GLA_V_W), lambda p, j: (p, 0, 0))
    outs = [("o_fwd", real_shape, pl.BlockSpec((GLA_BATCHES, GLA_BLOCK, GLA_V_W), lambda p, j: (p, j, 0))),
            ("o_rev", real_shape, pl.BlockSpec((GLA_BATCHES, GLA_BLOCK, GLA_V_W), lambda p, j: (p, n_blk - 1 - j, 0))),
            ("o_fwd_meta", meta_shape, meta_out), ("o_rev_meta", meta_shape, meta_out)]
    res = _call(_gla_scan_body, (bsz // GLA_BATCHES, n_blk), ins, outs, scratch, ("parallel", "arbitrary"), "gla_scan")
    return {"o_fwd": res["o_fwd"].reshape(bsz * seq, GLA_V_W), "o_rev": res["o_rev"].reshape(bsz * seq, GLA_V_W),
            "o_fwd_meta": res["o_fwd_meta"].reshape(bsz * N_META, GLA_V_W),
            "o_rev_meta": res["o_rev_meta"].reshape(bsz * N_META, GLA_V_W)}


def _attn_scores_block(q, kt_ref, kb, s_ref, mx):
    cols = slice(kb * ATT_KV_BLOCK, (kb + 1) * ATT_KV_BLOCK)
    s = _dot(q, kt_ref[:, cols])
    s_ref[kb] = s
    tiles = [s[:, c * LANES:(c + 1) * LANES] for c in range(ATT_KV_BLOCK // LANES)]
    while len(tiles) > 1:
        tiles = [jnp.maximum(tiles[i], tiles[i + 1]) for i in range(0, len(tiles), 2)]
    return tiles[0] if mx is None else jnp.maximum(mx, tiles[0])


def _attn_scores_finish(q, km, mx, sm_ref, m_ref):
    sm = _dot_nt(q, km)
    sm_ref[...] = sm
    m = jnp.maximum(jnp.max(mx, axis=1, keepdims=True), jnp.max(sm, axis=1, keepdims=True))
    m_ref[...] = jnp.broadcast_to(m, m_ref.shape)


def _attn_weighted_start(sm_ref, m_ref, vm):
    m = m_ref[...]
    return m, _dot(jnp.exp2(sm_ref[...] - m[:, 0:N_META]).astype(BF16), vm)


def _attn_weighted_block(s_ref, kb, m, v_ref, acc):
    rows = slice(kb * ATT_KV_BLOCK, (kb + 1) * ATT_KV_BLOCK)
    p = [jnp.exp2(s_ref[kb, :, c * LANES:(c + 1) * LANES] - m) for c in range(ATT_KV_BLOCK // LANES)]
    return acc + _dot(jnp.concatenate(p, axis=1).astype(BF16), v_ref[rows, :])


def _attn_core_body(n_kv_blocks, n_q, n_tiles, meta_queries, refs):
    t = pl.program_id(0)
    kt_ref, v_ref = refs["kt"], refs["v"]
    km = refs["k_meta"][...]
    vm = refs["v_meta"][...]
    stash = [(refs[f"s{i}"], refs[f"sm{i}"], refs[f"m{i}"]) for i in range(2)]

    def stack(ref):
        return jnp.concatenate([ref[:, g * ATT_HEAD_DIM:(g + 1) * ATT_HEAD_DIM] for g in range(ATT_GROUP)], axis=0)

    def unstack(acc, ref):
        o = acc[:, 0:ATT_HEAD_DIM] / acc[:, ATT_HEAD_DIM:]
        rows = ref.shape[0]
        for g in range(ATT_GROUP):
            ref[:, g * ATT_HEAD_DIM:(g + 1) * ATT_HEAD_DIM] = o[g * rows:(g + 1) * rows, :].astype(BF16)

    @pl.when(t == 0)
    def _():
        for ref in stash[1]:
            ref[...] = jnp.zeros_like(ref)

    def step(write, read):
        ws_ref, wsm_ref, wm_ref = write
        rs_ref, rsm_ref, rm_ref = read
        q = stack(refs["q"])
        m, acc = _attn_weighted_start(rsm_ref, rm_ref, vm)
        mx = None
        for kb in range(n_kv_blocks):
            acc = _attn_weighted_block(rs_ref, kb, m, v_ref, acc)
            mx = _attn_scores_block(q, kt_ref, kb, ws_ref, mx)
        _attn_scores_finish(q, km, mx, wsm_ref, wm_ref)
        unstack(acc, refs["o"])

    pl.when(t % 2 == 0)(lambda: step(stash[0], stash[1]))
    pl.when(t % 2 == 1)(lambda: step(stash[1], stash[0]))

    if meta_queries:
        @pl.when(jnp.minimum(t, n_tiles - 1) % n_q == n_q - 1)
        def _():
            sq_ref, smq_ref, mq_ref = refs["sq"], refs["smq"], refs["mq"]
            q = stack(refs["q_meta"])
            mx = None
            for kb in range(n_kv_blocks):
                mx = _attn_scores_block(q, kt_ref, kb, sq_ref, mx)
            _attn_scores_finish(q, km, mx, smq_ref, mq_ref)
            m, acc = _attn_weighted_start(smq_ref, mq_ref, vm)
            for kb in range(n_kv_blocks):
                acc = _attn_weighted_block(sq_ref, kb, m, v_ref, acc)
            unstack(acc, refs["o_meta"])


def _attn_core(bsz, seq, t, meta_queries):
    n_q = seq // ATT_Q_TILE
    assert n_q >= 2
    n_tiles = bsz * ATT_KV_HEADS * n_q
    meta_blk0 = bsz * seq // N_META
    n_kv_blocks = seq // ATT_KV_BLOCK
    group_w = ATT_GROUP * ATT_HEAD_DIM

    def coords(tile):
        return tile // (ATT_KV_HEADS * n_q), (tile // n_q) % ATT_KV_HEADS, tile % n_q

    def spec(shape, which, fn):
        tile = (lambda s: jnp.minimum(s, n_tiles - 1)) if which == "cur" else (lambda s: jnp.maximum(s - 1, 0))
        return pl.BlockSpec(shape, lambda s: fn(*coords(tile(s))))

    def stash(sfx, rows):
        return [("s" + sfx, pltpu.VMEM((n_kv_blocks, rows, ATT_KV_BLOCK), F32)),
                ("sm" + sfx, pltpu.VMEM((rows, N_META), F32)), ("m" + sfx, pltpu.VMEM((rows, LANES), F32))]

    ins = [("q", t["q"], spec((ATT_Q_TILE, group_w), "cur", lambda b, kh, i: (b * n_q + i, kh))),
           ("kt", t["kt"], spec((ATT_HEAD_DIM, seq), "cur", lambda b, kh, i: (kh, b))),
           ("k_meta", t["k"], spec((N_META, ATT_HEAD_DIM), "cur", lambda b, kh, i: (meta_blk0 + b, kh))),
           ("v", t["v"], spec((seq, 2 * ATT_HEAD_DIM), "prev", lambda b, kh, i: (b, kh))),
           ("v_meta", t["v"], spec((N_META, 2 * ATT_HEAD_DIM), "prev", lambda b, kh, i: (meta_blk0 + b, kh)))]
    outs = [("o", jax.ShapeDtypeStruct((bsz * seq, ATT_Q_W), BF16),
             spec((ATT_Q_TILE, group_w), "prev", lambda b, kh, i: (b * n_q + i, kh)))]
    scratch = stash("0", ATT_GROUP * ATT_Q_TILE) + stash("1", ATT_GROUP * ATT_Q_TILE)
    if meta_queries:
        ins.append(("q_meta", t["q"], spec((N_META, group_w), "cur", lambda b, kh, i: (meta_blk0 + b, kh))))
        outs.append(("o_meta", jax.ShapeDtypeStruct((bsz * N_META, ATT_Q_W), BF16),
                     spec((N_META, group_w), "cur", lambda b, kh, i: (b, kh))))
        scratch += stash("q", ATT_GROUP * N_META)
    return _call(functools.partial(_attn_core_body, n_kv_blocks, n_q, n_tiles, meta_queries), (n_tiles + 1,),
                 ins, outs, scratch, ("arbitrary",), "attn_core")


def kernel(x, meta_tokens, norm_ffn1, ffn1_w_gate, ffn1_w_up, ffn1_w_down, norm_mix, gla_w_in, gla_gate_w1, gla_gate_w2, gla_gate_b, gla_head_norm, gla_w_out, attn_w_in, attn_q_norm, attn_k_norm, attn_w_out, norm_ffn2, ffn2_w_gate, ffn2_w_up, ffn2_w_down, norm_final):
    bsz, seq, d = x.shape
    depth = norm_ffn1.shape[0]
    assert d == D_MODEL and bsz * N_META == ROW_TILE and seq % STEP_ROWS == 0 and seq % ATT_KV_BLOCK == 0
    n_real = bsz * seq
    row = lambda g: g.reshape(1, -1).astype(F32)
    bf = lambda w: w.astype(BF16)

    h = x.reshape(n_real, d)
    h_meta = jnp.broadcast_to(meta_tokens.astype(x.dtype)[None], (bsz, N_META, d)).reshape(bsz * N_META, d)

    for i in range(depth):
        j = i // N_MIXERS
        last = i == depth - 1
        ffn1 = (row(norm_ffn1[i]), bf(ffn1_w_gate[i]), bf(ffn1_w_up[i]), bf(ffn1_w_down[i]))
        ffn2 = (row(norm_ffn2[i]), bf(ffn2_w_gate[i]), bf(ffn2_w_up[i]), bf(ffn2_w_down[i]))
        final_gain = row(norm_final) if last else None
        if i % N_MIXERS == 0:
            zero = jnp.zeros_like(gla_gate_w2[j, 0])
            gate_w2 = jnp.concatenate([jnp.concatenate([gla_gate_w2[j, 0], zero], axis=1),
                                       jnp.concatenate([zero, gla_gate_w2[j, 1]], axis=1)], axis=0)
            mixer_ins = [("w_in", bf(gla_w_in[j])),
                         ("gate_w1", bf(jnp.concatenate([gla_gate_w1[j, 0], gla_gate_w1[j, 1]], axis=1))),
                         ("gate_w2", bf(gate_w2)), ("gate_b", gla_gate_b[j].reshape(1, -1).astype(F32))]
            t = _token_stage_in("gla", h, h_meta, ffn1, row(norm_mix[i]), mixer_ins, seq)
            core = _gla_scan(bsz, seq, t)
            h = _token_stage_out("gla", core, t["h_out"], bf(gla_w_out[j]), ffn2,
                                 {"r": t["r"], "head_gain": row(gla_head_norm[j])}, final_gain, n_real)
        else:
            mixer_ins = [("w_in", bf(attn_w_in[j])), ("q_gain", row(attn_q_norm[j])), ("k_gain", row(attn_k_norm[j]))]
            t = _token_stage_in("attn", h, h_meta, ffn1, row(norm_mix[i]), mixer_ins, seq)
            core = _attn_core(bsz, seq, t, meta_queries=not last)
            h = _token_stage_out("attn", core, t["h_out"], bf(attn_w_out[j]), ffn2, {}, final_gain, n_real)
        h_meta = None
    return h.reshape(bsz, seq, d)
```

```python
import functools

import jax
import jax.numpy as jnp
from jax import lax
from jax.experimental import pallas as pl
from jax.experimental.pallas import tpu as pltpu

F32 = jnp.float32
BF16 = jnp.bfloat16

D_MODEL = 1024
N_META = 16
GRID_W = 64
D_FF = 2816
NORM_EPS = 1e-6
MACARON_WEIGHT = 0.5
N_MIXERS = 2

GLA_HEADS = 4
GLA_DK = D_MODEL // 2 // GLA_HEADS
GLA_DV = D_MODEL // GLA_HEADS
GLA_QK_W = GLA_HEADS * GLA_DK
GLA_V_W = GLA_HEADS * GLA_DV
GLA_GATE_RANK = 16
GLA_GATE_TAU = 16.0
GLA_CHUNK = 64

ATT_Q_HEADS = 8
ATT_KV_HEADS = 2
ATT_HEAD_DIM = D_MODEL // ATT_Q_HEADS
ATT_GROUP = ATT_Q_HEADS // ATT_KV_HEADS
ATT_Q_W = ATT_Q_HEADS * ATT_HEAD_DIM
ATT_KV_W = ATT_KV_HEADS * ATT_HEAD_DIM
ROPE_THETA = 10000.0
ROPE_AXIS_DIM = ATT_HEAD_DIM // 2
LOG2_E = 1.4426950408889634

LANES = 128
ROW_TILE = 256
STEP_TILES = 2
STEP_ROWS = STEP_TILES * ROW_TILE
STAGE_LAG = 5
FF_CHUNK = 256
GLA_BLOCK = 256
GLA_BATCHES = 2
ATT_Q_TILE = 512
ATT_KV_BLOCK = 512
VMEM_LIMIT = 56 * 1024 * 1024


def _params(semantics):
    return pltpu.CompilerParams(dimension_semantics=semantics, vmem_limit_bytes=VMEM_LIMIT)


def _const_spec(shape):
    nd = len(shape)
    return pl.BlockSpec(shape, lambda *_: (0,) * nd, pipeline_mode=pl.Buffered(1))


def _rms(x, gain):
    ms = jnp.mean(x * x, axis=-1, keepdims=True)
    return x * lax.rsqrt(ms + NORM_EPS) * gain


def _dot(a, b):
    return jnp.dot(a, b, preferred_element_type=F32)


def _dot_nt(a, b):
    return lax.dot_general(a, b, (((1,), (1,)), ((), ())), preferred_element_type=F32)


def _dot_tn(a, b):
    return lax.dot_general(a, b, (((0,), (0,)), ((), ())), preferred_element_type=F32)


def _call(body, grid, ins, outs, scratch, semantics, name):
    keys = [k for k, _, _ in ins] + [k for k, _, _ in outs] + [k for k, _ in scratch]

    def kernel_body(*refs):
        body(dict(zip(keys, refs, strict=True)))

    results = pl.pallas_call(
        kernel_body,
        grid=grid,
        in_specs=[s for _, _, s in ins],
        out_specs=[s for _, _, s in outs],
        out_shape=[a for _, a, _ in outs],
        scratch_shapes=[s for _, s in scratch],
        compiler_params=_params(semantics),
        name=name,
    )(*[a for _, a, _ in ins])
    return dict(zip([k for k, _, _ in outs], results, strict=True))


def _row_spec(width, clamp_steps=None):
    if clamp_steps is None:
        return pl.BlockSpec((STEP_ROWS, width), lambda i: (i, 0))
    return pl.BlockSpec((STEP_ROWS, width), lambda i: (jnp.minimum(i, clamp_steps - 1), 0))


def _tile_rows(tile):
    return slice(tile * ROW_TILE, (tile + 1) * ROW_TILE)


def _meta_tile(tile, real, meta_ref, n_real_steps):
    if tile > 0 or meta_ref is None:
        return real
    return jnp.where(pl.program_id(0) == n_real_steps, meta_ref[...], real)


def _swiglu_pieces(env, refs, a_ref):
    def norm():
        env["xn"] = _rms(env["x"], refs["ffn_gain"][...]).astype(BF16)

    def chunk(c):
        cols = slice(c * FF_CHUNK, (c + 1) * FF_CHUNK)
        g = _dot(env["xn"], refs["w_gate"][:, cols])
        u = _dot(env["xn"], refs["w_up"][:, cols])
        a_ref[:, cols] = (g * jax.nn.sigmoid(g) * u).astype(BF16)

    def down():
        env["y"] = env["x"] + MACARON_WEIGHT * _dot(a_ref[...], refs["w_down"][...])

    return [norm] + [functools.partial(chunk, c) for c in range(D_FF // FF_CHUNK)] + [down]


def _interleave(pipelines, lag):
    n = max(len(p) for p in pipelines)
    for step in range(n + lag * (len(pipelines) - 1)):
        for i, p in enumerate(pipelines):
            k = step - i * lag
            if 0 <= k < len(p):
                p[k]()


def _gla_in_pieces(env, refs, rows):
    win_ref = refs["w_in"]

    def norm():
        env["yn"] = _rms(env["y"], refs["mix_gain"][...]).astype(BF16)

    def q():
        refs["q"][rows, :] = (_dot(env["yn"], win_ref[:, 0:GLA_QK_W]) * (GLA_DK ** -0.5)).astype(BF16)

    def k():
        refs["k"][rows, :] = _dot(env["yn"], win_ref[:, GLA_QK_W:2 * GLA_QK_W]).astype(BF16)

    def v(half):
        cols = slice(half * GLA_QK_W, (half + 1) * GLA_QK_W)
        refs["v"][rows, cols] = _dot(env["yn"], win_ref[:, 2 * GLA_QK_W + half * GLA_QK_W:
                                                        2 * GLA_QK_W + (half + 1) * GLA_QK_W]).astype(BF16)

    def r(half):
        cols = slice(half * GLA_QK_W, (half + 1) * GLA_QK_W)
        base = 2 * GLA_QK_W + GLA_V_W
        refs["r"][rows, cols] = _dot(env["yn"], win_ref[:, base + half * GLA_QK_W:
                                                        base + (half + 1) * GLA_QK_W]).astype(BF16)

    def gates():
        t = _dot(env["yn"], refs["gate_w1"][...]).astype(BF16)
        z = _dot(t, refs["gate_w2"][...]) + refs["gate_b"][...]
        logg = (jnp.minimum(z, 0.0) - jnp.log(1.0 + jnp.exp(-jnp.abs(z)))) * (1.0 / GLA_GATE_TAU)
        refs["lg_fwd"][rows, :] = logg[:, 0:GLA_QK_W]
        refs["lg_rev"][rows, :] = logg[:, GLA_QK_W:]

    return [norm, gates, q, k, functools.partial(v, 0), functools.partial(v, 1),
            functools.partial(r, 0), functools.partial(r, 1)]


def _gla_out(o, r, h, refs):
    gate = r * jax.nn.sigmoid(r)
    cols = []
    for hd in range(GLA_HEADS):
        vs = slice(hd * GLA_DV, (hd + 1) * GLA_DV)
        cols.append(_rms(o[:, vs], refs["head_gain"][...]) * gate[:, vs])
    return h + _dot(jnp.concatenate(cols, axis=1).astype(BF16), refs["w_out"][...])


def _rope(x, cos, sin_signed, first_half):
    partner = jnp.where(first_half, pltpu.roll(x, LANES - ROPE_AXIS_DIM // 2, 1),
                        pltpu.roll(x, ROPE_AXIS_DIM // 2, 1))
    return x * cos + partner * sin_signed


def _attn_in_pieces(env, refs, rows):
    win_ref = refs["w_in"]

    def norm():
        env["yn"] = _rms(env["y"], refs["mix_gain"][...]).astype(BF16)
        cos = refs["cos"][rows, :]
        lane = lax.broadcasted_iota(jnp.int32, cos.shape, 1)
        env["rope"] = (cos, refs["sin"][rows, :], (lane % ROPE_AXIS_DIM) < (ROPE_AXIS_DIM // 2))

    def q(pair):
        cols = slice(2 * pair * ATT_HEAD_DIM, (2 * pair + 2) * ATT_HEAD_DIM)
        qq = _dot(env["yn"], win_ref[:, cols])
        for i in range(2):
            hs = slice(i * ATT_HEAD_DIM, (i + 1) * ATT_HEAD_DIM)
            qh = _rope(_rms(qq[:, hs], refs["q_gain"][...]), *env["rope"]) * (ATT_HEAD_DIM ** -0.5 * LOG2_E)
            refs["q"][rows, (2 * pair + i) * ATT_HEAD_DIM:(2 * pair + i + 1) * ATT_HEAD_DIM] = qh.astype(BF16)

    def kv():
        k = _dot(env["yn"], win_ref[:, ATT_Q_W:ATT_Q_W + ATT_KV_W])
        v = _dot(env["yn"], win_ref[:, ATT_Q_W + ATT_KV_W:]).astype(BF16)
        ones = jnp.ones((v.shape[0], ATT_HEAD_DIM), BF16)
        for hd in range(ATT_KV_HEADS):
            hs = slice(hd * ATT_HEAD_DIM, (hd + 1) * ATT_HEAD_DIM)
            kh = _rope(_rms(k[:, hs], refs["k_gain"][...]), *env["rope"])
            refs["k"][rows, hs] = kh.astype(BF16)
            refs["kt"][hs, rows] = kh.astype(BF16).T
            refs["v"][rows, 2 * hd * ATT_HEAD_DIM:(2 * hd + 1) * ATT_HEAD_DIM] = v[:, hs]
            refs["v"][rows, (2 * hd + 1) * ATT_HEAD_DIM:(2 * hd + 2) * ATT_HEAD_DIM] = ones

    return [norm] + [functools.partial(q, pair) for pair in range(ATT_Q_HEADS // 2)] + [kv]


def _rope_tables(seq):
    pos = jnp.arange(seq)
    inv_freq = ROPE_THETA ** (-jnp.arange(0, ROPE_AXIS_DIM, 2, dtype=F32) / ROPE_AXIS_DIM)
    ang_row = (pos // GRID_W).astype(F32)[:, None] * inv_freq[None, :]
    ang_col = (pos % GRID_W).astype(F32)[:, None] * inv_freq[None, :]
    ang = jnp.concatenate([ang_row, ang_row, ang_col, ang_col], axis=1)
    ang = jnp.concatenate([ang, jnp.zeros((STEP_ROWS, ATT_HEAD_DIM), F32)], axis=0)
    sign = jnp.tile(jnp.concatenate([-jnp.ones((ROPE_AXIS_DIM // 2,), F32),
                                     jnp.ones((ROPE_AXIS_DIM // 2,), F32)]), 2)
    return jnp.cos(ang), jnp.sin(ang) * sign[None, :]


def _act_scratch():
    return [(f"act{tile}", pltpu.VMEM((ROW_TILE, D_FF), BF16)) for tile in range(STEP_TILES)]


def _ffn_operands(gain, wg, wu, wd):
    return [("ffn_gain", gain, _const_spec((1, D_MODEL))), ("w_gate", wg, _const_spec((D_MODEL, D_FF))),
            ("w_up", wu, _const_spec((D_MODEL, D_FF))), ("w_down", wd, _const_spec((D_FF, D_MODEL)))]


def _stage_in_body(mixer, n_real_steps, refs):
    def pipeline(tile):
        rows, env = _tile_rows(tile), {}

        def load():
            env["x"] = _meta_tile(tile, refs["h"][rows, :], refs.get("h_meta"), n_real_steps)

        def store():
            refs["h_out"][rows, :] = env["y"]

        mixer_in = _gla_in_pieces if mixer == "gla" else _attn_in_pieces
        return [load] + _swiglu_pieces(env, refs, refs[f"act{tile}"]) + [store] + mixer_in(env, refs, rows)

    _interleave([pipeline(tile) for tile in range(STEP_TILES)], STAGE_LAG)


def _token_stage_in(mixer, h, h_meta, ffn, mix_gain, mixer_ins, seq):
    n_real_steps = h.shape[0] // STEP_ROWS
    n_rows = h.shape[0] + (h_meta.shape[0] if h_meta is not None else 0)
    ins = [("h", h, _row_spec(D_MODEL, n_real_steps if h_meta is not None else None))]
    if h_meta is not None:
        assert h_meta.shape[0] == ROW_TILE
        ins.append(("h_meta", h_meta, _const_spec((ROW_TILE, D_MODEL))))
    ins += _ffn_operands(*ffn) + [("mix_gain", mix_gain, _const_spec((1, D_MODEL)))]
    ins += [(key, arr, _const_spec(arr.shape)) for key, arr in mixer_ins]
    shp = lambda w, dt: jax.ShapeDtypeStruct((n_rows, w), dt)
    outs = [("h_out", shp(D_MODEL, F32), _row_spec(D_MODEL))]
    if mixer == "gla":
        outs += [("q", shp(GLA_QK_W, BF16), _row_spec(GLA_QK_W)), ("k", shp(GLA_QK_W, BF16), _row_spec(GLA_QK_W)),
                 ("v", shp(GLA_V_W, BF16), _row_spec(GLA_V_W)), ("r", shp(GLA_V_W, BF16), _row_spec(GLA_V_W)),
                 ("lg_fwd", shp(GLA_QK_W, F32), _row_spec(GLA_QK_W)),
                 ("lg_rev", shp(GLA_QK_W, F32), _row_spec(GLA_QK_W))]
    else:
        n_seq_steps = seq // STEP_ROWS
        real_steps = (n_rows - ROW_TILE) // STEP_ROWS
        table = pl.BlockSpec((STEP_ROWS, ATT_HEAD_DIM),
                             lambda i: (jnp.where(i < real_steps, i % n_seq_steps, n_seq_steps), 0))
        cos, sin = _rope_tables(seq)
        ins += [("cos", cos, table), ("sin", sin, table)]
        outs += [("q", shp(ATT_Q_W, BF16), _row_spec(ATT_Q_W)), ("k", shp(ATT_KV_W, BF16), _row_spec(ATT_KV_W)),
                 ("kt", jax.ShapeDtypeStruct((ATT_KV_W, n_rows), BF16),
                  pl.BlockSpec((ATT_KV_W, STEP_ROWS), lambda i: (0, i))),
                 ("v", shp(2 * ATT_KV_W, BF16), _row_spec(2 * ATT_KV_W))]
    return _call(functools.partial(_stage_in_body, mixer, n_real_steps), (pl.cdiv(n_rows, STEP_ROWS),), ins, outs,
                 _act_scratch(), ("parallel",), f"token_stage_in_{mixer}")


def _stage_out_body(mixer, n_real_steps, final_norm, refs):
    def pipeline(tile):
        rows, env = _tile_rows(tile), {}

        def mix_out():
            h = refs["h"][rows, :]
            if mixer == "gla":
                o = (_meta_tile(tile, refs["o_fwd"][rows, :], refs.get("o_fwd_meta"), n_real_steps)
                     + _meta_tile(tile, refs["o_rev"][rows, :], refs.get("o_rev_meta"), n_real_steps))
                env["x"] = _gla_out(o, refs["r"][rows, :].astype(F32), h, refs)
            else:
                o = _meta_tile(tile, refs["o"][rows, :], refs.get("o_meta"), n_real_steps)
                env["x"] = h + _dot(o, refs["w_out"][...])

        def store():
            refs["h_out"][rows, :] = _rms(env["y"], refs["final_gain"][...]) if final_norm else env["y"]

        return [mix_out] + _swiglu_pieces(env, refs, refs[f"act{tile}"]) + [store]

    _interleave([pipeline(tile) for tile in range(STEP_TILES)], STAGE_LAG)


def _token_stage_out(mixer, core, h, w_out, ffn, extra, final_gain, n_real):
    last = final_gain is not None
    n_real_steps = n_real // STEP_ROWS
    n_rows = n_real if last else h.shape[0]
    has_meta = not last
    ins = [("h", h, _row_spec(D_MODEL))]
    if mixer == "gla":
        ins += [("o_fwd", core["o_fwd"], _row_spec(GLA_V_W, n_real_steps)),
                ("o_rev", core["o_rev"], _row_spec(GLA_V_W, n_real_steps)),
                ("r", extra["r"], _row_spec(GLA_V_W)), ("head_gain", extra["head_gain"], _const_spec((1, GLA_DV)))]
        if has_meta:
            ins += [("o_fwd_meta", core["o_fwd_meta"], _const_spec((ROW_TILE, GLA_V_W))),
                    ("o_rev_meta", core["o_rev_meta"], _const_spec((ROW_TILE, GLA_V_W)))]
    else:
        ins += [("o", core["o"], _row_spec(ATT_Q_W, n_real_steps))]
        if has_meta:
            ins += [("o_meta", core["o_meta"], _const_spec((ROW_TILE, ATT_Q_W)))]
    ins += [("w_out", w_out, _const_spec(w_out.shape))] + _ffn_operands(*ffn)
    if last:
        ins.append(("final_gain", final_gain, _const_spec((1, D_MODEL))))
    outs = [("h_out", jax.ShapeDtypeStruct((n_rows, D_MODEL), F32), _row_spec(D_MODEL))]
    return _call(functools.partial(_stage_out_body, mixer, n_real_steps, last), (pl.cdiv(n_rows, STEP_ROWS),),
                 ins, outs, _act_scratch(), ("parallel",), f"token_stage_out_{mixer}")["h_out"]


def _gla_chunks_local(items):
    t_len = items[0][1].shape[0]
    row = lax.broadcasted_iota(jnp.int32, (t_len, t_len), 0)
    col = lax.broadcasted_iota(jnp.int32, (t_len, t_len), 1)
    row3 = lax.broadcasted_iota(jnp.int32, (t_len, 3 * t_len), 0)
    col3 = lax.broadcasted_iota(jnp.int32, (t_len, 3 * t_len), 1) % t_len
    mask = {False: col <= row, True: col >= row}
    mask3 = {False: (col3 <= row3).astype(BF16), True: (col3 >= row3).astype(BF16)}
    pieces = []
    for _, _, _, _, lg in items:
        hi = lg.astype(BF16)
        rest = lg - hi.astype(F32)
        mid = rest.astype(BF16)
        pieces.append(jnp.concatenate([hi, mid, (rest - mid.astype(F32)).astype(BF16)], axis=0))
    bs = [_dot(mask3[item[0]], p) for item, p in zip(items, pieces)]
    scaled = []
    for (rev, q, k, _, _), b in zip(items, bs):
        b_tot = b[0:1, :] if rev else b[t_len - 1:t_len, :]
        scaled.append(((q * jnp.exp(b)).astype(BF16), (k * jnp.exp(-b)).astype(BF16),
                       (k * jnp.exp(b_tot - b)).astype(BF16), jnp.exp(b_tot)))
    heads = [slice(h * GLA_DK, (h + 1) * GLA_DK) for h in range(GLA_HEADS)]
    scores = [[_dot_nt(qd[:, ks], ki[:, ks]) for ks in heads] for qd, ki, _, _ in scaled]
    lhs = [[jnp.concatenate([jnp.where(mask[item[0]], s, 0.0).astype(BF16), ke[:, ks].T], axis=0)
            for s, ks in zip(ss, heads)] for item, ss, (_, _, ke, _) in zip(items, scores, scaled)]
    local = [[_dot(l, item[3][:, h * GLA_DV:(h + 1) * GLA_DV]) for h, l in enumerate(ls)]
             for item, ls in zip(items, lhs)]
    return [(qd, dec, loc) for (qd, _, _, dec), loc in zip(scaled, local)]


def _gla_chunks_carry(chunks, states):
    t_len = chunks[0][0].shape[0]
    heads = [slice(h * GLA_DK, (h + 1) * GLA_DK) for h in range(GLA_HEADS)]
    carried = [[_dot(qd[:, ks], st.astype(BF16)) for ks, st in zip(heads, sts)]
               for (qd, _, _), sts in zip(chunks, states)]
    outs = [jnp.concatenate([loc[0:t_len, :] + c for loc, c in zip(local, cs)], axis=1)
            for (_, _, local), cs in zip(chunks, carried)]
    dec_cols = [[jnp.broadcast_to(dec[:, ks], (GLA_DK, GLA_DK)).T for ks in heads] for _, dec, _ in chunks]
    new_states = [[jnp.concatenate([dc] * (GLA_DV // GLA_DK), axis=1) * st + loc[t_len:, :]
                   for dc, st, loc in zip(dcs, sts, local)]
                  for dcs, sts, (_, _, local) in zip(dec_cols, states, chunks)]
    return outs, new_states


def _gla_scan_body(refs):
    j = pl.program_id(1)
    n_chunks = GLA_BLOCK // GLA_CHUNK
    pad = GLA_CHUNK - N_META
    streams = [(slot, rev) for slot in range(GLA_BATCHES) for rev in (False, True)]
    tag = lambda slot, rev: ("_rev" if rev else "_fwd") + str(slot)

    @pl.when(j == 0)
    def _():
        for s in streams:
            refs["state" + tag(*s)][...] = jnp.zeros_like(refs["state" + tag(*s)])

    def load_states(s):
        return [refs["state" + tag(*s)][h] for h in range(GLA_HEADS)]

    def store_states(s, states):
        for h in range(GLA_HEADS):
            refs["state" + tag(*s)][h] = states[h]

    def meta_chunks(rev):
        front = lambda ref: jnp.concatenate([jnp.zeros((pad, ref.shape[1]), ref.dtype), ref[...]], axis=0)
        for slot in range(GLA_BATCHES):
            m = "_meta" + str(slot)
            (qd, dec, local), = _gla_chunks_local([(rev, front(refs["q" + m]), front(refs["k" + m]),
                                                    front(refs["v" + m]),
                                                    front(refs[("lg_rev" if rev else "lg_fwd") + m]))])
            (o,), (states,) = _gla_chunks_carry([(qd, dec, local)], [load_states((slot, rev))])
            store_states((slot, rev), states)
            refs["o_rev_meta" if rev else "o_fwd_meta"][slot] = o[pad:, :]

    pl.when(j == 0)(lambda: meta_chunks(False))

    rows = lambda c: slice(c * GLA_CHUNK, (c + 1) * GLA_CHUNK)
    states = {s: load_states(s) for s in streams}
    for ci in range(n_chunks):
        items = []
        for s in streams:
            c = n_chunks - 1 - ci if s[1] else ci
            items.append((s[1], refs["q" + tag(*s)][rows(c), :], refs["k" + tag(*s)][rows(c), :],
                          refs["v" + tag(*s)][rows(c), :], refs["lg" + tag(*s)][rows(c), :]))
        outs, new_states = _gla_chunks_carry(_gla_chunks_local(items), [states[s] for s in streams])
        for s, o, st in zip(streams, outs, new_states):
            c = n_chunks - 1 - ci if s[1] else ci
            states[s] = st
            refs["o_rev" if s[1] else "o_fwd"][s[0], rows(c), :] = o
    for s in streams:
        store_states(s, states[s])

    pl.when(j == pl.num_programs(1) - 1)(lambda: meta_chunks(True))


def _gla_scan(bsz, seq, t):
    assert bsz % GLA_BATCHES == 0
    n_blk = seq // GLA_BLOCK
    meta_blk0 = bsz * seq // N_META
    ins, scratch = [], []
    for slot in range(GLA_BATCHES):
        batch = lambda p, slot=slot: p * GLA_BATCHES + slot
        fwd = lambda w, batch=batch: pl.BlockSpec((GLA_BLOCK, w), lambda p, j: (batch(p) * n_blk + j, 0))
        rev = lambda w, batch=batch: pl.BlockSpec((GLA_BLOCK, w), lambda p, j: (batch(p) * n_blk + n_blk - 1 - j, 0))
        meta = lambda w, batch=batch: pl.BlockSpec((N_META, w), lambda p, j: (meta_blk0 + batch(p), 0))
        for sfx, spec in (("_fwd", fwd), ("_rev", rev)):
            sfx_slot = sfx + str(slot)
            ins += [("q" + sfx_slot, t["q"], spec(GLA_QK_W)), ("k" + sfx_slot, t["k"], spec(GLA_QK_W)),
                    ("v" + sfx_slot, t["v"], spec(GLA_V_W)), ("lg" + sfx_slot, t["lg" + sfx], spec(GLA_QK_W))]
            scratch.append(("state" + sfx_slot, pltpu.VMEM((GLA_HEADS, GLA_DK, GLA_DV), F32)))
        m = "_meta" + str(slot)
        ins += [("q" + m, t["q"], meta(GLA_QK_W)), ("k" + m, t["k"], meta(GLA_QK_W)), ("v" + m, t["v"], meta(GLA_V_W)),
                ("lg_fwd" + m, t["lg_fwd"], meta(GLA_QK_W)), ("lg_rev" + m, t["lg_rev"], meta(GLA_QK_W))]
    real_shape = jax.ShapeDtypeStruct((bsz, seq, GLA_V_W), F32)
    meta_shape = jax.ShapeDtypeStruct((bsz, N_META, GLA_V_W), F32)
    meta_out = pl.BlockSpec((GLA_BATCHES, N_META, GLA_V_W), lambda p, j: (p, 0, 0))
    outs = [("o_fwd", real_shape, pl.BlockSpec((GLA_BATCHES, GLA_BLOCK, GLA_V_W), lambda p, j: (p, j, 0))),
            ("o_rev", real_shape, pl.BlockSpec((GLA_BATCHES, GLA_BLOCK, GLA_V_W), lambda p, j: (p, n_blk - 1 - j, 0))),
            ("o_fwd_meta", meta_shape, meta_out), ("o_rev_meta", meta_shape, meta_out)]
    res = _call(_gla_scan_body, (bsz // GLA_BATCHES, n_blk), ins, outs, scratch, ("parallel", "arbitrary"), "gla_scan")
    return {"o_fwd": res["o_fwd"].reshape(bsz * seq, GLA_V_W), "o_rev": res["o_rev"].reshape(bsz * seq, GLA_V_W),
            "o_fwd_meta": res["o_fwd_meta"].reshape(bsz * N_META, GLA_V_W),
            "o_rev_meta": res["o_rev_meta"].reshape(bsz * N_META, GLA_V_W)}


def _attn_scores_block(q, kt_ref, kb, s_ref, mx):
    cols = slice(kb * ATT_KV_BLOCK, (kb + 1) * ATT_KV_BLOCK)
    s = _dot(q, kt_ref[:, cols])
    s_ref[kb] = s
    tiles = [s[:, c * LANES:(c + 1) * LANES] for c in range(ATT_KV_BLOCK // LANES)]
    while len(tiles) > 1:
        tiles = [jnp.maximum(tiles[i], tiles[i + 1]) for i in range(0, len(tiles), 2)]
    return tiles[0] if mx is None else jnp.maximum(mx, tiles[0])


def _attn_scores_finish(q, km, mx, sm_ref, m_ref):
    sm = _dot_nt(q, km)
    sm_ref[...] = sm
    m = jnp.maximum(jnp.max(mx, axis=1, keepdims=True), jnp.max(sm, axis=1, keepdims=True))
    m_ref[...] = jnp.broadcast_to(m, m_ref.shape)


def _attn_weighted_start(sm_ref, m_ref, vm):
    m = m_ref[...]
    return m, _dot(jnp.exp2(sm_ref[...] - m[:, 0:N_META]).astype(BF16), vm)


def _attn_weighted_block(s_ref, kb, m, v_ref, acc):
    rows = slice(kb * ATT_KV_BLOCK, (kb + 1) * ATT_KV_BLOCK)
    p = [jnp.exp2(s_ref[kb, :, c * LANES:(c + 1) * LANES] - m) for c in range(ATT_KV_BLOCK // LANES)]
    return acc + _dot(jnp.concatenate(p, axis=1).astype(BF16), v_ref[rows, :])


def _attn_core_body(n_kv_blocks, n_q, n_tiles, meta_queries, refs):
    t = pl.program_id(0)
    kt_ref, v_ref = refs["kt"], refs["v"]
    km = refs["k_meta"][...]
    vm = refs["v_meta"][...]
    s_ref, sm_ref, m_ref = refs["s"], refs["sm"], refs["m"]

    def stack(ref):
        return jnp.concatenate([ref[:, g * ATT_HEAD_DIM:(g + 1) * ATT_HEAD_DIM] for g in range(ATT_GROUP)], axis=0)

    def unstack(acc, ref):
        o = acc[:, 0:ATT_HEAD_DIM] / acc[:, ATT_HEAD_DIM:]
        rows = ref.shape[0]
        for g in range(ATT_GROUP):
            ref[:, g * ATT_HEAD_DIM:(g + 1) * ATT_HEAD_DIM] = o[g * rows:(g + 1) * rows, :].astype(BF16)

    @pl.when(t == 0)
    def _():
        for ref in (s_ref, sm_ref, m_ref):
            ref[...] = jnp.zeros_like(ref)

    q = stack(refs["q"])
    m, acc = _attn_weighted_start(sm_ref, m_ref, vm)
    mx = None
    for kb in range(n_kv_blocks):
        acc = _attn_weighted_block(s_ref, kb, m, v_ref, acc)
        mx = _attn_scores_block(q, kt_ref, kb, s_ref, mx)
    _attn_scores_finish(q, km, mx, sm_ref, m_ref)
    unstack(acc, refs["o"])

    if meta_queries:
        @pl.when(jnp.minimum(t, n_tiles - 1) % n_q == n_q - 1)
        def _():
            sq_ref, smq_ref, mq_ref = refs["sq"], refs["smq"], refs["mq"]
            q = stack(refs["q_meta"])
            mx = None
            for kb in range(n_kv_blocks):
                mx = _attn_scores_block(q, kt_ref, kb, sq_ref, mx)
            _attn_scores_finish(q, km, mx, smq_ref, mq_ref)
            m, acc = _attn_weighted_start(smq_ref, mq_ref, vm)
            for kb in range(n_kv_blocks):
                acc = _attn_weighted_block(sq_ref, kb, m, v_ref, acc)
            unstack(acc, refs["o_meta"])


def _attn_core(bsz, seq, t, meta_queries):
    n_q = seq // ATT_Q_TILE
    assert n_q >= 2
    n_tiles = bsz * ATT_KV_HEADS * n_q
    meta_blk0 = bsz * seq // N_META
    n_kv_blocks = seq // ATT_KV_BLOCK
    group_w = ATT_GROUP * ATT_HEAD_DIM

    def coords(tile):
        return tile // (ATT_KV_HEADS * n_q), (tile // n_q) % ATT_KV_HEADS, tile % n_q

    def spec(shape, which, fn):
        tile = (lambda s: jnp.minimum(s, n_tiles - 1)) if which == "cur" else (lambda s: jnp.maximum(s - 1, 0))
        return pl.BlockSpec(shape, lambda s: fn(*coords(tile(s))))

    def stash(sfx, rows):
        return [("s" + sfx, pltpu.VMEM((n_kv_blocks, rows, ATT_KV_BLOCK), F32)),
                ("sm" + sfx, pltpu.VMEM((rows, N_META), F32)), ("m" + sfx, pltpu.VMEM((rows, LANES), F32))]

    ins = [("q", t["q"], spec((ATT_Q_TILE, group_w), "cur", lambda b, kh, i: (b * n_q + i, kh))),
           ("kt", t["kt"], spec((ATT_HEAD_DIM, seq), "cur", lambda b, kh, i: (kh, b))),
           ("k_meta", t["k"], spec((N_META, ATT_HEAD_DIM), "cur", lambda b, kh, i: (meta_blk0 + b, kh))),
           ("v", t["v"], spec((seq, 2 * ATT_HEAD_DIM), "prev", lambda b, kh, i: (b, kh))),
           ("v_meta", t["v"], spec((N_META, 2 * ATT_HEAD_DIM), "prev", lambda b, kh, i: (meta_blk0 + b, kh)))]
    outs = [("o", jax.ShapeDtypeStruct((bsz * seq, ATT_Q_W), BF16),
             spec((ATT_Q_TILE, group_w), "prev", lambda b, kh, i: (b * n_q + i, kh)))]
    scratch = stash("", ATT_GROUP * ATT_Q_TILE)
    if meta_queries:
        ins.append(("q_meta", t["q"], spec((N_META, group_w), "cur", lambda b, kh, i: (meta_blk0 + b, kh))))
        outs.append(("o_meta", jax.ShapeDtypeStruct((bsz * N_META, ATT_Q_W), BF16),
                     spec((N_META, group_w), "cur", lambda b, kh, i: (b, kh))))
        scratch += stash("q", ATT_GROUP * N_META)
    return _call(functools.partial(_attn_core_body, n_kv_blocks, n_q, n_tiles, meta_queries), (n_tiles + 1,),
                 ins, outs, scratch, ("arbitrary",), "attn_core")


def kernel(x, meta_tokens, norm_ffn1, ffn1_w_gate, ffn1_w_up, ffn1_w_down, norm_mix, gla_w_in, gla_gate_w1, gla_gate_w2, gla_gate_b, gla_head_norm, gla_w_out, attn_w_in, attn_q_norm, attn_k_norm, attn_w_out, norm_ffn2, ffn2_w_gate, ffn2_w_up, ffn2_w_down, norm_final):
    bsz, seq, d = x.shape
    depth = norm_ffn1.shape[0]
    assert d == D_MODEL and bsz * N_META == ROW_TILE and seq % STEP_ROWS == 0 and seq % ATT_KV_BLOCK == 0
    n_real = bsz * seq
    row = lambda g: g.reshape(1, -1).astype(F32)
    bf = lambda w: w.astype(BF16)

    h = x.reshape(n_real, d)
    h_meta = jnp.broadcast_to(meta_tokens.astype(x.dtype)[None], (bsz, N_META, d)).reshape(bsz * N_META, d)

    for i in range(depth):
        j = i // N_MIXERS
        last = i == depth - 1
        ffn1 = (row(norm_ffn1[i]), bf(ffn1_w_gate[i]), bf(ffn1_w_up[i]), bf(ffn1_w_down[i]))
        ffn2 = (row(norm_ffn2[i]), bf(ffn2_w_gate[i]), bf(ffn2_w_up[i]), bf(ffn2_w_down[i]))
        final_gain = row(norm_final) if last else None
        if i % N_MIXERS == 0:
            zero = jnp.zeros_like(gla_gate_w2[j, 0])
            gate_w2 = jnp.concatenate([jnp.concatenate([gla_gate_w2[j, 0], zero], axis=1),
                                       jnp.concatenate([zero, gla_gate_w2[j, 1]], axis=1)], axis=0)
            mixer_ins = [("w_in", bf(gla_w_in[j])),
                         ("gate_w1", bf(jnp.concatenate([gla_gate_w1[j, 0], gla_gate_w1[j, 1]], axis=1))),
                         ("gate_w2", bf(gate_w2)), ("gate_b", gla_gate_b[j].reshape(1, -1).astype(F32))]
            t = _token_stage_in("gla", h, h_meta, ffn1, row(norm_mix[i]), mixer_ins, seq)
            core = _gla_scan(bsz, seq, t)
            h = _token_stage_out("gla", core, t["h_out"], bf(gla_w_out[j]), ffn2,
                                 {"r": t["r"], "head_gain": row(gla_head_norm[j])}, final_gain, n_real)
        else:
            mixer_ins = [("w_in", bf(attn_w_in[j])), ("q_gain", row(attn_q_norm[j])), ("k_gain", row(attn_k_norm[j]))]
            t = _token_stage_in("attn", h, h_meta, ffn1, row(norm_mix[i]), mixer_ins, seq)
            core = _attn_core(bsz, seq, t, meta_queries=not last)
            h = _token_stage_out("attn", core, t["h_out"], bf(attn_w_out[j]), ffn2, {}, final_gain, n_real)
        h_meta = None
    return h.reshape(bsz, seq, d)
```

```python
import functools

import jax
import jax.numpy as jnp
from jax import lax
from jax.experimental import pallas as pl
from jax.experimental.pallas import tpu as pltpu

F32 = jnp.float32
BF16 = jnp.bfloat16

D_MODEL = 1024
N_META = 16
GRID_W = 64
D_FF = 2816
NORM_EPS = 1e-6
MACARON_WEIGHT = 0.5
N_MIXERS = 2

GLA_HEADS = 4
GLA_DK = D_MODEL // 2 // GLA_HEADS
GLA_DV = D_MODEL // GLA_HEADS
GLA_QK_W = GLA_HEADS * GLA_DK
GLA_V_W = GLA_HEADS * GLA_DV
GLA_GATE_RANK = 16
GLA_GATE_TAU = 16.0
GLA_CHUNK = 64

ATT_Q_HEADS = 8
ATT_KV_HEADS = 2
ATT_HEAD_DIM = D_MODEL // ATT_Q_HEADS
ATT_GROUP = ATT_Q_HEADS // ATT_KV_HEADS
ATT_Q_W = ATT_Q_HEADS * ATT_HEAD_DIM
ATT_KV_W = ATT_KV_HEADS * ATT_HEAD_DIM
ROPE_THETA = 10000.0
ROPE_AXIS_DIM = ATT_HEAD_DIM // 2
LOG2_E = 1.4426950408889634

LANES = 128
ROW_TILE = 256
STEP_TILES = 2
STEP_ROWS = STEP_TILES * ROW_TILE
STAGE_LAG = 5
FF_CHUNK = 256
GLA_BLOCK = 256
GLA_BATCHES = 2
ATT_Q_TILE = 512
ATT_KV_BLOCK = 512
CAST_STEPS = 32
BF16_SUBLANES = 16
VMEM_LIMIT = 56 * 1024 * 1024


def _params(semantics):
    return pltpu.CompilerParams(dimension_semantics=semantics, vmem_limit_bytes=VMEM_LIMIT)


def _const_spec(shape):
    nd = len(shape)
    return pl.BlockSpec(shape, lambda *_: (0,) * nd, pipeline_mode=pl.Buffered(1))


def _rms(x, gain):
    ms = jnp.mean(x * x, axis=-1, keepdims=True)
    return x * lax.rsqrt(ms + NORM_EPS) * gain


def _dot(a, b):
    return jnp.dot(a, b, preferred_element_type=F32)


def _dot_nt(a, b):
    return lax.dot_general(a, b, (((1,), (1,)), ((), ())), preferred_element_type=F32)


def _dot_tn(a, b):
    return lax.dot_general(a, b, (((0,), (0,)), ((), ())), preferred_element_type=F32)


def _call(body, grid, ins, outs, scratch, semantics, name):
    keys = [k for k, _, _ in ins] + [k for k, _, _ in outs] + [k for k, _ in scratch]

    def kernel_body(*refs):
        body(dict(zip(keys, refs, strict=True)))

    results = pl.pallas_call(
        kernel_body,
        grid=grid,
        in_specs=[s for _, _, s in ins],
        out_specs=[s for _, _, s in outs],
        out_shape=[a for _, a, _ in outs],
        scratch_shapes=[s for _, s in scratch],
        compiler_params=_params(semantics),
        name=name,
    )(*[a for _, a, _ in ins])
    return dict(zip([k for k, _, _ in outs], results, strict=True))


def _row_spec(width, clamp_steps=None):
    if clamp_steps is None:
        return pl.BlockSpec((STEP_ROWS, width), lambda i: (i, 0))
    return pl.BlockSpec((STEP_ROWS, width), lambda i: (jnp.minimum(i, clamp_steps - 1), 0))


def _tile_rows(tile):
    return slice(tile * ROW_TILE, (tile + 1) * ROW_TILE)


def _meta_tile(tile, real, meta_ref, n_real_steps):
    if tile > 0 or meta_ref is None:
        return real
    return jnp.where(pl.program_id(0) == n_real_steps, meta_ref[...], real)


def _swiglu_pieces(env, refs, a_ref):
    def norm():
        env["xn"] = _rms(env["x"], refs["ffn_gain"][...]).astype(BF16)

    def chunk(c):
        cols = slice(c * FF_CHUNK, (c + 1) * FF_CHUNK)
        g = _dot(env["xn"], refs["w_gate"][:, cols])
        u = _dot(env["xn"], refs["w_up"][:, cols])
        a_ref[:, cols] = (g * jax.nn.sigmoid(g) * u).astype(BF16)

    def down():
        env["y"] = env["x"] + MACARON_WEIGHT * _dot(a_ref[...], refs["w_down"][...])

    return [norm] + [functools.partial(chunk, c) for c in range(D_FF // FF_CHUNK)] + [down]


def _interleave(pipelines, lag):
    n = max(len(p) for p in pipelines)
    for step in range(n + lag * (len(pipelines) - 1)):
        for i, p in enumerate(pipelines):
            k = step - i * lag
            if 0 <= k < len(p):
                p[k]()


def _gla_in_pieces(env, refs, rows):
    win_ref = refs["w_in"]

    def norm():
        env["yn"] = _rms(env["y"], refs["mix_gain"][...]).astype(BF16)

    def q():
        refs["q"][rows, :] = (_dot(env["yn"], win_ref[:, 0:GLA_QK_W]) * (GLA_DK ** -0.5)).astype(BF16)

    def k():
        refs["k"][rows, :] = _dot(env["yn"], win_ref[:, GLA_QK_W:2 * GLA_QK_W]).astype(BF16)

    def v(half):
        cols = slice(half * GLA_QK_W, (half + 1) * GLA_QK_W)
        refs["v"][rows, cols] = _dot(env["yn"], win_ref[:, 2 * GLA_QK_W + half * GLA_QK_W:
                                                        2 * GLA_QK_W + (half + 1) * GLA_QK_W]).astype(BF16)

    def r(half):
        cols = slice(half * GLA_QK_W, (half + 1) * GLA_QK_W)
        base = 2 * GLA_QK_W + GLA_V_W
        refs["r"][rows, cols] = _dot(env["yn"], win_ref[:, base + half * GLA_QK_W:
                                                        base + (half + 1) * GLA_QK_W]).astype(BF16)

    def gates():
        t = _dot(env["yn"], refs["gate_w1"][...]).astype(BF16)
        z = _dot(t, refs["gate_w2"][...]) + refs["gate_b"][...]
        logg = (jnp.minimum(z, 0.0) - jnp.log(1.0 + jnp.exp(-jnp.abs(z)))) * (1.0 / GLA_GATE_TAU)
        refs["lg_fwd"][rows, :] = logg[:, 0:GLA_QK_W]
        refs["lg_rev"][rows, :] = logg[:, GLA_QK_W:]

    return [norm, gates, q, k, functools.partial(v, 0), functools.partial(v, 1),
            functools.partial(r, 0), functools.partial(r, 1)]


def _gla_out(o, r, h, refs):
    gate = r * jax.nn.sigmoid(r)
    cols = []
    for hd in range(GLA_HEADS):
        vs = slice(hd * GLA_DV, (hd + 1) * GLA_DV)
        cols.append(_rms(o[:, vs], refs["head_gain"][...]) * gate[:, vs])
    return h + _dot(jnp.concatenate(cols, axis=1).astype(BF16), refs["w_out"][...])


def _rope(x, cos, sin_signed, first_half):
    partner = jnp.where(first_half, pltpu.roll(x, LANES - ROPE_AXIS_DIM // 2, 1),
                        pltpu.roll(x, ROPE_AXIS_DIM // 2, 1))
    return x * cos + partner * sin_signed


def _attn_in_pieces(env, refs, rows):
    win_ref = refs["w_in"]

    def norm():
        env["yn"] = _rms(env["y"], refs["mix_gain"][...]).astype(BF16)
        cos = refs["cos"][rows, :]
        lane = lax.broadcasted_iota(jnp.int32, cos.shape, 1)
        env["rope"] = (cos, refs["sin"][rows, :], (lane % ROPE_AXIS_DIM) < (ROPE_AXIS_DIM // 2))

    def q(pair):
        cols = slice(2 * pair * ATT_HEAD_DIM, (2 * pair + 2) * ATT_HEAD_DIM)
        qq = _dot(env["yn"], win_ref[:, cols])
        for i in range(2):
            hs = slice(i * ATT_HEAD_DIM, (i + 1) * ATT_HEAD_DIM)
            qh = _rope(_rms(qq[:, hs], refs["q_gain"][...]), *env["rope"]) * (ATT_HEAD_DIM ** -0.5 * LOG2_E)
            refs["q"][rows, (2 * pair + i) * ATT_HEAD_DIM:(2 * pair + i + 1) * ATT_HEAD_DIM] = qh.astype(BF16)

    def kv():
        k = _dot(env["yn"], win_ref[:, ATT_Q_W:ATT_Q_W + ATT_KV_W])
        v = _dot(env["yn"], win_ref[:, ATT_Q_W + ATT_KV_W:]).astype(BF16)
        ones = jnp.ones((v.shape[0], ATT_HEAD_DIM), BF16)
        for hd in range(ATT_KV_HEADS):
            hs = slice(hd * ATT_HEAD_DIM, (hd + 1) * ATT_HEAD_DIM)
            kh = _rope(_rms(k[:, hs], refs["k_gain"][...]), *env["rope"])
            refs["k"][rows, hs] = kh.astype(BF16)
            refs["kt"][hs, rows] = kh.astype(BF16).T
            refs["v"][rows, 2 * hd * ATT_HEAD_DIM:(2 * hd + 1) * ATT_HEAD_DIM] = v[:, hs]
            refs["v"][rows, (2 * hd + 1) * ATT_HEAD_DIM:(2 * hd + 2) * ATT_HEAD_DIM] = ones

    return [norm] + [functools.partial(q, pair) for pair in range(ATT_Q_HEADS // 2)] + [kv]


def _rope_tables(seq):
    pos = jnp.arange(seq)
    inv_freq = ROPE_THETA ** (-jnp.arange(0, ROPE_AXIS_DIM, 2, dtype=F32) / ROPE_AXIS_DIM)
    ang_row = (pos // GRID_W).astype(F32)[:, None] * inv_freq[None, :]
    ang_col = (pos % GRID_W).astype(F32)[:, None] * inv_freq[None, :]
    ang = jnp.concatenate([ang_row, ang_row, ang_col, ang_col], axis=1)
    ang = jnp.concatenate([ang, jnp.zeros((STEP_ROWS, ATT_HEAD_DIM), F32)], axis=0)
    sign = jnp.tile(jnp.concatenate([-jnp.ones((ROPE_AXIS_DIM // 2,), F32),
                                     jnp.ones((ROPE_AXIS_DIM // 2,), F32)]), 2)
    return jnp.cos(ang), jnp.sin(ang) * sign[None, :]


def _act_scratch():
    return [(f"act{tile}", pltpu.VMEM((ROW_TILE, D_FF), BF16)) for tile in range(STEP_TILES)]


def _layer_operand(key, stack, layer):
    return (key, stack, pl.BlockSpec((None,) + stack.shape[1:], lambda *_: (layer, 0, 0),
                                     pipeline_mode=pl.Buffered(1)))


def _ffn_operands(gain, wg, wu, wd, layer):
    return [("ffn_gain", gain, _const_spec((1, D_MODEL))), _layer_operand("w_gate", wg, layer),
            _layer_operand("w_up", wu, layer), _layer_operand("w_down", wd, layer)]


def _to_bf16(weights):
    flat = [w.reshape(-1, w.shape[-1]) for w in weights]
    ins, outs = [], []
    for n, f in enumerate(flat):
        rows = f.shape[0] // CAST_STEPS
        assert f.shape[0] % CAST_STEPS == 0 and rows % BF16_SUBLANES == 0
        spec = pl.BlockSpec((rows, f.shape[1]), lambda i: (i, 0))
        ins.append((f"in{n}", f, spec))
        outs.append((f"out{n}", jax.ShapeDtypeStruct(f.shape, BF16), spec))

    def body(refs):
        for n in range(len(flat)):
            refs[f"out{n}"][...] = refs[f"in{n}"][...].astype(BF16)

    res = _call(body, (CAST_STEPS,), ins, outs, [], ("parallel",), "weights_to_bf16")
    return [res[f"out{n}"].reshape(w.shape) for n, w in enumerate(weights)]


def _stage_in_body(mixer, n_real_steps, refs):
    def pipeline(tile):
        rows, env = _tile_rows(tile), {}

        def load():
            env["x"] = _meta_tile(tile, refs["h"][rows, :], refs.get("h_meta"), n_real_steps)

        def store():
            refs["h_out"][rows, :] = env["y"]

        mixer_in = _gla_in_pieces if mixer == "gla" else _attn_in_pieces
        return [load] + _swiglu_pieces(env, refs, refs[f"act{tile}"]) + [store] + mixer_in(env, refs, rows)

    _interleave([pipeline(tile) for tile in range(STEP_TILES)], STAGE_LAG)


def _token_stage_in(mixer, h, h_meta, ffn, mix_gain, mixer_ins, seq):
    n_real_steps = h.shape[0] // STEP_ROWS
    n_rows = h.shape[0] + (h_meta.shape[0] if h_meta is not None else 0)
    ins = [("h", h, _row_spec(D_MODEL, n_real_steps if h_meta is not None else None))]
    if h_meta is not None:
        assert h_meta.shape[0] == ROW_TILE
        ins.append(("h_meta", h_meta, _const_spec((ROW_TILE, D_MODEL))))
    ins += _ffn_operands(*ffn) + [("mix_gain", mix_gain, _const_spec((1, D_MODEL)))] + mixer_ins
    shp = lambda w, dt: jax.ShapeDtypeStruct((n_rows, w), dt)
    outs = [("h_out", shp(D_MODEL, F32), _row_spec(D_MODEL))]
    if mixer == "gla":
        outs += [("q", shp(GLA_QK_W, BF16), _row_spec(GLA_QK_W)), ("k", shp(GLA_QK_W, BF16), _row_spec(GLA_QK_W)),
                 ("v", shp(GLA_V_W, BF16), _row_spec(GLA_V_W)), ("r", shp(GLA_V_W, BF16), _row_spec(GLA_V_W)),
                 ("lg_fwd", shp(GLA_QK_W, F32), _row_spec(GLA_QK_W)),
                 ("lg_rev", shp(GLA_QK_W, F32), _row_spec(GLA_QK_W))]
    else:
        n_seq_steps = seq // STEP_ROWS
        real_steps = (n_rows - ROW_TILE) // STEP_ROWS
        table = pl.BlockSpec((STEP_ROWS, ATT_HEAD_DIM),
                             lambda i: (jnp.where(i < real_steps, i % n_seq_steps, n_seq_steps), 0))
        cos, sin = _rope_tables(seq)
        ins += [("cos", cos, table), ("sin", sin, table)]
        outs += [("q", shp(ATT_Q_W, BF16), _row_spec(ATT_Q_W)), ("k", shp(ATT_KV_W, BF16), _row_spec(ATT_KV_W)),
                 ("kt", jax.ShapeDtypeStruct((ATT_KV_W, n_rows), BF16),
                  pl.BlockSpec((ATT_KV_W, STEP_ROWS), lambda i: (0, i))),
                 ("v", shp(2 * ATT_KV_W, BF16), _row_spec(2 * ATT_KV_W))]
    return _call(functools.partial(_stage_in_body, mixer, n_real_steps), (pl.cdiv(n_rows, STEP_ROWS),), ins, outs,
                 _act_scratch(), ("parallel",), f"token_stage_in_{mixer}")


def _stage_out_body(mixer, n_real_steps, final_norm, refs):
    def pipeline(tile):
        rows, env = _tile_rows(tile), {}

        def mix_out():
            h = refs["h"][rows, :]
            if mixer == "gla":
                o = (_meta_tile(tile, refs["o_fwd"][rows, :], refs.get("o_fwd_meta"), n_real_steps)
                     + _meta_tile(tile, refs["o_rev"][rows, :], refs.get("o_rev_meta"), n_real_steps))
                env["x"] = _gla_out(o, refs["r"][rows, :].astype(F32), h, refs)
            else:
                o = _meta_tile(tile, refs["o"][rows, :], refs.get("o_meta"), n_real_steps)
                env["x"] = h + _dot(o, refs["w_out"][...])

        def store():
            refs["h_out"][rows, :] = _rms(env["y"], refs["final_gain"][...]) if final_norm else env["y"]

        return [mix_out] + _swiglu_pieces(env, refs, refs[f"act{tile}"]) + [store]

    _interleave([pipeline(tile) for tile in range(STEP_TILES)], STAGE_LAG)


def _token_stage_out(mixer, core, h, w_out, ffn, extra, final_gain, n_real):
    last = final_gain is not None
    n_real_steps = n_real // STEP_ROWS
    n_rows = n_real if last else h.shape[0]
    has_meta = not last
    ins = [("h", h, _row_spec(D_MODEL))]
    if mixer == "gla":
        ins += [("o_fwd", core["o_fwd"], _row_spec(GLA_V_W, n_real_steps)),
                ("o_rev", core["o_rev"], _row_spec(GLA_V_W, n_real_steps)),
                ("r", extra["r"], _row_spec(GLA_V_W)), ("head_gain", extra["head_gain"], _const_spec((1, GLA_DV)))]
        if has_meta:
            ins += [("o_fwd_meta", core["o_fwd_meta"], _const_spec((ROW_TILE, GLA_V_W))),
                    ("o_rev_meta", core["o_rev_meta"], _const_spec((ROW_TILE, GLA_V_W)))]
    else:
        ins += [("o", core["o"], _row_spec(ATT_Q_W, n_real_steps))]
        if has_meta:
            ins += [("o_meta", core["o_meta"], _const_spec((ROW_TILE, ATT_Q_W)))]
    ins += [w_out] + _ffn_operands(*ffn)
    if last:
        ins.append(("final_gain", final_gain, _const_spec((1, D_MODEL))))
    outs = [("h_out", jax.ShapeDtypeStruct((n_rows, D_MODEL), F32), _row_spec(D_MODEL))]
    return _call(functools.partial(_stage_out_body, mixer, n_real_steps, last), (pl.cdiv(n_rows, STEP_ROWS),),
                 ins, outs, _act_scratch(), ("parallel",), f"token_stage_out_{mixer}")["h_out"]


def _gla_chunks_local(items):
    t_len = items[0][1].shape[0]
    row = lax.broadcasted_iota(jnp.int32, (t_len, t_len), 0)
    col = lax.broadcasted_iota(jnp.int32, (t_len, t_len), 1)
    row3 = lax.broadcasted_iota(jnp.int32, (t_len, 3 * t_len), 0)
    col3 = lax.broadcasted_iota(jnp.int32, (t_len, 3 * t_len), 1) % t_len
    mask = {False: col <= row, True: col >= row}
    mask3 = {False: (col3 <= row3).astype(BF16), True: (col3 >= row3).astype(BF16)}
    pieces = []
    for _, _, _, _, lg in items:
        hi = lg.astype(BF16)
        rest = lg - hi.astype(F32)
        mid = rest.astype(BF16)
        pieces.append(jnp.concatenate([hi, mid, (rest - mid.astype(F32)).astype(BF16)], axis=0))
    bs = [_dot(mask3[item[0]], p) for item, p in zip(items, pieces)]
    scaled = []
    for (rev, q, k, _, _), b in zip(items, bs):
        b_tot = b[0:1, :] if rev else b[t_len - 1:t_len, :]
        scaled.append(((q * jnp.exp(b)).astype(BF16), (k * jnp.exp(-b)).astype(BF16),
                       (k * jnp.exp(b_tot - b)).astype(BF16), jnp.exp(b_tot)))
    heads = [slice(h * GLA_DK, (h + 1) * GLA_DK) for h in range(GLA_HEADS)]
    scores = [[_dot_nt(qd[:, ks], ki[:, ks]) for ks in heads] for qd, ki, _, _ in scaled]
    lhs = [[jnp.concatenate([jnp.where(mask[item[0]], s, 0.0).astype(BF16), ke[:, ks].T], axis=0)
            for s, ks in zip(ss, heads)] for item, ss, (_, _, ke, _) in zip(items, scores, scaled)]
    local = [[_dot(l, item[3][:, h * GLA_DV:(h + 1) * GLA_DV]) for h, l in enumerate(ls)]
             for item, ls in zip(items, lhs)]
    return [(qd, dec, loc) for (qd, _, _, dec), loc in zip(scaled, local)]


def _gla_chunks_carry(chunks, states):
    t_len = chunks[0][0].shape[0]
    heads = [slice(h * GLA_DK, (h + 1) * GLA_DK) for h in range(GLA_HEADS)]
    carried = [[_dot(qd[:, ks], st.astype(BF16)) for ks, st in zip(heads, sts)]
               for (qd, _, _), sts in zip(chunks, states)]
    outs = [jnp.concatenate([loc[0:t_len, :] + c for loc, c in zip(local, cs)], axis=1)
            for (_, _, local), cs in zip(chunks, carried)]
    dec_cols = [[jnp.broadcast_to(dec[:, ks], (GLA_DK, GLA_DK)).T for ks in heads] for _, dec, _ in chunks]
    new_states = [[jnp.concatenate([dc] * (GLA_DV // GLA_DK), axis=1) * st + loc[t_len:, :]
                   for dc, st, loc in zip(dcs, sts, local)]
                  for dcs, sts, (_, _, local) in zip(dec_cols, states, chunks)]
    return outs, new_states


def _gla_scan_body(refs):
    j = pl.program_id(1)
    n_chunks = GLA_BLOCK // GLA_CHUNK
    pad = GLA_CHUNK - N_META
    streams = [(slot, rev) for slot in range(GLA_BATCHES) for rev in (False, True)]
    tag = lambda slot, rev: ("_rev" if rev else "_fwd") + str(slot)

    @pl.when(j == 0)
    def _():
        for s in streams:
            refs["state" + tag(*s)][...] = jnp.zeros_like(refs["state" + tag(*s)])

    def load_states(s):
        return [refs["state" + tag(*s)][h] for h in range(GLA_HEADS)]

    def store_states(s, states):
        for h in range(GLA_HEADS):
            refs["state" + tag(*s)][h] = states[h]

    def meta_chunks(rev):
        front = lambda ref: jnp.concatenate([jnp.zeros((pad, ref.shape[1]), ref.dtype), ref[...]], axis=0)
        for slot in range(GLA_BATCHES):
            m = "_meta" + str(slot)
            (qd, dec, local), = _gla_chunks_local([(rev, front(refs["q" + m]), front(refs["k" + m]),
                                                    front(refs["v" + m]),
                                                    front(refs[("lg_rev" if rev else "lg_fwd") + m]))])
            (o,), (states,) = _gla_chunks_carry([(qd, dec, local)], [load_states((slot, rev))])
            store_states((slot, rev), states)
            refs["o_rev_meta" if rev else "o_fwd_meta"][slot] = o[pad:, :]

    pl.when(j == 0)(lambda: meta_chunks(False))

    rows = lambda c: slice(c * GLA_CHUNK, (c + 1) * GLA_CHUNK)
    states = {s: load_states(s) for s in streams}
    for ci in range(n_chunks):
        items = []
        for s in streams:
            c = n_chunks - 1 - ci if s[1] else ci
            items.append((s[1], refs["q" + tag(*s)][rows(c), :], refs["k" + tag(*s)][rows(c), :],
                          refs["v" + tag(*s)][rows(c), :], refs["lg" + tag(*s)][rows(c), :]))
        outs, new_states = _gla_chunks_carry(_gla_chunks_local(items), [states[s] for s in streams])
        for s, o, st in zip(streams, outs, new_states):
            c = n_chunks - 1 - ci if s[1] else ci
            states[s] = st
            refs["o_rev" if s[1] else "o_fwd"][s[0], rows(c), :] = o
    for s in streams:
        store_states(s, states[s])

    pl.when(j == pl.num_programs(1) - 1)(lambda: meta_chunks(True))


def _gla_scan(bsz, seq, t):
    assert bsz % GLA_BATCHES == 0
    n_blk = seq // GLA_BLOCK
    meta_blk0 = bsz * seq // N_META
    ins, scratch = [], []
    for slot in range(GLA_BATCHES):
        batch = lambda p, slot=slot: p * GLA_BATCHES + slot
        fwd = lambda w, batch=batch: pl.BlockSpec((GLA_BLOCK, w), lambda p, j: (batch(p) * n_blk + j, 0))
        rev = lambda w, batch=batch: pl.BlockSpec((GLA_BLOCK, w), lambda p, j: (batch(p) * n_blk + n_blk - 1 - j, 0))
        meta = lambda w, batch=batch: pl.BlockSpec((N_META, w), lambda p, j: (meta_blk0 + batch(p), 0))
        for sfx, spec in (("_fwd", fwd), ("_rev", rev)):
            sfx_slot = sfx + str(slot)
            ins += [("q" + sfx_slot, t["q"], spec(GLA_QK_W)), ("k" + sfx_slot, t["k"], spec(GLA_QK_W)),
                    ("v" + sfx_slot, t["v"], spec(GLA_V_W)), ("lg" + sfx_slot, t["lg" + sfx], spec(GLA_QK_W))]
            scratch.append(("state" + sfx_slot, pltpu.VMEM((GLA_HEADS, GLA_DK, GLA_DV), F32)))
        m = "_meta" + str(slot)
        ins += [("q" + m, t["q"], meta(GLA_QK_W)), ("k" + m, t["k"], meta(GLA_QK_W)), ("v" + m, t["v"], meta(GLA_V_W)),
                ("lg_fwd" + m, t["lg_fwd"], meta(GLA_QK_W)), ("lg_rev" + m, t["lg_rev"], meta(GLA_QK_W))]
    real_shape = jax.ShapeDtypeStruct((bsz, seq, GLA_V_W), F32)
    meta_shape = jax.ShapeDtypeStruct((bsz, N_META, GLA_V_W), F32)
    meta_out = pl.BlockSpec((GLA_BATCHES, N_META, GLA_V_W), lambda p, j: (p, 0, 0))
    outs = [("o_fwd", real_shape, pl.BlockSpec((GLA_BATCHES, GLA_BLOCK, GLA_V_W), lambda p, j: (p, j, 0))),
            ("o_rev", real_shape, pl.BlockSpec((GLA_BATCHES, GLA_BLOCK, GLA_V_W), lambda p, j: (p, n_blk - 1 - j, 0))),
            ("o_fwd_meta", meta_shape, meta_out), ("o_rev_meta", meta_shape, meta_out)]
    res = _call(_gla_scan_body, (bsz // GLA_BATCHES, n_blk), ins, outs, scratch, ("parallel", "arbitrary"), "gla_scan")
    return {"o_fwd": res["o_fwd"].reshape(bsz * seq, GLA_V_W), "o_rev": res["o_rev"].reshape(bsz * seq, GLA_V_W),
            "o_fwd_meta": res["o_fwd_meta"].reshape(bsz * N_META, GLA_V_W),
            "o_rev_meta": res["o_rev_meta"].reshape(bsz * N_META, GLA_V_W)}


def _attn_scores_block(q, kt_ref, kb, s_ref, mx):
    cols = slice(kb * ATT_KV_BLOCK, (kb + 1) * ATT_KV_BLOCK)
    s = _dot(q, kt_ref[:, cols])
    s_ref[kb] = s
    tiles = [s[:, c * LANES:(c + 1) * LANES] for c in range(ATT_KV_BLOCK // LANES)]
    while len(tiles) > 1:
        tiles = [jnp.maximum(tiles[i], tiles[i + 1]) for i in range(0, len(tiles), 2)]
    return tiles[0] if mx is None else jnp.maximum(mx, tiles[0])


def _attn_scores_finish(q, km, mx, sm_ref, m_ref):
    sm = _dot_nt(q, km)
    sm_ref[...] = sm
    m = jnp.maximum(jnp.max(mx, axis=1, keepdims=True), jnp.max(sm, axis=1, keepdims=True))
    m_ref[...] = jnp.broadcast_to(m, m_ref.shape)


def _attn_weighted_start(sm_ref, m_ref, vm):
    m = m_ref[...]
    return m, _dot(jnp.exp2(sm_ref[...] - m[:, 0:N_META]).astype(BF16), vm)


def _attn_weighted_block(s_ref, kb, m, v_ref, acc):
    rows = slice(kb * ATT_KV_BLOCK, (kb + 1) * ATT_KV_BLOCK)
    p = [jnp.exp2(s_ref[kb, :, c * LANES:(c + 1) * LANES] - m) for c in range(ATT_KV_BLOCK // LANES)]
    return acc + _dot(jnp.concatenate(p, axis=1).astype(BF16), v_ref[rows, :])


def _attn_core_body(n_kv_blocks, n_q, n_tiles, meta_queries, refs):
    t = pl.program_id(0)
    kt_ref, v_ref = refs["kt"], refs["v"]
    km = refs["k_meta"][...]
    vm = refs["v_meta"][...]
    s_ref, sm_ref, m_ref = refs["s"], refs["sm"], refs["m"]

    def stack(ref):
        return jnp.concatenate([ref[:, g * ATT_HEAD_DIM:(g + 1) * ATT_HEAD_DIM] for g in range(ATT_GROUP)], axis=0)

    def unstack(acc, ref):
        o = acc[:, 0:ATT_HEAD_DIM] / acc[:, ATT_HEAD_DIM:]
        rows = ref.shape[0]
        for g in range(ATT_GROUP):
            ref[:, g * ATT_HEAD_DIM:(g + 1) * ATT_HEAD_DIM] = o[g * rows:(g + 1) * rows, :].astype(BF16)

    @pl.when(t == 0)
    def _():
        for ref in (s_ref, sm_ref, m_ref):
            ref[...] = jnp.zeros_like(ref)

    q = stack(refs["q"])
    m, acc = _attn_weighted_start(sm_ref, m_ref, vm)
    mx = None
    for kb in range(n_kv_blocks):
        acc = _attn_weighted_block(s_ref, kb, m, v_ref, acc)
        mx = _attn_scores_block(q, kt_ref, kb, s_ref, mx)
    _attn_scores_finish(q, km, mx, sm_ref, m_ref)
    unstack(acc, refs["o"])

    if meta_queries:
        @pl.when(jnp.minimum(t, n_tiles - 1) % n_q == n_q - 1)
        def _():
            sq_ref, smq_ref, mq_ref = refs["sq"], refs["smq"], refs["mq"]
            q = stack(refs["q_meta"])
            mx = None
            for kb in range(n_kv_blocks):
                mx = _attn_scores_block(q, kt_ref, kb, sq_ref, mx)
            _attn_scores_finish(q, km, mx, smq_ref, mq_ref)
            m, acc = _attn_weighted_start(smq_ref, mq_ref, vm)
            for kb in range(n_kv_blocks):
                acc = _attn_weighted_block(sq_ref, kb, m, v_ref, acc)
            unstack(acc, refs["o_meta"])


def _attn_core(bsz, seq, t, meta_queries):
    n_q = seq // ATT_Q_TILE
    assert n_q >= 2
    n_tiles = bsz * ATT_KV_HEADS * n_q
    meta_blk0 = bsz * seq // N_META
    n_kv_blocks = seq // ATT_KV_BLOCK
    group_w = ATT_GROUP * ATT_HEAD_DIM

    def coords(tile):
        return tile // (ATT_KV_HEADS * n_q), (tile // n_q) % ATT_KV_HEADS, tile % n_q

    def spec(shape, which, fn):
        tile = (lambda s: jnp.minimum(s, n_tiles - 1)) if which == "cur" else (lambda s: jnp.maximum(s - 1, 0))
        return pl.BlockSpec(shape, lambda s: fn(*coords(tile(s))))

    def stash(sfx, rows):
        return [("s" + sfx, pltpu.VMEM((n_kv_blocks, rows, ATT_KV_BLOCK), F32)),
                ("sm" + sfx, pltpu.VMEM((rows, N_META), F32)), ("m" + sfx, pltpu.VMEM((rows, LANES), F32))]

    ins = [("q", t["q"], spec((ATT_Q_TILE, group_w), "cur", lambda b, kh, i: (b * n_q + i, kh))),
           ("kt", t["kt"], spec((ATT_HEAD_DIM, seq), "cur", lambda b, kh, i: (kh, b))),
           ("k_meta", t["k"], spec((N_META, ATT_HEAD_DIM), "cur", lambda b, kh, i: (meta_blk0 + b, kh))),
           ("v", t["v"], spec((seq, 2 * ATT_HEAD_DIM), "prev", lambda b, kh, i: (b, kh))),
           ("v_meta", t["v"], spec((N_META, 2 * ATT_HEAD_DIM), "prev", lambda b, kh, i: (meta_blk0 + b, kh)))]
    outs = [("o", jax.ShapeDtypeStruct((bsz * seq, ATT_Q_W), BF16),
             spec((ATT_Q_TILE, group_w), "prev", lambda b, kh, i: (b * n_q + i, kh)))]
    scratch = stash("", ATT_GROUP * ATT_Q_TILE)
    if meta_queries:
        ins.append(("q_meta", t["q"], spec((N_META, group_w), "cur", lambda b, kh, i: (meta_blk0 + b, kh))))
        outs.append(("o_meta", jax.ShapeDtypeStruct((bsz * N_META, ATT_Q_W), BF16),
                     spec((N_META, group_w), "cur", lambda b, kh, i: (b, kh))))
        scratch += stash("q", ATT_GROUP * N_META)
    return _call(functools.partial(_attn_core_body, n_kv_blocks, n_q, n_tiles, meta_queries), (n_tiles + 1,),
                 ins, outs, scratch, ("arbitrary",), "attn_core")


def kernel(x, meta_tokens, norm_ffn1, ffn1_w_gate, ffn1_w_up, ffn1_w_down, norm_mix, gla_w_in, gla_gate_w1, gla_gate_w2, gla_gate_b, gla_head_norm, gla_w_out, attn_w_in, attn_q_norm, attn_k_norm, attn_w_out, norm_ffn2, ffn2_w_gate, ffn2_w_up, ffn2_w_down, norm_final):
    bsz, seq, d = x.shape
    depth = norm_ffn1.shape[0]
    assert d == D_MODEL and bsz * N_META == ROW_TILE and seq % STEP_ROWS == 0 and seq % ATT_KV_BLOCK == 0
    n_real = bsz * seq
    row = lambda g: g.reshape(1, -1).astype(F32)
    small = lambda key, arr: (key, arr, _const_spec(arr.shape))

    h = x.reshape(n_real, d)
    h_meta = jnp.broadcast_to(meta_tokens.astype(x.dtype)[None], (bsz, N_META, d)).reshape(bsz * N_META, d)
    (ffn1_wg, ffn1_wu, ffn1_wd, ffn2_wg, ffn2_wu, ffn2_wd, gla_in_w, gla_out_w, attn_in_w, attn_out_w) = _to_bf16(
        [ffn1_w_gate, ffn1_w_up, ffn1_w_down, ffn2_w_gate, ffn2_w_up, ffn2_w_down,
         gla_w_in, gla_w_out, attn_w_in, attn_w_out])

    for i in range(depth):
        j = i // N_MIXERS
        last = i == depth - 1
        ffn1 = (row(norm_ffn1[i]), ffn1_wg, ffn1_wu, ffn1_wd, i)
        ffn2 = (row(norm_ffn2[i]), ffn2_wg, ffn2_wu, ffn2_wd, i)
        final_gain = row(norm_final) if last else None
        if i % N_MIXERS == 0:
            zero = jnp.zeros_like(gla_gate_w2[j, 0])
            gate_w2 = jnp.concatenate([jnp.concatenate([gla_gate_w2[j, 0], zero], axis=1),
                                       jnp.concatenate([zero, gla_gate_w2[j, 1]], axis=1)], axis=0)
            gate_w1 = jnp.concatenate([gla_gate_w1[j, 0], gla_gate_w1[j, 1]], axis=1)
            mixer_ins = [_layer_operand("w_in", gla_in_w, j), small("gate_w1", gate_w1.astype(BF16)),
                         small("gate_w2", gate_w2.astype(BF16)),
                         small("gate_b", gla_gate_b[j].reshape(1, -1).astype(F32))]
            t = _token_stage_in("gla", h, h_meta, ffn1, row(norm_mix[i]), mixer_ins, seq)
            core = _gla_scan(bsz, seq, t)
            h = _token_stage_out("gla", core, t["h_out"], _layer_operand("w_out", gla_out_w, j), ffn2,
                                 {"r": t["r"], "head_gain": row(gla_head_norm[j])}, final_gain, n_real)
        else:
            mixer_ins = [_layer_operand("w_in", attn_in_w, j), small("q_gain", row(attn_q_norm[j])),
                         small("k_gain", row(attn_k_norm[j]))]
            t = _token_stage_in("attn", h, h_meta, ffn1, row(norm_mix[i]), mixer_ins, seq)
            core = _attn_core(bsz, seq, t, meta_queries=not last)
            h = _token_stage_out("attn", core, t["h_out"], _layer_operand("w_out", attn_out_w, j), ffn2, {},
                                 final_gain, n_real)
        h_meta = None
    return h.reshape(bsz, seq, d)
```

```python
import functools

import jax
import jax.numpy as jnp
from jax import lax
from jax.experimental import pallas as pl
from jax.experimental.pallas import tpu as pltpu

F32 = jnp.float32
BF16 = jnp.bfloat16

D_MODEL = 1024
N_META = 16
GRID_W = 64
D_FF = 2816
NORM_EPS = 1e-6
MACARON_WEIGHT = 0.5
N_MIXERS = 2

GLA_HEADS = 4
GLA_DK = D_MODEL // 2 // GLA_HEADS
GLA_DV = D_MODEL // GLA_HEADS
GLA_QK_W = GLA_HEADS * GLA_DK
GLA_V_W = GLA_HEADS * GLA_DV
GLA_GATE_RANK = 16
GLA_GATE_TAU = 16.0
GLA_CHUNK = 64

ATT_Q_HEADS = 8
ATT_KV_HEADS = 2
ATT_HEAD_DIM = D_MODEL // ATT_Q_HEADS
ATT_GROUP = ATT_Q_HEADS // ATT_KV_HEADS
ATT_Q_W = ATT_Q_HEADS * ATT_HEAD_DIM
ATT_KV_W = ATT_KV_HEADS * ATT_HEAD_DIM
ROPE_THETA = 10000.0
ROPE_AXIS_DIM = ATT_HEAD_DIM // 2
LOG2_E = 1.4426950408889634

LANES = 128
ROW_TILE = 256
STEP_TILES = 2
STEP_ROWS = STEP_TILES * ROW_TILE
STAGE_LAG = 5
FF_CHUNK = 256
GLA_BLOCK = 256
GLA_BATCHES = 2
ATT_Q_TILE = 512
ATT_KV_BLOCK = 512
CAST_STEPS = 32
BF16_SUBLANES = 16
VMEM_LIMIT = 56 * 1024 * 1024


def _params(semantics):
    return pltpu.CompilerParams(dimension_semantics=semantics, vmem_limit_bytes=VMEM_LIMIT)


def _const_spec(shape):
    nd = len(shape)
    return pl.BlockSpec(shape, lambda *_: (0,) * nd, pipeline_mode=pl.Buffered(1))


def _rms(x, gain):
    ms = jnp.mean(x * x, axis=-1, keepdims=True)
    return x * lax.rsqrt(ms + NORM_EPS) * gain


def _dot(a, b):
    return jnp.dot(a, b, preferred_element_type=F32)


def _dot_nt(a, b):
    return lax.dot_general(a, b, (((1,), (1,)), ((), ())), preferred_element_type=F32)


def _dot_tn(a, b):
    return lax.dot_general(a, b, (((0,), (0,)), ((), ())), preferred_element_type=F32)


def _call(body, grid, ins, outs, scratch, semantics, name):
    keys = [k for k, _, _ in ins] + [k for k, _, _ in outs] + [k for k, _ in scratch]

    def kernel_body(*refs):
        body(dict(zip(keys, refs, strict=True)))

    results = pl.pallas_call(
        kernel_body,
        grid=grid,
        in_specs=[s for _, _, s in ins],
        out_specs=[s for _, _, s in outs],
        out_shape=[a for _, a, _ in outs],
        scratch_shapes=[s for _, s in scratch],
        compiler_params=_params(semantics),
        name=name,
    )(*[a for _, a, _ in ins])
    return dict(zip([k for k, _, _ in outs], results, strict=True))


def _row_spec(width, clamp_steps=None):
    if clamp_steps is None:
        return pl.BlockSpec((STEP_ROWS, width), lambda i: (i, 0))
    return pl.BlockSpec((STEP_ROWS, width), lambda i: (jnp.minimum(i, clamp_steps - 1), 0))


def _tile_rows(tile):
    return slice(tile * ROW_TILE, (tile + 1) * ROW_TILE)


def _meta_tile(tile, real, meta_ref, n_real_steps):
    if tile > 0 or meta_ref is None:
        return real
    return jnp.where(pl.program_id(0) == n_real_steps, meta_ref[...], real)


def _swiglu_pieces(env, refs, a_ref):
    def norm():
        env["xn"] = _rms(env["x"], refs["ffn_gain"][...]).astype(BF16)

    def chunk(c):
        cols = slice(c * FF_CHUNK, (c + 1) * FF_CHUNK)
        g = _dot(env["xn"], refs["w_gate"][:, cols])
        u = _dot(env["xn"], refs["w_up"][:, cols])
        a_ref[:, cols] = (g * jax.nn.sigmoid(g) * u).astype(BF16)

    def down():
        env["y"] = env["x"] + MACARON_WEIGHT * _dot(a_ref[...], refs["w_down"][...])

    return [norm] + [functools.partial(chunk, c) for c in range(D_FF // FF_CHUNK)] + [down]


def _interleave(pipelines, lag):
    n = max(len(p) for p in pipelines)
    for step in range(n + lag * (len(pipelines) - 1)):
        for i, p in enumerate(pipelines):
            k = step - i * lag
            if 0 <= k < len(p):
                p[k]()


def _gla_in_pieces(env, refs, rows):
    win_ref = refs["w_in"]

    def norm():
        env["yn"] = _rms(env["y"], refs["mix_gain"][...]).astype(BF16)

    def q():
        refs["q"][rows, :] = (_dot(env["yn"], win_ref[:, 0:GLA_QK_W]) * (GLA_DK ** -0.5)).astype(BF16)

    def k():
        refs["k"][rows, :] = _dot(env["yn"], win_ref[:, GLA_QK_W:2 * GLA_QK_W]).astype(BF16)

    def v(half):
        cols = slice(half * GLA_QK_W, (half + 1) * GLA_QK_W)
        refs["v"][rows, cols] = _dot(env["yn"], win_ref[:, 2 * GLA_QK_W + half * GLA_QK_W:
                                                        2 * GLA_QK_W + (half + 1) * GLA_QK_W]).astype(BF16)

    def r(half):
        cols = slice(half * GLA_QK_W, (half + 1) * GLA_QK_W)
        base = 2 * GLA_QK_W + GLA_V_W
        refs["r"][rows, cols] = _dot(env["yn"], win_ref[:, base + half * GLA_QK_W:
                                                        base + (half + 1) * GLA_QK_W]).astype(BF16)

    def gates():
        t = _dot(env["yn"], refs["gate_w1"][...]).astype(BF16)
        z = _dot(t, refs["gate_w2"][...]) + refs["gate_b"][...]
        logg = (jnp.minimum(z, 0.0) - jnp.log(1.0 + jnp.exp(-jnp.abs(z)))) * (1.0 / GLA_GATE_TAU)
        refs["lg_fwd"][rows, :] = logg[:, 0:GLA_QK_W]
        refs["lg_rev"][rows, :] = logg[:, GLA_QK_W:]

    return [norm, gates, q, k, functools.partial(v, 0), functools.partial(v, 1),
            functools.partial(r, 0), functools.partial(r, 1)]


def _gla_out(o, r, h, refs):
    x = h
    for hd in range(GLA_HEADS):
        vs = slice(hd * GLA_DV, (hd + 1) * GLA_DV)
        r_h = r[:, vs].astype(F32)
        y = (_rms(o[:, vs], refs["head_gain"][...]) * (r_h * jax.nn.sigmoid(r_h))).astype(BF16)
        x = x + _dot(y, refs["w_out"][vs, :])
    return x


def _rope(x, cos, sin_signed, first_half):
    partner = jnp.where(first_half, pltpu.roll(x, LANES - ROPE_AXIS_DIM // 2, 1),
                        pltpu.roll(x, ROPE_AXIS_DIM // 2, 1))
    return x * cos + partner * sin_signed


def _attn_in_pieces(env, refs, rows):
    win_ref = refs["w_in"]

    def norm():
        env["yn"] = _rms(env["y"], refs["mix_gain"][...]).astype(BF16)
        cos = refs["cos"][rows, :]
        lane = lax.broadcasted_iota(jnp.int32, cos.shape, 1)
        env["rope"] = (cos, refs["sin"][rows, :], (lane % ROPE_AXIS_DIM) < (ROPE_AXIS_DIM // 2))

    def q(pair):
        cols = slice(2 * pair * ATT_HEAD_DIM, (2 * pair + 2) * ATT_HEAD_DIM)
        qq = _dot(env["yn"], win_ref[:, cols])
        for i in range(2):
            hs = slice(i * ATT_HEAD_DIM, (i + 1) * ATT_HEAD_DIM)
            qh = _rope(_rms(qq[:, hs], refs["q_gain"][...]), *env["rope"]) * (ATT_HEAD_DIM ** -0.5 * LOG2_E)
            refs["q"][rows, (2 * pair + i) * ATT_HEAD_DIM:(2 * pair + i + 1) * ATT_HEAD_DIM] = qh.astype(BF16)

    def kv():
        k = _dot(env["yn"], win_ref[:, ATT_Q_W:ATT_Q_W + ATT_KV_W])
        v = _dot(env["yn"], win_ref[:, ATT_Q_W + ATT_KV_W:]).astype(BF16)
        ones = jnp.ones((v.shape[0], ATT_HEAD_DIM), BF16)
        for hd in range(ATT_KV_HEADS):
            hs = slice(hd * ATT_HEAD_DIM, (hd + 1) * ATT_HEAD_DIM)
            kh = _rope(_rms(k[:, hs], refs["k_gain"][...]), *env["rope"])
            refs["k"][rows, hs] = kh.astype(BF16)
            refs["kt"][hs, rows] = kh.astype(BF16).T
            refs["v"][rows, 2 * hd * ATT_HEAD_DIM:(2 * hd + 1) * ATT_HEAD_DIM] = v[:, hs]
            refs["v"][rows, (2 * hd + 1) * ATT_HEAD_DIM:(2 * hd + 2) * ATT_HEAD_DIM] = ones

    return [norm] + [functools.partial(q, pair) for pair in range(ATT_Q_HEADS // 2)] + [kv]


def _rope_tables(seq):
    pos = jnp.arange(seq)
    inv_freq = ROPE_THETA ** (-jnp.arange(0, ROPE_AXIS_DIM, 2, dtype=F32) / ROPE_AXIS_DIM)
    ang_row = (pos // GRID_W).astype(F32)[:, None] * inv_freq[None, :]
    ang_col = (pos % GRID_W).astype(F32)[:, None] * inv_freq[None, :]
    ang = jnp.concatenate([ang_row, ang_row, ang_col, ang_col], axis=1)
    ang = jnp.concatenate([ang, jnp.zeros((STEP_ROWS, ATT_HEAD_DIM), F32)], axis=0)
    sign = jnp.tile(jnp.concatenate([-jnp.ones((ROPE_AXIS_DIM // 2,), F32),
                                     jnp.ones((ROPE_AXIS_DIM // 2,), F32)]), 2)
    return jnp.cos(ang), jnp.sin(ang) * sign[None, :]


def _act_scratch():
    return [(f"act{tile}", pltpu.VMEM((ROW_TILE, D_FF), BF16)) for tile in range(STEP_TILES)]


def _layer_operand(key, stack, layer):
    return (key, stack, pl.BlockSpec((None,) + stack.shape[1:], lambda *_: (layer, 0, 0),
                                     pipeline_mode=pl.Buffered(1)))


def _ffn_operands(gain, wg, wu, wd, layer):
    return [("ffn_gain", gain, _const_spec((1, D_MODEL))), _layer_operand("w_gate", wg, layer),
            _layer_operand("w_up", wu, layer), _layer_operand("w_down", wd, layer)]


def _to_bf16(weights):
    flat = [w.reshape(-1, w.shape[-1]) for w in weights]
    ins, outs = [], []
    for n, f in enumerate(flat):
        rows = f.shape[0] // CAST_STEPS
        assert f.shape[0] % CAST_STEPS == 0 and rows % BF16_SUBLANES == 0
        spec = pl.BlockSpec((rows, f.shape[1]), lambda i: (i, 0))
        ins.append((f"in{n}", f, spec))
        outs.append((f"out{n}", jax.ShapeDtypeStruct(f.shape, BF16), spec))

    def body(refs):
        for n in range(len(flat)):
            refs[f"out{n}"][...] = refs[f"in{n}"][...].astype(BF16)

    res = _call(body, (CAST_STEPS,), ins, outs, [], ("parallel",), "weights_to_bf16")
    return [res[f"out{n}"].reshape(w.shape) for n, w in enumerate(weights)]


def _stage_in_body(mixer, n_real_steps, refs):
    def pipeline(tile):
        rows, env = _tile_rows(tile), {}

        def load():
            env["x"] = _meta_tile(tile, refs["h"][rows, :], refs.get("h_meta"), n_real_steps)

        def store():
            refs["h_out"][rows, :] = env["y"]

        mixer_in = _gla_in_pieces if mixer == "gla" else _attn_in_pieces
        return [load] + _swiglu_pieces(env, refs, refs[f"act{tile}"]) + [store] + mixer_in(env, refs, rows)

    _interleave([pipeline(tile) for tile in range(STEP_TILES)], STAGE_LAG)


def _token_stage_in(mixer, h, h_meta, ffn, mix_gain, mixer_ins, seq):
    n_real_steps = h.shape[0] // STEP_ROWS
    n_rows = h.shape[0] + (h_meta.shape[0] if h_meta is not None else 0)
    ins = [("h", h, _row_spec(D_MODEL, n_real_steps if h_meta is not None else None))]
    if h_meta is not None:
        assert h_meta.shape[0] == ROW_TILE
        ins.append(("h_meta", h_meta, _const_spec((ROW_TILE, D_MODEL))))
    ins += _ffn_operands(*ffn) + [("mix_gain", mix_gain, _const_spec((1, D_MODEL)))] + mixer_ins
    shp = lambda w, dt: jax.ShapeDtypeStruct((n_rows, w), dt)
    outs = [("h_out", shp(D_MODEL, F32), _row_spec(D_MODEL))]
    if mixer == "gla":
        outs += [("q", shp(GLA_QK_W, BF16), _row_spec(GLA_QK_W)), ("k", shp(GLA_QK_W, BF16), _row_spec(GLA_QK_W)),
                 ("v", shp(GLA_V_W, BF16), _row_spec(GLA_V_W)), ("r", shp(GLA_V_W, BF16), _row_spec(GLA_V_W)),
                 ("lg_fwd", shp(GLA_QK_W, F32), _row_spec(GLA_QK_W)),
                 ("lg_rev", shp(GLA_QK_W, F32), _row_spec(GLA_QK_W))]
    else:
        n_seq_steps = seq // STEP_ROWS
        real_steps = (n_rows - ROW_TILE) // STEP_ROWS
        table = pl.BlockSpec((STEP_ROWS, ATT_HEAD_DIM),
                             lambda i: (jnp.where(i < real_steps, i % n_seq_steps, n_seq_steps), 0))
        cos, sin = _rope_tables(seq)
        ins += [("cos", cos, table), ("sin", sin, table)]
        outs += [("q", shp(ATT_Q_W, BF16), _row_spec(ATT_Q_W)), ("k", shp(ATT_KV_W, BF16), _row_spec(ATT_KV_W)),
                 ("kt", jax.ShapeDtypeStruct((ATT_KV_W, n_rows), BF16),
                  pl.BlockSpec((ATT_KV_W, STEP_ROWS), lambda i: (0, i))),
                 ("v", shp(2 * ATT_KV_W, BF16), _row_spec(2 * ATT_KV_W))]
    return _call(functools.partial(_stage_in_body, mixer, n_real_steps), (pl.cdiv(n_rows, STEP_ROWS),), ins, outs,
                 _act_scratch(), ("parallel",), f"token_stage_in_{mixer}")


def _stage_out_body(mixer, n_real_steps, final_norm, refs):
    def pipeline(tile):
        rows, env = _tile_rows(tile), {}

        def mix_out():
            h = refs["h"][rows, :]
            if mixer == "gla":
                o = (_meta_tile(tile, refs["o_fwd"][rows, :], refs.get("o_fwd_meta"), n_real_steps)
                     + _meta_tile(tile, refs["o_rev"][rows, :], refs.get("o_rev_meta"), n_real_steps))
                env["x"] = _gla_out(o, refs["r"][rows, :], h, refs)
            else:
                o = _meta_tile(tile, refs["o"][rows, :], refs.get("o_meta"), n_real_steps)
                env["x"] = h + _dot(o, refs["w_out"][...])

        def store():
            refs["h_out"][rows, :] = _rms(env["y"], refs["final_gain"][...]) if final_norm else env["y"]

        return [mix_out] + _swiglu_pieces(env, refs, refs[f"act{tile}"]) + [store]

    _interleave([pipeline(tile) for tile in range(STEP_TILES)], STAGE_LAG)


def _token_stage_out(mixer, core, h, w_out, ffn, extra, final_gain, n_real):
    last = final_gain is not None
    n_real_steps = n_real // STEP_ROWS
    n_rows = n_real if last else h.shape[0]
    has_meta = not last
    ins = [("h", h, _row_spec(D_MODEL))]
    if mixer == "gla":
        ins += [("o_fwd", core["o_fwd"], _row_spec(GLA_V_W, n_real_steps)),
                ("o_rev", core["o_rev"], _row_spec(GLA_V_W, n_real_steps)),
                ("r", extra["r"], _row_spec(GLA_V_W)), ("head_gain", extra["head_gain"], _const_spec((1, GLA_DV)))]
        if has_meta:
            ins += [("o_fwd_meta", core["o_fwd_meta"], _const_spec((ROW_TILE, GLA_V_W))),
                    ("o_rev_meta", core["o_rev_meta"], _const_spec((ROW_TILE, GLA_V_W)))]
    else:
        ins += [("o", core["o"], _row_spec(ATT_Q_W, n_real_steps))]
        if has_meta:
            ins += [("o_meta", core["o_meta"], _const_spec((ROW_TILE, ATT_Q_W)))]
    ins += [w_out] + _ffn_operands(*ffn)
    if last:
        ins.append(("final_gain", final_gain, _const_spec((1, D_MODEL))))
    outs = [("h_out", jax.ShapeDtypeStruct((n_rows, D_MODEL), F32), _row_spec(D_MODEL))]
    return _call(functools.partial(_stage_out_body, mixer, n_real_steps, last), (pl.cdiv(n_rows, STEP_ROWS),),
                 ins, outs, _act_scratch(), ("parallel",), f"token_stage_out_{mixer}")["h_out"]


def _gla_chunks_local(items):
    t_len = items[0][1].shape[0]
    row = lax.broadcasted_iota(jnp.int32, (t_len, t_len), 0)
    col = lax.broadcasted_iota(jnp.int32, (t_len, t_len), 1)
    row3 = lax.broadcasted_iota(jnp.int32, (t_len, 3 * t_len), 0)
    col3 = lax.broadcasted_iota(jnp.int32, (t_len, 3 * t_len), 1) % t_len
    mask = {False: col <= row, True: col >= row}
    mask3 = {False: (col3 <= row3).astype(BF16), True: (col3 >= row3).astype(BF16)}
    pieces = []
    for _, _, _, _, lg in items:
        hi = lg.astype(BF16)
        rest = lg - hi.astype(F32)
        mid = rest.astype(BF16)
        pieces.append(jnp.concatenate([hi, mid, (rest - mid.astype(F32)).astype(BF16)], axis=0))
    bs = [_dot(mask3[item[0]], p) for item, p in zip(items, pieces)]
    scaled = []
    for (rev, q, k, _, _), b in zip(items, bs):
        b_tot = b[0:1, :] if rev else b[t_len - 1:t_len, :]
        scaled.append(((q * jnp.exp(b)).astype(BF16), (k * jnp.exp(-b)).astype(BF16),
                       (k * jnp.exp(b_tot - b)).astype(BF16), jnp.exp(b_tot)))
    heads = [slice(h * GLA_DK, (h + 1) * GLA_DK) for h in range(GLA_HEADS)]
    scores = [[_dot_nt(qd[:, ks], ki[:, ks]) for ks in heads] for qd, ki, _, _ in scaled]
    lhs = [[jnp.concatenate([jnp.where(mask[item[0]], s, 0.0).astype(BF16), ke[:, ks].T], axis=0)
            for s, ks in zip(ss, heads)] for item, ss, (_, _, ke, _) in zip(items, scores, scaled)]
    local = [[_dot(l, item[3][:, h * GLA_DV:(h + 1) * GLA_DV]) for h, l in enumerate(ls)]
             for item, ls in zip(items, lhs)]
    return [(qd, dec, loc) for (qd, _, _, dec), loc in zip(scaled, local)]


def _gla_chunks_carry(chunks, states):
    t_len = chunks[0][0].shape[0]
    heads = [slice(h * GLA_DK, (h + 1) * GLA_DK) for h in range(GLA_HEADS)]
    carried = [[_dot(qd[:, ks], st.astype(BF16)) for ks, st in zip(heads, sts)]
               for (qd, _, _), sts in zip(chunks, states)]
    outs = [jnp.concatenate([loc[0:t_len, :] + c for loc, c in zip(local, cs)], axis=1)
            for (_, _, local), cs in zip(chunks, carried)]
    dec_cols = [[jnp.broadcast_to(dec[:, ks], (GLA_DK, GLA_DK)).T for ks in heads] for _, dec, _ in chunks]
    new_states = [[jnp.concatenate([dc] * (GLA_DV // GLA_DK), axis=1) * st + loc[t_len:, :]
                   for dc, st, loc in zip(dcs, sts, local)]
                  for dcs, sts, (_, _, local) in zip(dec_cols, states, chunks)]
    return outs, new_states


def _gla_scan_body(refs):
    j = pl.program_id(1)
    n_chunks = GLA_BLOCK // GLA_CHUNK
    pad = GLA_CHUNK - N_META
    streams = [(slot, rev) for slot in range(GLA_BATCHES) for rev in (False, True)]
    tag = lambda slot, rev: ("_rev" if rev else "_fwd") + str(slot)

    @pl.when(j == 0)
    def _():
        for s in streams:
            refs["state" + tag(*s)][...] = jnp.zeros_like(refs["state" + tag(*s)])

    def load_states(s):
        return [refs["state" + tag(*s)][h] for h in range(GLA_HEADS)]

    def store_states(s, states):
        for h in range(GLA_HEADS):
            refs["state" + tag(*s)][h] = states[h]

    def meta_chunks(rev):
        front = lambda ref: jnp.concatenate([jnp.zeros((pad, ref.shape[1]), ref.dtype), ref[...]], axis=0)
        for slot in range(GLA_BATCHES):
            m = "_meta" + str(slot)
            (qd, dec, local), = _gla_chunks_local([(rev, front(refs["q" + m]), front(refs["k" + m]),
                                                    front(refs["v" + m]),
                                                    front(refs[("lg_rev" if rev else "lg_fwd") + m]))])
            (o,), (states,) = _gla_chunks_carry([(qd, dec, local)], [load_states((slot, rev))])
            store_states((slot, rev), states)
            refs["o_rev_meta" if rev else "o_fwd_meta"][slot] = o[pad:, :]

    pl.when(j == 0)(lambda: meta_chunks(False))

    rows = lambda c: slice(c * GLA_CHUNK, (c + 1) * GLA_CHUNK)
    states = {s: load_states(s) for s in streams}
    for ci in range(n_chunks):
        items = []
        for s in streams:
            c = n_chunks - 1 - ci if s[1] else ci
            items.append((s[1], refs["q" + tag(*s)][rows(c), :], refs["k" + tag(*s)][rows(c), :],
                          refs["v" + tag(*s)][rows(c), :], refs["lg" + tag(*s)][rows(c), :]))
        outs, new_states = _gla_chunks_carry(_gla_chunks_local(items), [states[s] for s in streams])
        for s, o, st in zip(streams, outs, new_states):
            c = n_chunks - 1 - ci if s[1] else ci
            states[s] = st
            refs["o_rev" if s[1] else "o_fwd"][s[0], rows(c), :] = o
    for s in streams:
        store_states(s, states[s])

    pl.when(j == pl.num_programs(1) - 1)(lambda: meta_chunks(True))


def _gla_scan(bsz, seq, t):
    assert bsz % GLA_BATCHES == 0
    n_blk = seq // GLA_BLOCK
    meta_blk0 = bsz * seq // N_META
    ins, scratch = [], []
    for slot in range(GLA_BATCHES):
        batch = lambda p, slot=slot: p * GLA_BATCHES + slot
        fwd = lambda w, batch=batch: pl.BlockSpec((GLA_BLOCK, w), lambda p, j: (batch(p) * n_blk + j, 0))
        rev = lambda w, batch=batch: pl.BlockSpec((GLA_BLOCK, w), lambda p, j: (batch(p) * n_blk + n_blk - 1 - j, 0))
        meta = lambda w, batch=batch: pl.BlockSpec((N_META, w), lambda p, j: (meta_blk0 + batch(p), 0))
        for sfx, spec in (("_fwd", fwd), ("_rev", rev)):
            sfx_slot = sfx + str(slot)
            ins += [("q" + sfx_slot, t["q"], spec(GLA_QK_W)), ("k" + sfx_slot, t["k"], spec(GLA_QK_W)),
                    ("v" + sfx_slot, t["v"], spec(GLA_V_W)), ("lg" + sfx_slot, t["lg" + sfx], spec(GLA_QK_W))]
            scratch.append(("state" + sfx_slot, pltpu.VMEM((GLA_HEADS, GLA_DK, GLA_DV), F32)))
        m = "_meta" + str(slot)
        ins += [("q" + m, t["q"], meta(GLA_QK_W)), ("k" + m, t["k"], meta(GLA_QK_W)), ("v" + m, t["v"], meta(GLA_V_W)),
                ("lg_fwd" + m, t["lg_fwd"], meta(GLA_QK_W)), ("lg_rev" + m, t["lg_rev"], meta(GLA_QK_W))]
    real_shape = jax.ShapeDtypeStruct((bsz, seq, GLA_V_W), F32)
    meta_shape = jax.ShapeDtypeStruct((bsz, N_META, GLA_V_W), F32)
    meta_out = pl.BlockSpec((GLA_BATCHES, N_META, GLA_V_W), lambda p, j: (p, 0, 0))
    outs = [("o_fwd", real_shape, pl.BlockSpec((GLA_BATCHES, GLA_BLOCK, GLA_V_W), lambda p, j: (p, j, 0))),
            ("o_rev", real_shape, pl.BlockSpec((GLA_BATCHES, GLA_BLOCK, GLA_V_W), lambda p, j: (p, n_blk - 1 - j, 0))),
            ("o_fwd_meta", meta_shape, meta_out), ("o_rev_meta", meta_shape, meta_out)]
    res = _call(_gla_scan_body, (bsz // GLA_BATCHES, n_blk), ins, outs, scratch, ("parallel", "arbitrary"), "gla_scan")
    return {"o_fwd": res["o_fwd"].reshape(bsz * seq, GLA_V_W), "o_rev": res["o_rev"].reshape(bsz * seq, GLA_V_W),
            "o_fwd_meta": res["o_fwd_meta"].reshape(bsz * N_META, GLA_V_W),
            "o_rev_meta": res["o_rev_meta"].reshape(bsz * N_META, GLA_V_W)}


def _attn_scores_block(q, kt_ref, kb, s_ref, mx):
    cols = slice(kb * ATT_KV_BLOCK, (kb + 1) * ATT_KV_BLOCK)
    s = _dot(q, kt_ref[:, cols])
    s_ref[kb] = s
    tiles = [s[:, c * LANES:(c + 1) * LANES] for c in range(ATT_KV_BLOCK // LANES)]
    while len(tiles) > 1:
        tiles = [jnp.maximum(tiles[i], tiles[i + 1]) for i in range(0, len(tiles), 2)]
    return tiles[0] if mx is None else jnp.maximum(mx, tiles[0])


def _attn_scores_finish(q, km, mx, sm_ref, m_ref):
    sm = _dot_nt(q, km)
    sm_ref[...] = sm
    m = jnp.maximum(jnp.max(mx, axis=1, keepdims=True), jnp.max(sm, axis=1, keepdims=True))
    m_ref[...] = jnp.broadcast_to(m, m_ref.shape)


def _attn_weighted_start(sm_ref, m_ref, vm):
    m = m_ref[...]
    return m, _dot(jnp.exp2(sm_ref[...] - m[:, 0:N_META]).astype(BF16), vm)


def _attn_weighted_block(s_ref, kb, m, v_ref, acc):
    rows = slice(kb * ATT_KV_BLOCK, (kb + 1) * ATT_KV_BLOCK)
    p = [jnp.exp2(s_ref[kb, :, c * LANES:(c + 1) * LANES] - m) for c in range(ATT_KV_BLOCK // LANES)]
    return acc + _dot(jnp.concatenate(p, axis=1).astype(BF16), v_ref[rows, :])


def _attn_core_body(n_kv_blocks, n_q, n_tiles, meta_queries, refs):
    t = pl.program_id(0)
    kt_ref, v_ref = refs["kt"], refs["v"]
    km = refs["k_meta"][...]
    vm = refs["v_meta"][...]
    s_ref, sm_ref, m_ref = refs["s"], refs["sm"], refs["m"]

    def stack(ref):
        return jnp.concatenate([ref[:, g * ATT_HEAD_DIM:(g + 1) * ATT_HEAD_DIM] for g in range(ATT_GROUP)], axis=0)

    def unstack(acc, ref):
        o = acc[:, 0:ATT_HEAD_DIM] / acc[:, ATT_HEAD_DIM:]
        rows = ref.shape[0]
        for g in range(ATT_GROUP):
            ref[:, g * ATT_HEAD_DIM:(g + 1) * ATT_HEAD_DIM] = o[g * rows:(g + 1) * rows, :].astype(BF16)

    @pl.when(t == 0)
    def _():
        for ref in (s_ref, sm_ref, m_ref):
            ref[...] = jnp.zeros_like(ref)

    q = stack(refs["q"])
    m, acc = _attn_weighted_start(sm_ref, m_ref, vm)
    mx = None
    for kb in range(n_kv_blocks):
        acc = _attn_weighted_block(s_ref, kb, m, v_ref, acc)
        mx = _attn_scores_block(q, kt_ref, kb, s_ref, mx)
    _attn_scores_finish(q, km, mx, sm_ref, m_ref)
    unstack(acc, refs["o"])

    if meta_queries:
        @pl.when(jnp.minimum(t, n_tiles - 1) % n_q == n_q - 1)
        def _():
            sq_ref, smq_ref, mq_ref = refs["sq"], refs["smq"], refs["mq"]
            q = stack(refs["q_meta"])
            mx = None
            for kb in range(n_kv_blocks):
                mx = _attn_scores_block(q, kt_ref, kb, sq_ref, mx)
            _attn_scores_finish(q, km, mx, smq_ref, mq_ref)
            m, acc = _attn_weighted_start(smq_ref, mq_ref, vm)
            for kb in range(n_kv_blocks):
                acc = _attn_weighted_block(sq_ref, kb, m, v_ref, acc)
            unstack(acc, refs["o_meta"])


def _attn_core(bsz, seq, t, meta_queries):
    n_q = seq // ATT_Q_TILE
    assert n_q >= 2
    n_tiles = bsz * ATT_KV_HEADS * n_q
    meta_blk0 = bsz * seq // N_META
    n_kv_blocks = seq // ATT_KV_BLOCK
    group_w = ATT_GROUP * ATT_HEAD_DIM

    def coords(tile):
        return tile // (ATT_KV_HEADS * n_q), (tile // n_q) % ATT_KV_HEADS, tile % n_q

    def spec(shape, which, fn):
        tile = (lambda s: jnp.minimum(s, n_tiles - 1)) if which == "cur" else (lambda s: jnp.maximum(s - 1, 0))
        return pl.BlockSpec(shape, lambda s: fn(*coords(tile(s))))

    def stash(sfx, rows):
        return [("s" + sfx, pltpu.VMEM((n_kv_blocks, rows, ATT_KV_BLOCK), F32)),
                ("sm" + sfx, pltpu.VMEM((rows, N_META), F32)), ("m" + sfx, pltpu.VMEM((rows, LANES), F32))]

    ins = [("q", t["q"], spec((ATT_Q_TILE, group_w), "cur", lambda b, kh, i: (b * n_q + i, kh))),
           ("kt", t["kt"], spec((ATT_HEAD_DIM, seq), "cur", lambda b, kh, i: (kh, b))),
           ("k_meta", t["k"], spec((N_META, ATT_HEAD_DIM), "cur", lambda b, kh, i: (meta_blk0 + b, kh))),
           ("v", t["v"], spec((seq, 2 * ATT_HEAD_DIM), "prev", lambda b, kh, i: (b, kh))),
           ("v_meta", t["v"], spec((N_META, 2 * ATT_HEAD_DIM), "prev", lambda b, kh, i: (meta_blk0 + b, kh)))]
    outs = [("o", jax.ShapeDtypeStruct((bsz * seq, ATT_Q_W), BF16),
             spec((ATT_Q_TILE, group_w), "prev", lambda b, kh, i: (b * n_q + i, kh)))]
    scratch = stash("", ATT_GROUP * ATT_Q_TILE)
    if meta_queries:
        ins.append(("q_meta", t["q"], spec((N_META, group_w), "cur", lambda b, kh, i: (meta_blk0 + b, kh))))
        outs.append(("o_meta", jax.ShapeDtypeStruct((bsz * N_META, ATT_Q_W), BF16),
                     spec((N_META, group_w), "cur", lambda b, kh, i: (b, kh))))
        scratch += stash("q", ATT_GROUP * N_META)
    return _call(functools.partial(_attn_core_body, n_kv_blocks, n_q, n_tiles, meta_queries), (n_tiles + 1,),
                 ins, outs, scratch, ("arbitrary",), "attn_core")


def kernel(x, meta_tokens, norm_ffn1, ffn1_w_gate, ffn1_w_up, ffn1_w_down, norm_mix, gla_w_in, gla_gate_w1, gla_gate_w2, gla_gate_b, gla_head_norm, gla_w_out, attn_w_in, attn_q_norm, attn_k_norm, attn_w_out, norm_ffn2, ffn2_w_gate, ffn2_w_up, ffn2_w_down, norm_final):
    bsz, seq, d = x.shape
    depth = norm_ffn1.shape[0]
    assert d == D_MODEL and bsz * N_META == ROW_TILE and seq % STEP_ROWS == 0 and seq % ATT_KV_BLOCK == 0
    n_real = bsz * seq
    row = lambda g: g.reshape(1, -1).astype(F32)
    small = lambda key, arr: (key, arr, _const_spec(arr.shape))

    h = x.reshape(n_real, d)
    h_meta = jnp.broadcast_to(meta_tokens.astype(x.dtype)[None], (bsz, N_META, d)).reshape(bsz * N_META, d)
    (ffn1_wg, ffn1_wu, ffn1_wd, ffn2_wg, ffn2_wu, ffn2_wd, gla_in_w, gla_out_w, attn_in_w, attn_out_w) = _to_bf16(
        [ffn1_w_gate, ffn1_w_up, ffn1_w_down, ffn2_w_gate, ffn2_w_up, ffn2_w_down,
         gla_w_in, gla_w_out, attn_w_in, attn_w_out])

    for i in range(depth):
        j = i // N_MIXERS
        last = i == depth - 1
        ffn1 = (row(norm_ffn1[i]), ffn1_wg, ffn1_wu, ffn1_wd, i)
        ffn2 = (row(norm_ffn2[i]), ffn2_wg, ffn2_wu, ffn2_wd, i)
        final_gain = row(norm_final) if last else None
        if i % N_MIXERS == 0:
            zero = jnp.zeros_like(gla_gate_w2[j, 0])
            gate_w2 = jnp.concatenate([jnp.concatenate([gla_gate_w2[j, 0], zero], axis=1),
                                       jnp.concatenate([zero, gla_gate_w2[j, 1]], axis=1)], axis=0)
            gate_w1 = jnp.concatenate([gla_gate_w1[j, 0], gla_gate_w1[j, 1]], axis=1)
            mixer_ins = [_layer_operand("w_in", gla_in_w, j), small("gate_w1", gate_w1.astype(BF16)),
                         small("gate_w2", gate_w2.astype(BF16)),
                         small("gate_b", gla_gate_b[j].reshape(1, -1).astype(F32))]
            t = _token_stage_in("gla", h, h_meta, ffn1, row(norm_mix[i]), mixer_ins, seq)
            core = _gla_scan(bsz, seq, t)
            h = _token_stage_out("gla", core, t["h_out"], _layer_operand("w_out", gla_out_w, j), ffn2,
                                 {"r": t["r"], "head_gain": row(gla_head_norm[j])}, final_gain, n_real)
        else:
            mixer_ins = [_layer_operand("w_in", attn_in_w, j), small("q_gain", row(attn_q_norm[j])),
                         small("k_gain", row(attn_k_norm[j]))]
            t = _token_stage_in("attn", h, h_meta, ffn1, row(norm_mix[i]), mixer_ins, seq)
            core = _attn_core(bsz, seq, t, meta_queries=not last)
            h = _token_stage_out("attn", core, t["h_out"], _layer_operand("w_out", attn_out_w, j), ffn2, {},
                                 final_gain, n_real)
        h_meta = None
    return h.reshape(bsz, seq, d)
```

```python
import functools

import jax
import jax.numpy as jnp
from jax import lax
from jax.experimental import pallas as pl
from jax.experimental.pallas import tpu as pltpu

F32 = jnp.float32
BF16 = jnp.bfloat16

D_MODEL = 1024
N_META = 16
GRID_W = 64
D_FF = 2816
NORM_EPS = 1e-6
MACARON_WEIGHT = 0.5
N_MIXERS = 2

GLA_HEADS = 4
GLA_DK = D_MODEL // 2 // GLA_HEADS
GLA_DV = D_MODEL // GLA_HEADS
GLA_QK_W = GLA_HEADS * GLA_DK
GLA_V_W = GLA_HEADS * GLA_DV
GLA_GATE_TAU = 16.0
GLA_CHUNK = 64

ATT_Q_HEADS = 8
ATT_KV_HEADS = 2
ATT_HEAD_DIM = D_MODEL // ATT_Q_HEADS
ATT_GROUP = ATT_Q_HEADS // ATT_KV_HEADS
ATT_Q_W = ATT_Q_HEADS * ATT_HEAD_DIM
ATT_KV_W = ATT_KV_HEADS * ATT_HEAD_DIM
ROPE_THETA = 10000.0
ROPE_AXIS_DIM = ATT_HEAD_DIM // 2
LOG2_E = 1.4426950408889634

LANES = 128
ROW_TILE = 256
STEP_TILES = 2
STEP_ROWS = STEP_TILES * ROW_TILE
STAGE_LAG = 5
FF_CHUNK = 256
GLA_BLOCK = 256
GLA_BATCHES = 2
ATT_Q_TILE = 512
ATT_KV_BLOCK = 512
CAST_STEPS = 32
BF16_SUBLANES = 16
V7X_VMEM_BYTES = 64 * 1024 * 1024
VMEM_LIMIT = V7X_VMEM_BYTES - 8 * 1024 * 1024


def _params(semantics):
    return pltpu.CompilerParams(dimension_semantics=semantics, vmem_limit_bytes=VMEM_LIMIT)


def _const_spec(shape):
    nd = len(shape)
    return pl.BlockSpec(shape, lambda *_: (0,) * nd, pipeline_mode=pl.Buffered(1))


def _rms(x, gain):
    ms = jnp.mean(x * x, axis=-1, keepdims=True)
    return x * lax.rsqrt(ms + NORM_EPS) * gain


def _dot(a, b):
    return jnp.dot(a, b, preferred_element_type=F32)


def _dot_nt(a, b):
    return lax.dot_general(a, b, (((1,), (1,)), ((), ())), preferred_element_type=F32)


def _call(body, grid, ins, outs, scratch, semantics, name):
    keys = [k for k, _, _ in ins] + [k for k, _, _ in outs] + [k for k, _ in scratch]

    def kernel_body(*refs):
        body(dict(zip(keys, refs, strict=True)))

    results = pl.pallas_call(
        kernel_body,
        grid=grid,
        in_specs=[s for _, _, s in ins],
        out_specs=[s for _, _, s in outs],
        out_shape=[a for _, a, _ in outs],
        scratch_shapes=[s for _, s in scratch],
        compiler_params=_params(semantics),
        name=name,
    )(*[a for _, a, _ in ins])
    return dict(zip([k for k, _, _ in outs], results, strict=True))


def _row_spec(width, clamp_steps=None):
    if clamp_steps is None:
        return pl.BlockSpec((STEP_ROWS, width), lambda i: (i, 0))
    return pl.BlockSpec((STEP_ROWS, width), lambda i: (jnp.minimum(i, clamp_steps - 1), 0))


def _tile_rows(tile):
    return slice(tile * ROW_TILE, (tile + 1) * ROW_TILE)


def _meta_tile(tile, real, meta_ref, n_real_steps):
    if tile > 0 or meta_ref is None:
        return real
    return jnp.where(pl.program_id(0) == n_real_steps, meta_ref[...], real)


def _swiglu_pieces(env, refs, a_ref):
    def norm():
        env["xn"] = _rms(env["x"], refs["ffn_gain"][...]).astype(BF16)

    def chunk(c):
        cols = slice(c * FF_CHUNK, (c + 1) * FF_CHUNK)
        g = _dot(env["xn"], refs["w_gate"][:, cols])
        u = _dot(env["xn"], refs["w_up"][:, cols])
        a_ref[:, cols] = (g * jax.nn.sigmoid(g) * u).astype(BF16)

    def down():
        env["y"] = env["x"] + MACARON_WEIGHT * _dot(a_ref[...], refs["w_down"][...])

    return [norm] + [functools.partial(chunk, c) for c in range(D_FF // FF_CHUNK)] + [down]


def _interleave(pipelines, lag):
    n = max(len(p) for p in pipelines)
    for step in range(n + lag * (len(pipelines) - 1)):
        for i, p in enumerate(pipelines):
            k = step - i * lag
            if 0 <= k < len(p):
                p[k]()


def _gla_in_pieces(env, refs, rows):
    win_ref = refs["w_in"]

    def norm():
        env["yn"] = _rms(env["y"], refs["mix_gain"][...]).astype(BF16)

    def q():
        refs["q"][rows, :] = (_dot(env["yn"], win_ref[:, 0:GLA_QK_W]) * (GLA_DK ** -0.5)).astype(BF16)

    def k():
        refs["k"][rows, :] = _dot(env["yn"], win_ref[:, GLA_QK_W:2 * GLA_QK_W]).astype(BF16)

    def v(half):
        cols = slice(half * GLA_QK_W, (half + 1) * GLA_QK_W)
        refs["v"][rows, cols] = _dot(env["yn"], win_ref[:, 2 * GLA_QK_W + half * GLA_QK_W:
                                                        2 * GLA_QK_W + (half + 1) * GLA_QK_W]).astype(BF16)

    def r(half):
        cols = slice(half * GLA_QK_W, (half + 1) * GLA_QK_W)
        base = 2 * GLA_QK_W + GLA_V_W
        refs["r"][rows, cols] = _dot(env["yn"], win_ref[:, base + half * GLA_QK_W:
                                                        base + (half + 1) * GLA_QK_W]).astype(BF16)

    def gates():
        t = _dot(env["yn"], refs["gate_w1"][...]).astype(BF16)
        z = _dot(t, refs["gate_w2"][...]) + refs["gate_b"][...]
        logg = (jnp.minimum(z, 0.0) - jnp.log(1.0 + jnp.exp(-jnp.abs(z)))) * (1.0 / GLA_GATE_TAU)
        refs["lg_fwd"][rows, :] = logg[:, 0:GLA_QK_W]
        refs["lg_rev"][rows, :] = logg[:, GLA_QK_W:]

    return [norm, gates, q, k, functools.partial(v, 0), functools.partial(v, 1),
            functools.partial(r, 0), functools.partial(r, 1)]


def _gla_out(o, r, h, refs):
    x = h
    for hd in range(GLA_HEADS):
        vs = slice(hd * GLA_DV, (hd + 1) * GLA_DV)
        r_h = r[:, vs].astype(F32)
        y = (_rms(o[:, vs], refs["head_gain"][...]) * (r_h * jax.nn.sigmoid(r_h))).astype(BF16)
        x = x + _dot(y, refs["w_out"][vs, :])
    return x


def _rope(x, cos, sin_signed, first_half):
    partner = jnp.where(first_half, pltpu.roll(x, LANES - ROPE_AXIS_DIM // 2, 1),
                        pltpu.roll(x, ROPE_AXIS_DIM // 2, 1))
    return x * cos + partner * sin_signed


def _attn_in_pieces(env, refs, rows):
    win_ref = refs["w_in"]

    def norm():
        env["yn"] = _rms(env["y"], refs["mix_gain"][...]).astype(BF16)
        cos = refs["cos"][rows, :]
        lane = lax.broadcasted_iota(jnp.int32, cos.shape, 1)
        env["rope"] = (cos, refs["sin"][rows, :], (lane % ROPE_AXIS_DIM) < (ROPE_AXIS_DIM // 2))

    def q(pair):
        cols = slice(2 * pair * ATT_HEAD_DIM, (2 * pair + 2) * ATT_HEAD_DIM)
        qq = _dot(env["yn"], win_ref[:, cols])
        for i in range(2):
            hs = slice(i * ATT_HEAD_DIM, (i + 1) * ATT_HEAD_DIM)
            qh = _rope(_rms(qq[:, hs], refs["q_gain"][...]), *env["rope"]) * (ATT_HEAD_DIM ** -0.5 * LOG2_E)
            refs["q"][rows, (2 * pair + i) * ATT_HEAD_DIM:(2 * pair + i + 1) * ATT_HEAD_DIM] = qh.astype(BF16)

    def kv():
        k = _dot(env["yn"], win_ref[:, ATT_Q_W:ATT_Q_W + ATT_KV_W])
        v = _dot(env["yn"], win_ref[:, ATT_Q_W + ATT_KV_W:]).astype(BF16)
        ones = jnp.ones((v.shape[0], ATT_HEAD_DIM), BF16)
        for hd in range(ATT_KV_HEADS):
            hs = slice(hd * ATT_HEAD_DIM, (hd + 1) * ATT_HEAD_DIM)
            kh = _rope(_rms(k[:, hs], refs["k_gain"][...]), *env["rope"])
            refs["k"][rows, hs] = kh.astype(BF16)
            refs["kt"][hs, rows] = kh.astype(BF16).T
            refs["v"][rows, 2 * hd * ATT_HEAD_DIM:(2 * hd + 1) * ATT_HEAD_DIM] = v[:, hs]
            refs["v"][rows, (2 * hd + 1) * ATT_HEAD_DIM:(2 * hd + 2) * ATT_HEAD_DIM] = ones

    return [norm] + [functools.partial(q, pair) for pair in range(ATT_Q_HEADS // 2)] + [kv]


def _rope_tables(seq):
    pos = jnp.arange(seq)
    inv_freq = ROPE_THETA ** (-jnp.arange(0, ROPE_AXIS_DIM, 2, dtype=F32) / ROPE_AXIS_DIM)
    ang_row = (pos // GRID_W).astype(F32)[:, None] * inv_freq[None, :]
    ang_col = (pos % GRID_W).astype(F32)[:, None] * inv_freq[None, :]
    ang = jnp.concatenate([ang_row, ang_row, ang_col, ang_col], axis=1)
    ang = jnp.concatenate([ang, jnp.zeros((STEP_ROWS, ATT_HEAD_DIM), F32)], axis=0)
    sign = jnp.tile(jnp.concatenate([-jnp.ones((ROPE_AXIS_DIM // 2,), F32),
                                     jnp.ones((ROPE_AXIS_DIM // 2,), F32)]), 2)
    return jnp.cos(ang), jnp.sin(ang) * sign[None, :]


def _act_scratch():
    return [(f"act{tile}", pltpu.VMEM((ROW_TILE, D_FF), BF16)) for tile in range(STEP_TILES)]


def _layer_operand(key, stack, layer):
    return (key, stack, pl.BlockSpec((None,) + stack.shape[1:], lambda *_: (layer, 0, 0),
                                     pipeline_mode=pl.Buffered(1)))


def _ffn_operands(gain, wg, wu, wd, layer):
    return [("ffn_gain", gain, _const_spec((1, D_MODEL))), _layer_operand("w_gate", wg, layer),
            _layer_operand("w_up", wu, layer), _layer_operand("w_down", wd, layer)]


def _to_bf16(weights):
    flat = [w.reshape(-1, w.shape[-1]) for w in weights]
    ins, outs = [], []
    for n, f in enumerate(flat):
        rows = f.shape[0] // CAST_STEPS
        assert f.shape[0] % CAST_STEPS == 0 and rows % BF16_SUBLANES == 0
        spec = pl.BlockSpec((rows, f.shape[1]), lambda i: (i, 0))
        ins.append((f"in{n}", f, spec))
        outs.append((f"out{n}", jax.ShapeDtypeStruct(f.shape, BF16), spec))

    def body(refs):
        for n in range(len(flat)):
            refs[f"out{n}"][...] = refs[f"in{n}"][...].astype(BF16)

    res = _call(body, (CAST_STEPS,), ins, outs, [], ("parallel",), "weights_to_bf16")
    return [res[f"out{n}"].reshape(w.shape) for n, w in enumerate(weights)]


def _stage_in_body(mixer, n_real_steps, refs):
    def pipeline(tile):
        rows, env = _tile_rows(tile), {}

        def load():
            env["x"] = _meta_tile(tile, refs["h"][rows, :], refs.get("h_meta"), n_real_steps)

        def store():
            refs["h_out"][rows, :] = env["y"]

        mixer_in = _gla_in_pieces if mixer == "gla" else _attn_in_pieces
        return [load] + _swiglu_pieces(env, refs, refs[f"act{tile}"]) + [store] + mixer_in(env, refs, rows)

    _interleave([pipeline(tile) for tile in range(STEP_TILES)], STAGE_LAG)


def _token_stage_in(mixer, h, h_meta, ffn, mix_gain, mixer_ins, seq):
    n_real_steps = h.shape[0] // STEP_ROWS
    n_rows = h.shape[0] + (h_meta.shape[0] if h_meta is not None else 0)
    ins = [("h", h, _row_spec(D_MODEL, n_real_steps if h_meta is not None else None))]
    if h_meta is not None:
        assert h_meta.shape[0] == ROW_TILE
        ins.append(("h_meta", h_meta, _const_spec((ROW_TILE, D_MODEL))))
    ins += _ffn_operands(*ffn) + [("mix_gain", mix_gain, _const_spec((1, D_MODEL)))] + mixer_ins
    shp = lambda w, dt: jax.ShapeDtypeStruct((n_rows, w), dt)
    outs = [("h_out", shp(D_MODEL, F32), _row_spec(D_MODEL))]
    if mixer == "gla":
        outs += [("q", shp(GLA_QK_W, BF16), _row_spec(GLA_QK_W)), ("k", shp(GLA_QK_W, BF16), _row_spec(GLA_QK_W)),
                 ("v", shp(GLA_V_W, BF16), _row_spec(GLA_V_W)), ("r", shp(GLA_V_W, BF16), _row_spec(GLA_V_W)),
                 ("lg_fwd", shp(GLA_QK_W, F32), _row_spec(GLA_QK_W)),
                 ("lg_rev", shp(GLA_QK_W, F32), _row_spec(GLA_QK_W))]
    else:
        n_seq_steps = seq // STEP_ROWS
        real_steps = (n_rows - ROW_TILE) // STEP_ROWS
        table = pl.BlockSpec((STEP_ROWS, ATT_HEAD_DIM),
                             lambda i: (jnp.where(i < real_steps, i % n_seq_steps, n_seq_steps), 0))
        cos, sin = _rope_tables(seq)
        ins += [("cos", cos, table), ("sin", sin, table)]
        outs += [("q", shp(ATT_Q_W, BF16), _row_spec(ATT_Q_W)), ("k", shp(ATT_KV_W, BF16), _row_spec(ATT_KV_W)),
                 ("kt", jax.ShapeDtypeStruct((ATT_KV_W, n_rows), BF16),
                  pl.BlockSpec((ATT_KV_W, STEP_ROWS), lambda i: (0, i))),
                 ("v", shp(2 * ATT_KV_W, BF16), _row_spec(2 * ATT_KV_W))]
    return _call(functools.partial(_stage_in_body, mixer, n_real_steps), (pl.cdiv(n_rows, STEP_ROWS),), ins, outs,
                 _act_scratch(), ("parallel",), f"token_stage_in_{mixer}")


def _stage_out_body(mixer, n_real_steps, final_norm, refs):
    def pipeline(tile):
        rows, env = _tile_rows(tile), {}

        def mix_out():
            h = refs["h"][rows, :]
            if mixer == "gla":
                o = (_meta_tile(tile, refs["o_fwd"][rows, :], refs.get("o_fwd_meta"), n_real_steps)
                     + _meta_tile(tile, refs["o_rev"][rows, :], refs.get("o_rev_meta"), n_real_steps))
                env["x"] = _gla_out(o, refs["r"][rows, :], h, refs)
            else:
                o = _meta_tile(tile, refs["o"][rows, :], refs.get("o_meta"), n_real_steps)
                env["x"] = h + _dot(o, refs["w_out"][...])

        def store():
            refs["h_out"][rows, :] = _rms(env["y"], refs["final_gain"][...]) if final_norm else env["y"]

        return [mix_out] + _swiglu_pieces(env, refs, refs[f"act{tile}"]) + [store]

    _interleave([pipeline(tile) for tile in range(STEP_TILES)], STAGE_LAG)


def _token_stage_out(mixer, core, h, w_out, ffn, extra, final_gain, n_real):
    last = final_gain is not None
    n_real_steps = n_real // STEP_ROWS
    n_rows = n_real if last else h.shape[0]
    has_meta = not last
    ins = [("h", h, _row_spec(D_MODEL))]
    if mixer == "gla":
        ins += [("o_fwd", core["o_fwd"], _row_spec(GLA_V_W, n_real_steps)),
                ("o_rev", core["o_rev"], _row_spec(GLA_V_W, n_real_steps)),
                ("r", extra["r"], _row_spec(GLA_V_W)), ("head_gain", extra["head_gain"], _const_spec((1, GLA_DV)))]
        if has_meta:
            ins += [("o_fwd_meta", core["o_fwd_meta"], _const_spec((ROW_TILE, GLA_V_W))),
                    ("o_rev_meta", core["o_rev_meta"], _const_spec((ROW_TILE, GLA_V_W)))]
    else:
        ins += [("o", core["o"], _row_spec(ATT_Q_W, n_real_steps))]
        if has_meta:
            ins += [("o_meta", core["o_meta"], _const_spec((ROW_TILE, ATT_Q_W)))]
    ins += [w_out] + _ffn_operands(*ffn)
    if last:
        ins.append(("final_gain", final_gain, _const_spec((1, D_MODEL))))
    outs = [("h_out", jax.ShapeDtypeStruct((n_rows, D_MODEL), F32), _row_spec(D_MODEL))]
    return _call(functools.partial(_stage_out_body, mixer, n_real_steps, last), (pl.cdiv(n_rows, STEP_ROWS),),
                 ins, outs, _act_scratch(), ("parallel",), f"token_stage_out_{mixer}")["h_out"]


def _gla_chunks_local(items):
    t_len = items[0][1].shape[0]
    row = lax.broadcasted_iota(jnp.int32, (t_len, t_len), 0)
    col = lax.broadcasted_iota(jnp.int32, (t_len, t_len), 1)
    row3 = lax.broadcasted_iota(jnp.int32, (t_len, 3 * t_len), 0)
    col3 = lax.broadcasted_iota(jnp.int32, (t_len, 3 * t_len), 1) % t_len
    mask = {False: col <= row, True: col >= row}
    mask3 = {False: (col3 <= row3).astype(BF16), True: (col3 >= row3).astype(BF16)}
    pieces = []
    for _, _, _, _, lg in items:
        hi = lg.astype(BF16)
        rest = lg - hi.astype(F32)
        mid = rest.astype(BF16)
        pieces.append(jnp.concatenate([hi, mid, (rest - mid.astype(F32)).astype(BF16)], axis=0))
    bs = [_dot(mask3[item[0]], p) for item, p in zip(items, pieces)]
    scaled = []
    for (rev, q, k, _, _), b in zip(items, bs):
        b_tot = b[0:1, :] if rev else b[t_len - 1:t_len, :]
        scaled.append(((q * jnp.exp(b)).astype(BF16), (k * jnp.exp(-b)).astype(BF16),
                       (k * jnp.exp(b_tot - b)).astype(BF16), jnp.exp(b_tot)))
    heads = [slice(h * GLA_DK, (h + 1) * GLA_DK) for h in range(GLA_HEADS)]
    scores = [[_dot_nt(qd[:, ks], ki[:, ks]) for ks in heads] for qd, ki, _, _ in scaled]
    lhs = [[jnp.concatenate([jnp.where(mask[item[0]], s, 0.0).astype(BF16), ke[:, ks].T], axis=0)
            for s, ks in zip(ss, heads)] for item, ss, (_, _, ke, _) in zip(items, scores, scaled)]
    local = [[_dot(l, item[3][:, h * GLA_DV:(h + 1) * GLA_DV]) for h, l in enumerate(ls)]
             for item, ls in zip(items, lhs)]
    return [(qd, dec, loc) for (qd, _, _, dec), loc in zip(scaled, local)]


def _gla_chunks_carry(chunks, states):
    t_len = chunks[0][0].shape[0]
    heads = [slice(h * GLA_DK, (h + 1) * GLA_DK) for h in range(GLA_HEADS)]
    carried = [[_dot(qd[:, ks], st.astype(BF16)) for ks, st in zip(heads, sts)]
               for (qd, _, _), sts in zip(chunks, states)]
    outs = [jnp.concatenate([loc[0:t_len, :] + c for loc, c in zip(local, cs)], axis=1)
            for (_, _, local), cs in zip(chunks, carried)]
    dec_cols = [[jnp.broadcast_to(dec[:, ks], (GLA_DK, GLA_DK)).T for ks in heads] for _, dec, _ in chunks]
    new_states = [[jnp.concatenate([dc] * (GLA_DV // GLA_DK), axis=1) * st + loc[t_len:, :]
                   for dc, st, loc in zip(dcs, sts, local)]
                  for dcs, sts, (_, _, local) in zip(dec_cols, states, chunks)]
    return outs, new_states


def _gla_scan_body(refs):
    j = pl.program_id(1)
    n_chunks = GLA_BLOCK // GLA_CHUNK
    pad = GLA_CHUNK - N_META
    streams = [(slot, rev) for slot in range(GLA_BATCHES) for rev in (False, True)]
    tag = lambda slot, rev: ("_rev" if rev else "_fwd") + str(slot)

    @pl.when(j == 0)
    def _():
        for s in streams:
            refs["state" + tag(*s)][...] = jnp.zeros_like(refs["state" + tag(*s)])

    def load_states(s):
        return [refs["state" + tag(*s)][h] for h in range(GLA_HEADS)]

    def store_states(s, states):
        for h in range(GLA_HEADS):
            refs["state" + tag(*s)][h] = states[h]

    def meta_chunks(rev):
        front = lambda ref: jnp.concatenate([jnp.zeros((pad, ref.shape[1]), ref.dtype), ref[...]], axis=0)
        for slot in range(GLA_BATCHES):
            m = "_meta" + str(slot)
            (qd, dec, local), = _gla_chunks_local([(rev, front(refs["q" + m]), front(refs["k" + m]),
                                                    front(refs["v" + m]),
                                                    front(refs[("lg_rev" if rev else "lg_fwd") + m]))])
            (o,), (states,) = _gla_chunks_carry([(qd, dec, local)], [load_states((slot, rev))])
            store_states((slot, rev), states)
            refs["o_rev_meta" if rev else "o_fwd_meta"][slot] = o[pad:, :]

    pl.when(j == 0)(lambda: meta_chunks(False))

    rows = lambda c: slice(c * GLA_CHUNK, (c + 1) * GLA_CHUNK)
    states = {s: load_states(s) for s in streams}
    for ci in range(n_chunks):
        items = []
        for s in streams:
            c = n_chunks - 1 - ci if s[1] else ci
            items.append((s[1], refs["q" + tag(*s)][rows(c), :], refs["k" + tag(*s)][rows(c), :],
                          refs["v" + tag(*s)][rows(c), :], refs["lg" + tag(*s)][rows(c), :]))
        outs, new_states = _gla_chunks_carry(_gla_chunks_local(items), [states[s] for s in streams])
        for s, o, st in zip(streams, outs, new_states):
            c = n_chunks - 1 - ci if s[1] else ci
            states[s] = st
            refs["o_rev" if s[1] else "o_fwd"][s[0], rows(c), :] = o
    for s in streams:
        store_states(s, states[s])

    pl.when(j == pl.num_programs(1) - 1)(lambda: meta_chunks(True))


def _gla_scan(bsz, seq, t):
    assert bsz % GLA_BATCHES == 0
    n_blk = seq // GLA_BLOCK
    meta_blk0 = bsz * seq // N_META
    ins, scratch = [], []
    for slot in range(GLA_BATCHES):
        batch = lambda p, slot=slot: p * GLA_BATCHES + slot
        fwd = lambda w, batch=batch: pl.BlockSpec((GLA_BLOCK, w), lambda p, j: (batch(p) * n_blk + j, 0))
        rev = lambda w, batch=batch: pl.BlockSpec((GLA_BLOCK, w), lambda p, j: (batch(p) * n_blk + n_blk - 1 - j, 0))
        meta = lambda w, batch=batch: pl.BlockSpec((N_META, w), lambda p, j: (meta_blk0 + batch(p), 0))
        for sfx, spec in (("_fwd", fwd), ("_rev", rev)):
            sfx_slot = sfx + str(slot)
            ins += [("q" + sfx_slot, t["q"], spec(GLA_QK_W)), ("k" + sfx_slot, t["k"], spec(GLA_QK_W)),
                    ("v" + sfx_slot, t["v"], spec(GLA_V_W)), ("lg" + sfx_slot, t["lg" + sfx], spec(GLA_QK_W))]
            scratch.append(("state" + sfx_slot, pltpu.VMEM((GLA_HEADS, GLA_DK, GLA_DV), F32)))
        m = "_meta" + str(slot)
        ins += [("q" + m, t["q"], meta(GLA_QK_W)), ("k" + m, t["k"], meta(GLA_QK_W)), ("v" + m, t["v"], meta(GLA_V_W)),
                ("lg_fwd" + m, t["lg_fwd"], meta(GLA_QK_W)), ("lg_rev" + m, t["lg_rev"], meta(GLA_QK_W))]
    real_shape = jax.ShapeDtypeStruct((bsz, seq, GLA_V_W), F32)
    meta_shape = jax.ShapeDtypeStruct((bsz, N_META, GLA_V_W), F32)
    meta_out = pl.BlockSpec((GLA_BATCHES, N_META, GLA_V_W), lambda p, j: (p, 0, 0))
    outs = [("o_fwd", real_shape, pl.BlockSpec((GLA_BATCHES, GLA_BLOCK, GLA_V_W), lambda p, j: (p, j, 0))),
            ("o_rev", real_shape, pl.BlockSpec((GLA_BATCHES, GLA_BLOCK, GLA_V_W), lambda p, j: (p, n_blk - 1 - j, 0))),
            ("o_fwd_meta", meta_shape, meta_out), ("o_rev_meta", meta_shape, meta_out)]
    res = _call(_gla_scan_body, (bsz // GLA_BATCHES, n_blk), ins, outs, scratch, ("parallel", "arbitrary"), "gla_scan")
    return {"o_fwd": res["o_fwd"].reshape(bsz * seq, GLA_V_W), "o_rev": res["o_rev"].reshape(bsz * seq, GLA_V_W),
            "o_fwd_meta": res["o_fwd_meta"].reshape(bsz * N_META, GLA_V_W),
            "o_rev_meta": res["o_rev_meta"].reshape(bsz * N_META, GLA_V_W)}


def _attn_scores_block(q, kt_ref, kb, s_ref, mx):
    cols = slice(kb * ATT_KV_BLOCK, (kb + 1) * ATT_KV_BLOCK)
    s = _dot(q, kt_ref[:, cols])
    s_ref[kb] = s
    tiles = [s[:, c * LANES:(c + 1) * LANES] for c in range(ATT_KV_BLOCK // LANES)]
    while len(tiles) > 1:
        tiles = [jnp.maximum(tiles[i], tiles[i + 1]) for i in range(0, len(tiles), 2)]
    return tiles[0] if mx is None else jnp.maximum(mx, tiles[0])


def _attn_scores_finish(q, km, mx, sm_ref, m_ref):
    sm = _dot_nt(q, km)
    sm_ref[...] = sm
    m = jnp.maximum(jnp.max(mx, axis=1, keepdims=True), jnp.max(sm, axis=1, keepdims=True))
    m_ref[...] = jnp.broadcast_to(m, m_ref.shape)


def _attn_weighted_start(sm_ref, m_ref, vm):
    m = m_ref[...]
    return m, _dot(jnp.exp2(sm_ref[...] - m[:, 0:N_META]).astype(BF16), vm)


def _attn_weighted_block(s_ref, kb, m, v_ref, acc):
    rows = slice(kb * ATT_KV_BLOCK, (kb + 1) * ATT_KV_BLOCK)
    p = [jnp.exp2(s_ref[kb, :, c * LANES:(c + 1) * LANES] - m) for c in range(ATT_KV_BLOCK // LANES)]
    return acc + _dot(jnp.concatenate(p, axis=1).astype(BF16), v_ref[rows, :])


def _attn_core_body(n_kv_blocks, n_q, n_tiles, meta_queries, refs):
    t = pl.program_id(0)
    kt_ref, v_ref = refs["kt"], refs["v"]
    km = refs["k_meta"][...]
    vm = refs["v_meta"][...]
    s_ref, sm_ref, m_ref = refs["s"], refs["sm"], refs["m"]

    def stack(ref):
        return jnp.concatenate([ref[:, g * ATT_HEAD_DIM:(g + 1) * ATT_HEAD_DIM] for g in range(ATT_GROUP)], axis=0)

    def unstack(acc, ref):
        o = acc[:, 0:ATT_HEAD_DIM] / acc[:, ATT_HEAD_DIM:]
        rows = ref.shape[0]
        for g in range(ATT_GROUP):
            ref[:, g * ATT_HEAD_DIM:(g + 1) * ATT_HEAD_DIM] = o[g * rows:(g + 1) * rows, :].astype(BF16)

    @pl.when(t == 0)
    def _():
        for ref in (s_ref, sm_ref, m_ref):
            ref[...] = jnp.zeros_like(ref)

    q = stack(refs["q"])
    m, acc = _attn_weighted_start(sm_ref, m_ref, vm)
    mx = None
    for kb in range(n_kv_blocks):
        acc = _attn_weighted_block(s_ref, kb, m, v_ref, acc)
        mx = _attn_scores_block(q, kt_ref, kb, s_ref, mx)
    _attn_scores_finish(q, km, mx, sm_ref, m_ref)
    unstack(acc, refs["o"])

    if meta_queries:
        @pl.when(jnp.minimum(t, n_tiles - 1) % n_q == n_q - 1)
        def _():
            sq_ref, smq_ref, mq_ref = refs["sq"], refs["smq"], refs["mq"]
            q = stack(refs["q_meta"])
            mx = None
            for kb in range(n_kv_blocks):
                mx = _attn_scores_block(q, kt_ref, kb, sq_ref, mx)
            _attn_scores_finish(q, km, mx, smq_ref, mq_ref)
            m, acc = _attn_weighted_start(smq_ref, mq_ref, vm)
            for kb in range(n_kv_blocks):
                acc = _attn_weighted_block(sq_ref, kb, m, v_ref, acc)
            unstack(acc, refs["o_meta"])


def _attn_core(bsz, seq, t, meta_queries):
    n_q = seq // ATT_Q_TILE
    assert n_q >= 2
    n_tiles = bsz * ATT_KV_HEADS * n_q
    meta_blk0 = bsz * seq // N_META
    n_kv_blocks = seq // ATT_KV_BLOCK
    group_w = ATT_GROUP * ATT_HEAD_DIM

    def coords(tile):
        return tile // (ATT_KV_HEADS * n_q), (tile // n_q) % ATT_KV_HEADS, tile % n_q

    def spec(shape, which, fn):
        tile = (lambda s: jnp.minimum(s, n_tiles - 1)) if which == "cur" else (lambda s: jnp.maximum(s - 1, 0))
        return pl.BlockSpec(shape, lambda s: fn(*coords(tile(s))))

    def stash(sfx, rows):
        return [("s" + sfx, pltpu.VMEM((n_kv_blocks, rows, ATT_KV_BLOCK), F32)),
                ("sm" + sfx, pltpu.VMEM((rows, N_META), F32)), ("m" + sfx, pltpu.VMEM((rows, LANES), F32))]

    ins = [("q", t["q"], spec((ATT_Q_TILE, group_w), "cur", lambda b, kh, i: (b * n_q + i, kh))),
           ("kt", t["kt"], spec((ATT_HEAD_DIM, seq), "cur", lambda b, kh, i: (kh, b))),
           ("k_meta", t["k"], spec((N_META, ATT_HEAD_DIM), "cur", lambda b, kh, i: (meta_blk0 + b, kh))),
           ("v", t["v"], spec((seq, 2 * ATT_HEAD_DIM), "prev", lambda b, kh, i: (b, kh))),
           ("v_meta", t["v"], spec((N_META, 2 * ATT_HEAD_DIM), "prev", lambda b, kh, i: (meta_blk0 + b, kh)))]
    outs = [("o", jax.ShapeDtypeStruct((bsz * seq, ATT_Q_W), BF16),
             spec((ATT_Q_TILE, group_w), "prev", lambda b, kh, i: (b * n_q + i, kh)))]
    scratch = stash("", ATT_GROUP * ATT_Q_TILE)
    if meta_queries:
        ins.append(("q_meta", t["q"], spec((N_META, group_w), "cur", lambda b, kh, i: (meta_blk0 + b, kh))))
        outs.append(("o_meta", jax.ShapeDtypeStruct((bsz * N_META, ATT_Q_W), BF16),
                     spec((N_META, group_w), "cur", lambda b, kh, i: (b, kh))))
        scratch += stash("q", ATT_GROUP * N_META)
    return _call(functools.partial(_attn_core_body, n_kv_blocks, n_q, n_tiles, meta_queries), (n_tiles + 1,),
                 ins, outs, scratch, ("arbitrary",), "attn_core")


def kernel(x, meta_tokens, norm_ffn1, ffn1_w_gate, ffn1_w_up, ffn1_w_down, norm_mix, gla_w_in, gla_gate_w1, gla_gate_w2, gla_gate_b, gla_head_norm, gla_w_out, attn_w_in, attn_q_norm, attn_k_norm, attn_w_out, norm_ffn2, ffn2_w_gate, ffn2_w_up, ffn2_w_down, norm_final):
    bsz, seq, d = x.shape
    depth = norm_ffn1.shape[0]
    assert d == D_MODEL and bsz * N_META == ROW_TILE and seq % STEP_ROWS == 0 and seq % ATT_KV_BLOCK == 0
    n_real = bsz * seq
    row = lambda g: g.reshape(1, -1).astype(F32)
    small = lambda key, arr: (key, arr, _const_spec(arr.shape))

    h = x.reshape(n_real, d)
    h_meta = jnp.broadcast_to(meta_tokens.astype(x.dtype)[None], (bsz, N_META, d)).reshape(bsz * N_META, d)
    (ffn1_wg, ffn1_wu, ffn1_wd, ffn2_wg, ffn2_wu, ffn2_wd, gla_in_w, gla_out_w, attn_in_w, attn_out_w) = _to_bf16(
        [ffn1_w_gate, ffn1_w_up, ffn1_w_down, ffn2_w_gate, ffn2_w_up, ffn2_w_down,
         gla_w_in, gla_w_out, attn_w_in, attn_w_out])

    for i in range(depth):
        j = i // N_MIXERS
        last = i == depth - 1
        ffn1 = (row(norm_ffn1[i]), ffn1_wg, ffn1_wu, ffn1_wd, i)
        ffn2 = (row(norm_ffn2[i]), ffn2_wg, ffn2_wu, ffn2_wd, i)
        final_gain = row(norm_final) if last else None
        if i % N_MIXERS == 0:
            zero = jnp.zeros_like(gla_gate_w2[j, 0])
            gate_w2 = jnp.concatenate([jnp.concatenate([gla_gate_w2[j, 0], zero], axis=1),
                                       jnp.concatenate([zero, gla_gate_w2[j, 1]], axis=1)], axis=0)
            gate_w1 = jnp.concatenate([gla_gate_w1[j, 0], gla_gate_w1[j, 1]], axis=1)
            mixer_ins = [_layer_operand("w_in", gla_in_w, j), small("gate_w1", gate_w1.astype(BF16)),
                         small("gate_w2", gate_w2.astype(BF16)),
                         small("gate_b", gla_gate_b[j].reshape(1, -1).astype(F32))]
            t = _token_stage_in("gla", h, h_meta, ffn1, row(norm_mix[i]), mixer_ins, seq)
            core = _gla_scan(bsz, seq, t)
            h = _token_stage_out("gla", core, t["h_out"], _layer_operand("w_out", gla_out_w, j), ffn2,
                                 {"r": t["r"], "head_gain": row(gla_head_norm[j])}, final_gain, n_real)
        else:
            mixer_ins = [_layer_operand("w_in", attn_in_w, j), small("q_gain", row(attn_q_norm[j])),
                         small("k_gain", row(attn_k_norm[j]))]
            t = _token_stage_in("attn", h, h_meta, ffn1, row(norm_mix[i]), mixer_ins, seq)
            core = _attn_core(bsz, seq, t, meta_queries=not last)
            h = _token_stage_out("attn", core, t["h_out"], _layer_operand("w_out", attn_out_w, j), ffn2, {},
                                 final_gain, n_real)
        h_meta = None
    return h.reshape(bsz, seq, d)
```

```python
import functools

import jax
import jax.numpy as jnp
from jax import lax
from jax.experimental import pallas as pl
from jax.experimental.pallas import tpu as pltpu

F32 = jnp.float32
BF16 = jnp.bfloat16

D_MODEL = 1024
N_META = 16
GRID_W = 64
D_FF = 2816
NORM_EPS = 1e-6
MACARON_WEIGHT = 0.5
N_MIXERS = 2

GLA_HEADS = 4
GLA_DK = D_MODEL // 2 // GLA_HEADS
GLA_DV = D_MODEL // GLA_HEADS
GLA_QK_W = GLA_HEADS * GLA_DK
GLA_V_W = GLA_HEADS * GLA_DV
GLA_GATE_TAU = 16.0
GLA_CHUNK = 64

ATT_Q_HEADS = 8
ATT_KV_HEADS = 2
ATT_HEAD_DIM = D_MODEL // ATT_Q_HEADS
ATT_GROUP = ATT_Q_HEADS // ATT_KV_HEADS
ATT_Q_W = ATT_Q_HEADS * ATT_HEAD_DIM
ATT_KV_W = ATT_KV_HEADS * ATT_HEAD_DIM
ROPE_THETA = 10000.0
ROPE_AXIS_DIM = ATT_HEAD_DIM // 2
LOG2_E = 1.4426950408889634

LANES = 128
ROW_TILE = 256
STEP_TILES = 2
STEP_ROWS = STEP_TILES * ROW_TILE
STAGE_LAG = 5
FF_CHUNK = 256
GLA_BLOCK = 512
GLA_BATCHES = 2
ATT_Q_TILE = 512
ATT_KV_BLOCK = 512
CAST_STEPS = 32
BF16_SUBLANES = 16
V7X_VMEM_BYTES = 64 * 1024 * 1024
VMEM_LIMIT = V7X_VMEM_BYTES - 8 * 1024 * 1024


def _params(semantics):
    return pltpu.CompilerParams(dimension_semantics=semantics, vmem_limit_bytes=VMEM_LIMIT)


def _const_spec(shape):
    nd = len(shape)
    return pl.BlockSpec(shape, lambda *_: (0,) * nd, pipeline_mode=pl.Buffered(1))


def _rms(x, gain):
    ms = jnp.mean(x * x, axis=-1, keepdims=True)
    return x * lax.rsqrt(ms + NORM_EPS) * gain


def _dot(a, b):
    return jnp.dot(a, b, preferred_element_type=F32)


def _dot_nt(a, b):
    return lax.dot_general(a, b, (((1,), (1,)), ((), ())), preferred_element_type=F32)


def _call(body, grid, ins, outs, scratch, semantics, name):
    keys = [k for k, _, _ in ins] + [k for k, _, _ in outs] + [k for k, _ in scratch]

    def kernel_body(*refs):
        body(dict(zip(keys, refs, strict=True)))

    results = pl.pallas_call(
        kernel_body,
        grid=grid,
        in_specs=[s for _, _, s in ins],
        out_specs=[s for _, _, s in outs],
        out_shape=[a for _, a, _ in outs],
        scratch_shapes=[s for _, s in scratch],
        compiler_params=_params(semantics),
        name=name,
    )(*[a for _, a, _ in ins])
    return dict(zip([k for k, _, _ in outs], results, strict=True))


def _row_spec(width, clamp_steps=None):
    if clamp_steps is None:
        return pl.BlockSpec((STEP_ROWS, width), lambda i: (i, 0))
    return pl.BlockSpec((STEP_ROWS, width), lambda i: (jnp.minimum(i, clamp_steps - 1), 0))


def _tile_rows(tile):
    return slice(tile * ROW_TILE, (tile + 1) * ROW_TILE)


def _meta_tile(tile, real, meta_ref, n_real_steps):
    if tile > 0 or meta_ref is None:
        return real
    return jnp.where(pl.program_id(0) == n_real_steps, meta_ref[...], real)


def _swiglu_pieces(env, refs, a_ref):
    def norm():
        env["xn"] = _rms(env["x"], refs["ffn_gain"][...]).astype(BF16)

    def chunk(c):
        cols = slice(c * FF_CHUNK, (c + 1) * FF_CHUNK)
        g = _dot(env["xn"], refs["w_gate"][:, cols])
        u = _dot(env["xn"], refs["w_up"][:, cols])
        a_ref[:, cols] = (g * jax.nn.sigmoid(g) * u).astype(BF16)

    def down():
        env["y"] = env["x"] + MACARON_WEIGHT * _dot(a_ref[...], refs["w_down"][...])

    return [norm] + [functools.partial(chunk, c) for c in range(D_FF // FF_CHUNK)] + [down]


def _interleave(pipelines, lag):
    n = max(len(p) for p in pipelines)
    for step in range(n + lag * (len(pipelines) - 1)):
        for i, p in enumerate(pipelines):
            k = step - i * lag
            if 0 <= k < len(p):
                p[k]()


def _gla_in_pieces(env, refs, rows):
    win_ref = refs["w_in"]

    def norm():
        env["yn"] = _rms(env["y"], refs["mix_gain"][...]).astype(BF16)

    def q():
        refs["q"][rows, :] = (_dot(env["yn"], win_ref[:, 0:GLA_QK_W]) * (GLA_DK ** -0.5)).astype(BF16)

    def k():
        refs["k"][rows, :] = _dot(env["yn"], win_ref[:, GLA_QK_W:2 * GLA_QK_W]).astype(BF16)

    def v(half):
        cols = slice(half * GLA_QK_W, (half + 1) * GLA_QK_W)
        refs["v"][rows, cols] = _dot(env["yn"], win_ref[:, 2 * GLA_QK_W + half * GLA_QK_W:
                                                        2 * GLA_QK_W + (half + 1) * GLA_QK_W]).astype(BF16)

    def r(half):
        cols = slice(half * GLA_QK_W, (half + 1) * GLA_QK_W)
        base = 2 * GLA_QK_W + GLA_V_W
        refs["r"][rows, cols] = _dot(env["yn"], win_ref[:, base + half * GLA_QK_W:
                                                        base + (half + 1) * GLA_QK_W]).astype(BF16)

    def gates():
        t = _dot(env["yn"], refs["gate_w1"][...]).astype(BF16)
        z = _dot(t, refs["gate_w2"][...]) + refs["gate_b"][...]
        logg = (jnp.minimum(z, 0.0) - jnp.log(1.0 + jnp.exp(-jnp.abs(z)))) * (1.0 / GLA_GATE_TAU)
        refs["lg_fwd"][rows, :] = logg[:, 0:GLA_QK_W]
        refs["lg_rev"][rows, :] = logg[:, GLA_QK_W:]

    return [norm, gates, q, k, functools.partial(v, 0), functools.partial(v, 1),
            functools.partial(r, 0), functools.partial(r, 1)]


def _gla_out(o, r, h, refs):
    x = h
    for hd in range(GLA_HEADS):
        vs = slice(hd * GLA_DV, (hd + 1) * GLA_DV)
        r_h = r[:, vs].astype(F32)
        y = (_rms(o[:, vs], refs["head_gain"][...]) * (r_h * jax.nn.sigmoid(r_h))).astype(BF16)
        x = x + _dot(y, refs["w_out"][vs, :])
    return x


def _rope(x, cos, sin_signed, first_half):
    partner = jnp.where(first_half, pltpu.roll(x, LANES - ROPE_AXIS_DIM // 2, 1),
                        pltpu.roll(x, ROPE_AXIS_DIM // 2, 1))
    return x * cos + partner * sin_signed


def _attn_in_pieces(env, refs, rows):
    win_ref = refs["w_in"]

    def norm():
        env["yn"] = _rms(env["y"], refs["mix_gain"][...]).astype(BF16)
        cos = refs["cos"][rows, :]
        lane = lax.broadcasted_iota(jnp.int32, cos.shape, 1)
        env["rope"] = (cos, refs["sin"][rows, :], (lane % ROPE_AXIS_DIM) < (ROPE_AXIS_DIM // 2))

    def q(pair):
        cols = slice(2 * pair * ATT_HEAD_DIM, (2 * pair + 2) * ATT_HEAD_DIM)
        qq = _dot(env["yn"], win_ref[:, cols])
        for i in range(2):
            hs = slice(i * ATT_HEAD_DIM, (i + 1) * ATT_HEAD_DIM)
            qh = _rope(_rms(qq[:, hs], refs["q_gain"][...]), *env["rope"]) * (ATT_HEAD_DIM ** -0.5 * LOG2_E)
            refs["q"][rows, (2 * pair + i) * ATT_HEAD_DIM:(2 * pair + i + 1) * ATT_HEAD_DIM] = qh.astype(BF16)

    def kv():
        k = _dot(env["yn"], win_ref[:, ATT_Q_W:ATT_Q_W + ATT_KV_W])
        v = _dot(env["yn"], win_ref[:, ATT_Q_W + ATT_KV_W:]).astype(BF16)
        ones = jnp.ones((v.shape[0], ATT_HEAD_DIM), BF16)
        for hd in range(ATT_KV_HEADS):
            hs = slice(hd * ATT_HEAD_DIM, (hd + 1) * ATT_HEAD_DIM)
            kh = _rope(_rms(k[:, hs], refs["k_gain"][...]), *env["rope"])
            refs["k"][rows, hs] = kh.astype(BF16)
            refs["kt"][hs, rows] = kh.astype(BF16).T
            refs["v"][rows, 2 * hd * ATT_HEAD_DIM:(2 * hd + 1) * ATT_HEAD_DIM] = v[:, hs]
            refs["v"][rows, (2 * hd + 1) * ATT_HEAD_DIM:(2 * hd + 2) * ATT_HEAD_DIM] = ones

    return [norm] + [functools.partial(q, pair) for pair in range(ATT_Q_HEADS // 2)] + [kv]


def _rope_tables(seq):
    pos = jnp.arange(seq)
    inv_freq = ROPE_THETA ** (-jnp.arange(0, ROPE_AXIS_DIM, 2, dtype=F32) / ROPE_AXIS_DIM)
    ang_row = (pos // GRID_W).astype(F32)[:, None] * inv_freq[None, :]
    ang_col = (pos % GRID_W).astype(F32)[:, None] * inv_freq[None, :]
    ang = jnp.concatenate([ang_row, ang_row, ang_col, ang_col], axis=1)
    ang = jnp.concatenate([ang, jnp.zeros((STEP_ROWS, ATT_HEAD_DIM), F32)], axis=0)
    sign = jnp.tile(jnp.concatenate([-jnp.ones((ROPE_AXIS_DIM // 2,), F32),
                                     jnp.ones((ROPE_AXIS_DIM // 2,), F32)]), 2)
    return jnp.cos(ang), jnp.sin(ang) * sign[None, :]


def _act_scratch():
    return [(f"act{tile}", pltpu.VMEM((ROW_TILE, D_FF), BF16)) for tile in range(STEP_TILES)]


def _layer_operand(key, stack, layer):
    return (key, stack, pl.BlockSpec((None,) + stack.shape[1:], lambda *_: (layer, 0, 0),
                                     pipeline_mode=pl.Buffered(1)))


def _ffn_operands(gain, wg, wu, wd, layer):
    return [("ffn_gain", gain, _const_spec((1, D_MODEL))), _layer_operand("w_gate", wg, layer),
            _layer_operand("w_up", wu, layer), _layer_operand("w_down", wd, layer)]


def _to_bf16(weights):
    flat = [w.reshape(-1, w.shape[-1]) for w in weights]
    ins, outs = [], []
    for n, f in enumerate(flat):
        rows = f.shape[0] // CAST_STEPS
        assert f.shape[0] % CAST_STEPS == 0 and rows % BF16_SUBLANES == 0
        spec = pl.BlockSpec((rows, f.shape[1]), lambda i: (i, 0))
        ins.append((f"in{n}", f, spec))
        outs.append((f"out{n}", jax.ShapeDtypeStruct(f.shape, BF16), spec))

    def body(refs):
        for n in range(len(flat)):
            refs[f"out{n}"][...] = refs[f"in{n}"][...].astype(BF16)

    res = _call(body, (CAST_STEPS,), ins, outs, [], ("parallel",), "weights_to_bf16")
    return [res[f"out{n}"].reshape(w.shape) for n, w in enumerate(weights)]


def _stage_in_body(mixer, n_real_steps, refs):
    def pipeline(tile):
        rows, env = _tile_rows(tile), {}

        def load():
            env["x"] = _meta_tile(tile, refs["h"][rows, :], refs.get("h_meta"), n_real_steps)

        def store():
            refs["h_out"][rows, :] = env["y"]

        mixer_in = _gla_in_pieces if mixer == "gla" else _attn_in_pieces
        return [load] + _swiglu_pieces(env, refs, refs[f"act{tile}"]) + [store] + mixer_in(env, refs, rows)

    _interleave([pipeline(tile) for tile in range(STEP_TILES)], STAGE_LAG)


def _token_stage_in(mixer, h, h_meta, ffn, mix_gain, mixer_ins, seq):
    n_real_steps = h.shape[0] // STEP_ROWS
    n_rows = h.shape[0] + (h_meta.shape[0] if h_meta is not None else 0)
    ins = [("h", h, _row_spec(D_MODEL, n_real_steps if h_meta is not None else None))]
    if h_meta is not None:
        assert h_meta.shape[0] == ROW_TILE
        ins.append(("h_meta", h_meta, _const_spec((ROW_TILE, D_MODEL))))
    ins += _ffn_operands(*ffn) + [("mix_gain", mix_gain, _const_spec((1, D_MODEL)))] + mixer_ins
    shp = lambda w, dt: jax.ShapeDtypeStruct((n_rows, w), dt)
    outs = [("h_out", shp(D_MODEL, F32), _row_spec(D_MODEL))]
    if mixer == "gla":
        outs += [("q", shp(GLA_QK_W, BF16), _row_spec(GLA_QK_W)), ("k", shp(GLA_QK_W, BF16), _row_spec(GLA_QK_W)),
                 ("v", shp(GLA_V_W, BF16), _row_spec(GLA_V_W)), ("r", shp(GLA_V_W, BF16), _row_spec(GLA_V_W)),
                 ("lg_fwd", shp(GLA_QK_W, F32), _row_spec(GLA_QK_W)),
                 ("lg_rev", shp(GLA_QK_W, F32), _row_spec(GLA_QK_W))]
    else:
        n_seq_steps = seq // STEP_ROWS
        real_steps = (n_rows - ROW_TILE) // STEP_ROWS
        table = pl.BlockSpec((STEP_ROWS, ATT_HEAD_DIM),
                             lambda i: (jnp.where(i < real_steps, i % n_seq_steps, n_seq_steps), 0))
        cos, sin = _rope_tables(seq)
        ins += [("cos", cos, table), ("sin", sin, table)]
        outs += [("q", shp(ATT_Q_W, BF16), _row_spec(ATT_Q_W)), ("k", shp(ATT_KV_W, BF16), _row_spec(ATT_KV_W)),
                 ("kt", jax.ShapeDtypeStruct((ATT_KV_W, n_rows), BF16),
                  pl.BlockSpec((ATT_KV_W, STEP_ROWS), lambda i: (0, i))),
                 ("v", shp(2 * ATT_KV_W, BF16), _row_spec(2 * ATT_KV_W))]
    return _call(functools.partial(_stage_in_body, mixer, n_real_steps), (pl.cdiv(n_rows, STEP_ROWS),), ins, outs,
                 _act_scratch(), ("parallel",), f"token_stage_in_{mixer}")


def _stage_out_body(mixer, n_real_steps, final_norm, refs):
    def pipeline(tile):
        rows, env = _tile_rows(tile), {}

        def mix_out():
            h = refs["h"][rows, :]
            if mixer == "gla":
                o = (_meta_tile(tile, refs["o_fwd"][rows, :], refs.get("o_fwd_meta"), n_real_steps)
                     + _meta_tile(tile, refs["o_rev"][rows, :], refs.get("o_rev_meta"), n_real_steps))
                env["x"] = _gla_out(o, refs["r"][rows, :], h, refs)
            else:
                o = _meta_tile(tile, refs["o"][rows, :], refs.get("o_meta"), n_real_steps)
                env["x"] = h + _dot(o, refs["w_out"][...])

        def store():
            refs["h_out"][rows, :] = _rms(env["y"], refs["final_gain"][...]) if final_norm else env["y"]

        return [mix_out] + _swiglu_pieces(env, refs, refs[f"act{tile}"]) + [store]

    _interleave([pipeline(tile) for tile in range(STEP_TILES)], STAGE_LAG)


def _token_stage_out(mixer, core, h, w_out, ffn, extra, final_gain, n_real):
    last = final_gain is not None
    n_real_steps = n_real // STEP_ROWS
    n_rows = n_real if last else h.shape[0]
    has_meta = not last
    ins = [("h", h, _row_spec(D_MODEL))]
    if mixer == "gla":
        ins += [("o_fwd", core["o_fwd"], _row_spec(GLA_V_W, n_real_steps)),
                ("o_rev", core["o_rev"], _row_spec(GLA_V_W, n_real_steps)),
                ("r", extra["r"], _row_spec(GLA_V_W)), ("head_gain", extra["head_gain"], _const_spec((1, GLA_DV)))]
        if has_meta:
            ins += [("o_fwd_meta", core["o_fwd_meta"], _const_spec((ROW_TILE, GLA_V_W))),
                    ("o_rev_meta", core["o_rev_meta"], _const_spec((ROW_TILE, GLA_V_W)))]
    else:
        ins += [("o", core["o"], _row_spec(ATT_Q_W, n_real_steps))]
        if has_meta:
            ins += [("o_meta", core["o_meta"], _const_spec((ROW_TILE, ATT_Q_W)))]
    ins += [w_out] + _ffn_operands(*ffn)
    if last:
        ins.append(("final_gain", final_gain, _const_spec((1, D_MODEL))))
    outs = [("h_out", jax.ShapeDtypeStruct((n_rows, D_MODEL), F32), _row_spec(D_MODEL))]
    return _call(functools.partial(_stage_out_body, mixer, n_real_steps, last), (pl.cdiv(n_rows, STEP_ROWS),),
                 ins, outs, _act_scratch(), ("parallel",), f"token_stage_out_{mixer}")["h_out"]


def _gla_chunks_local(items):
    t_len = items[0][1].shape[0]
    row = lax.broadcasted_iota(jnp.int32, (t_len, t_len), 0)
    col = lax.broadcasted_iota(jnp.int32, (t_len, t_len), 1)
    row3 = lax.broadcasted_iota(jnp.int32, (t_len, 3 * t_len), 0)
    col3 = lax.broadcasted_iota(jnp.int32, (t_len, 3 * t_len), 1) % t_len
    mask = {False: col <= row, True: col >= row}
    mask3 = {False: (col3 <= row3).astype(BF16), True: (col3 >= row3).astype(BF16)}
    pieces = []
    for _, _, _, _, lg in items:
        hi = lg.astype(BF16)
        rest = lg - hi.astype(F32)
        mid = rest.astype(BF16)
        pieces.append(jnp.concatenate([hi, mid, (rest - mid.astype(F32)).astype(BF16)], axis=0))
    bs = [_dot(mask3[item[0]], p) for item, p in zip(items, pieces)]
    scaled = []
    for (rev, q, k, _, _), b in zip(items, bs):
        b_tot = b[0:1, :] if rev else b[t_len - 1:t_len, :]
        scaled.append(((q * jnp.exp(b)).astype(BF16), (k * jnp.exp(-b)).astype(BF16),
                       (k * jnp.exp(b_tot - b)).astype(BF16), jnp.exp(b_tot)))
    heads = [slice(h * GLA_DK, (h + 1) * GLA_DK) for h in range(GLA_HEADS)]
    scores = [[_dot_nt(qd[:, ks], ki[:, ks]) for ks in heads] for qd, ki, _, _ in scaled]
    lhs = [[jnp.concatenate([jnp.where(mask[item[0]], s, 0.0).astype(BF16), ke[:, ks].T], axis=0)
            for s, ks in zip(ss, heads)] for item, ss, (_, _, ke, _) in zip(items, scores, scaled)]
    local = [[_dot(l, item[3][:, h * GLA_DV:(h + 1) * GLA_DV]) for h, l in enumerate(ls)]
             for item, ls in zip(items, lhs)]
    return [(qd, dec, loc) for (qd, _, _, dec), loc in zip(scaled, local)]


def _gla_chunks_carry(chunks, states):
    t_len = chunks[0][0].shape[0]
    heads = [slice(h * GLA_DK, (h + 1) * GLA_DK) for h in range(GLA_HEADS)]
    carried = [[_dot(qd[:, ks], st.astype(BF16)) for ks, st in zip(heads, sts)]
               for (qd, _, _), sts in zip(chunks, states)]
    outs = [jnp.concatenate([loc[0:t_len, :] + c for loc, c in zip(local, cs)], axis=1)
            for (_, _, local), cs in zip(chunks, carried)]
    dec_cols = [[jnp.broadcast_to(dec[:, ks], (GLA_DK, GLA_DK)).T for ks in heads] for _, dec, _ in chunks]
    new_states = [[jnp.concatenate([dc] * (GLA_DV // GLA_DK), axis=1) * st + loc[t_len:, :]
                   for dc, st, loc in zip(dcs, sts, local)]
                  for dcs, sts, (_, _, local) in zip(dec_cols, states, chunks)]
    return outs, new_states


def _gla_scan_body(refs):
    j = pl.program_id(1)
    n_chunks = GLA_BLOCK // GLA_CHUNK
    pad = GLA_CHUNK - N_META
    streams = [(slot, rev) for slot in range(GLA_BATCHES) for rev in (False, True)]
    tag = lambda slot, rev: ("_rev" if rev else "_fwd") + str(slot)

    @pl.when(j == 0)
    def _():
        for s in streams:
            refs["state" + tag(*s)][...] = jnp.zeros_like(refs["state" + tag(*s)])

    def load_states(s):
        return [refs["state" + tag(*s)][h] for h in range(GLA_HEADS)]

    def store_states(s, states):
        for h in range(GLA_HEADS):
            refs["state" + tag(*s)][h] = states[h]

    def meta_chunks(rev):
        front = lambda ref: jnp.concatenate([jnp.zeros((pad, ref.shape[1]), ref.dtype), ref[...]], axis=0)
        for slot in range(GLA_BATCHES):
            m = "_meta" + str(slot)
            (qd, dec, local), = _gla_chunks_local([(rev, front(refs["q" + m]), front(refs["k" + m]),
                                                    front(refs["v" + m]),
                                                    front(refs[("lg_rev" if rev else "lg_fwd") + m]))])
            (o,), (states,) = _gla_chunks_carry([(qd, dec, local)], [load_states((slot, rev))])
            store_states((slot, rev), states)
            refs["o_rev_meta" if rev else "o_fwd_meta"][slot] = o[pad:, :]

    pl.when(j == 0)(lambda: meta_chunks(False))

    rows = lambda c: slice(c * GLA_CHUNK, (c + 1) * GLA_CHUNK)
    states = {s: load_states(s) for s in streams}
    for ci in range(n_chunks):
        items = []
        for s in streams:
            c = n_chunks - 1 - ci if s[1] else ci
            items.append((s[1], refs["q" + tag(*s)][rows(c), :], refs["k" + tag(*s)][rows(c), :],
                          refs["v" + tag(*s)][rows(c), :], refs["lg" + tag(*s)][rows(c), :]))
        outs, new_states = _gla_chunks_carry(_gla_chunks_local(items), [states[s] for s in streams])
        for s, o, st in zip(streams, outs, new_states):
            c = n_chunks - 1 - ci if s[1] else ci
            states[s] = st
            refs["o_rev" if s[1] else "o_fwd"][s[0], rows(c), :] = o
    for s in streams:
        store_states(s, states[s])

    pl.when(j == pl.num_programs(1) - 1)(lambda: meta_chunks(True))


def _gla_scan(bsz, seq, t):
    assert bsz % GLA_BATCHES == 0
    n_blk = seq // GLA_BLOCK
    meta_blk0 = bsz * seq // N_META
    ins, scratch = [], []
    for slot in range(GLA_BATCHES):
        batch = lambda p, slot=slot: p * GLA_BATCHES + slot
        fwd = lambda w, batch=batch: pl.BlockSpec((GLA_BLOCK, w), lambda p, j: (batch(p) * n_blk + j, 0))
        rev = lambda w, batch=batch: pl.BlockSpec((GLA_BLOCK, w), lambda p, j: (batch(p) * n_blk + n_blk - 1 - j, 0))
        meta = lambda w, batch=batch: pl.BlockSpec((N_META, w), lambda p, j: (meta_blk0 + batch(p), 0))
        for sfx, spec in (("_fwd", fwd), ("_rev", rev)):
            sfx_slot = sfx + str(slot)
            ins += [("q" + sfx_slot, t["q"], spec(GLA_QK_W)), ("k" + sfx_slot, t["k"], spec(GLA_QK_W)),
                    ("v" + sfx_slot, t["v"], spec(GLA_V_W)), ("lg" + sfx_slot, t["lg" + sfx], spec(GLA_QK_W))]
            scratch.append(("state" + sfx_slot, pltpu.VMEM((GLA_HEADS, GLA_DK, GLA_DV), F32)))
        m = "_meta" + str(slot)
        ins += [("q" + m, t["q"], meta(GLA_QK_W)), ("k" + m, t["k"], meta(GLA_QK_W)), ("v" + m, t["v"], meta(GLA_V_W)),
                ("lg_fwd" + m, t["lg_fwd"], meta(GLA_QK_W)), ("lg_rev" + m, t["lg_rev"], meta(GLA_QK_W))]
    real_shape = jax.ShapeDtypeStruct((bsz, seq, GLA_V_W), F32)
    meta_shape = jax.ShapeDtypeStruct((bsz, N_META, GLA_V_W), F32)
    meta_out = pl.BlockSpec((GLA_BATCHES, N_META, GLA_V_W), lambda p, j: (p, 0, 0))
    outs = [("o_fwd", real_shape, pl.BlockSpec((GLA_BATCHES, GLA_BLOCK, GLA_V_W), lambda p, j: (p, j, 0))),
            ("o_rev", real_shape, pl.BlockSpec((GLA_BATCHES, GLA_BLOCK, GLA_V_W), lambda p, j: (p, n_blk - 1 - j, 0))),
            ("o_fwd_meta", meta_shape, meta_out), ("o_rev_meta", meta_shape, meta_out)]
    res = _call(_gla_scan_body, (bsz // GLA_BATCHES, n_blk), ins, outs, scratch, ("parallel", "arbitrary"), "gla_scan")
    return {"o_fwd": res["o_fwd"].reshape(bsz * seq, GLA_V_W), "o_rev": res["o_rev"].reshape(bsz * seq, GLA_V_W),
            "o_fwd_meta": res["o_fwd_meta"].reshape(bsz * N_META, GLA_V_W),
            "o_rev_meta": res["o_rev_meta"].reshape(bsz * N_META, GLA_V_W)}


def _attn_scores_block(q, kt_ref, kb, s_ref, mx):
    cols = slice(kb * ATT_KV_BLOCK, (kb + 1) * ATT_KV_BLOCK)
    s = _dot(q, kt_ref[:, cols])
    s_ref[kb] = s
    tiles = [s[:, c * LANES:(c + 1) * LANES] for c in range(ATT_KV_BLOCK // LANES)]
    while len(tiles) > 1:
        tiles = [jnp.maximum(tiles[i], tiles[i + 1]) for i in range(0, len(tiles), 2)]
    return tiles[0] if mx is None else jnp.maximum(mx, tiles[0])


def _attn_scores_finish(sm, mx, sm_ref, m_ref):
    sm_ref[...] = sm
    m = jnp.maximum(jnp.max(mx, axis=1, keepdims=True), jnp.max(sm, axis=1, keepdims=True))
    m_ref[...] = jnp.broadcast_to(m, m_ref.shape)


def _attn_weighted_start(sm_ref, m_ref, vm):
    m = m_ref[...]
    return m, _dot(jnp.exp2(sm_ref[...] - m[:, 0:N_META]).astype(BF16), vm)


def _attn_weighted_block(s_ref, kb, m, v_ref, acc):
    rows = slice(kb * ATT_KV_BLOCK, (kb + 1) * ATT_KV_BLOCK)
    p = [jnp.exp2(s_ref[kb, :, c * LANES:(c + 1) * LANES] - m) for c in range(ATT_KV_BLOCK // LANES)]
    return acc + _dot(jnp.concatenate(p, axis=1).astype(BF16), v_ref[rows, :])


def _attn_core_body(n_kv_blocks, n_q, n_tiles, meta_queries, refs):
    t = pl.program_id(0)
    kt_ref, v_ref = refs["kt"], refs["v"]
    km = refs["k_meta"][...]
    vm = refs["v_meta"][...]
    s_ref, sm_ref, m_ref = refs["s"], refs["sm"], refs["m"]

    def stack(ref):
        return jnp.concatenate([ref[:, g * ATT_HEAD_DIM:(g + 1) * ATT_HEAD_DIM] for g in range(ATT_GROUP)], axis=0)

    def unstack(acc, ref):
        o = acc[:, 0:ATT_HEAD_DIM] / acc[:, ATT_HEAD_DIM:]
        rows = ref.shape[0]
        for g in range(ATT_GROUP):
            ref[:, g * ATT_HEAD_DIM:(g + 1) * ATT_HEAD_DIM] = o[g * rows:(g + 1) * rows, :].astype(BF16)

    @pl.when(t == 0)
    def _():
        for ref in (s_ref, sm_ref, m_ref):
            ref[...] = jnp.zeros_like(ref)

    q = stack(refs["q"])
    sm = _dot_nt(q, km)
    m, acc = _attn_weighted_start(sm_ref, m_ref, vm)
    mx = None
    for kb in range(n_kv_blocks):
        acc = _attn_weighted_block(s_ref, kb, m, v_ref, acc)
        mx = _attn_scores_block(q, kt_ref, kb, s_ref, mx)
    _attn_scores_finish(sm, mx, sm_ref, m_ref)
    unstack(acc, refs["o"])

    if meta_queries:
        @pl.when(jnp.minimum(t, n_tiles - 1) % n_q == n_q - 1)
        def _():
            sq_ref, smq_ref, mq_ref = refs["sq"], refs["smq"], refs["mq"]
            q = stack(refs["q_meta"])
            mx = None
            for kb in range(n_kv_blocks):
                mx = _attn_scores_block(q, kt_ref, kb, sq_ref, mx)
            _attn_scores_finish(_dot_nt(q, km), mx, smq_ref, mq_ref)
            m, acc = _attn_weighted_start(smq_ref, mq_ref, vm)
            for kb in range(n_kv_blocks):
                acc = _attn_weighted_block(sq_ref, kb, m, v_ref, acc)
            unstack(acc, refs["o_meta"])


def _attn_core(bsz, seq, t, meta_queries):
    n_q = seq // ATT_Q_TILE
    assert n_q >= 2
    n_tiles = bsz * ATT_KV_HEADS * n_q
    meta_blk0 = bsz * seq // N_META
    n_kv_blocks = seq // ATT_KV_BLOCK
    group_w = ATT_GROUP * ATT_HEAD_DIM

    def coords(tile):
        return tile // (ATT_KV_HEADS * n_q), (tile // n_q) % ATT_KV_HEADS, tile % n_q

    def spec(shape, which, fn):
        tile = (lambda s: jnp.minimum(s, n_tiles - 1)) if which == "cur" else (lambda s: jnp.maximum(s - 1, 0))
        return pl.BlockSpec(shape, lambda s: fn(*coords(tile(s))))

    def stash(sfx, rows):
        return [("s" + sfx, pltpu.VMEM((n_kv_blocks, rows, ATT_KV_BLOCK), F32)),
                ("sm" + sfx, pltpu.VMEM((rows, N_META), F32)), ("m" + sfx, pltpu.VMEM((rows, LANES), F32))]

    ins = [("q", t["q"], spec((ATT_Q_TILE, group_w), "cur", lambda b, kh, i: (b * n_q + i, kh))),
           ("kt", t["kt"], spec((ATT_HEAD_DIM, seq), "cur", lambda b, kh, i: (kh, b))),
           ("k_meta", t["k"], spec((N_META, ATT_HEAD_DIM), "cur", lambda b, kh, i: (meta_blk0 + b, kh))),
           ("v", t["v"], spec((seq, 2 * ATT_HEAD_DIM), "prev", lambda b, kh, i: (b, kh))),
           ("v_meta", t["v"], spec((N_META, 2 * ATT_HEAD_DIM), "prev", lambda b, kh, i: (meta_blk0 + b, kh)))]
    outs = [("o", jax.ShapeDtypeStruct((bsz * seq, ATT_Q_W), BF16),
             spec((ATT_Q_TILE, group_w), "prev", lambda b, kh, i: (b * n_q + i, kh)))]
    scratch = stash("", ATT_GROUP * ATT_Q_TILE)
    if meta_queries:
        ins.append(("q_meta", t["q"], spec((N_META, group_w), "cur", lambda b, kh, i: (meta_blk0 + b, kh))))
        outs.append(("o_meta", jax.ShapeDtypeStruct((bsz * N_META, ATT_Q_W), BF16),
                     spec((N_META, group_w), "cur", lambda b, kh, i: (b, kh))))
        scratch += stash("q", ATT_GROUP * N_META)
    return _call(functools.partial(_attn_core_body, n_kv_blocks, n_q, n_tiles, meta_queries), (n_tiles + 1,),
                 ins, outs, scratch, ("arbitrary",), "attn_core")


def kernel(x, meta_tokens, norm_ffn1, ffn1_w_gate, ffn1_w_up, ffn1_w_down, norm_mix, gla_w_in, gla_gate_w1, gla_gate_w2, gla_gate_b, gla_head_norm, gla_w_out, attn_w_in, attn_q_norm, attn_k_norm, attn_w_out, norm_ffn2, ffn2_w_gate, ffn2_w_up, ffn2_w_down, norm_final):
    bsz, seq, d = x.shape
    depth = norm_ffn1.shape[0]
    assert d == D_MODEL and bsz * N_META == ROW_TILE and seq % STEP_ROWS == 0 and seq % ATT_KV_BLOCK == 0
    n_real = bsz * seq
    row = lambda g: g.reshape(1, -1).astype(F32)
    small = lambda key, arr: (key, arr, _const_spec(arr.shape))

    h = x.reshape(n_real, d)
    h_meta = jnp.broadcast_to(meta_tokens.astype(x.dtype)[None], (bsz, N_META, d)).reshape(bsz * N_META, d)
    (ffn1_wg, ffn1_wu, ffn1_wd, ffn2_wg, ffn2_wu, ffn2_wd, gla_in_w, gla_out_w, attn_in_w, attn_out_w) = _to_bf16(
        [ffn1_w_gate, ffn1_w_up, ffn1_w_down, ffn2_w_gate, ffn2_w_up, ffn2_w_down,
         gla_w_in, gla_w_out, attn_w_in, attn_w_out])

    for i in range(depth):
        j = i // N_MIXERS
        last = i == depth - 1
        ffn1 = (row(norm_ffn1[i]), ffn1_wg, ffn1_wu, ffn1_wd, i)
        ffn2 = (row(norm_ffn2[i]), ffn2_wg, ffn2_wu, ffn2_wd, i)
        final_gain = row(norm_final) if last else None
        if i % N_MIXERS == 0:
            zero = jnp.zeros_like(gla_gate_w2[j, 0])
            gate_w2 = jnp.concatenate([jnp.concatenate([gla_gate_w2[j, 0], zero], axis=1),
                                       jnp.concatenate([zero, gla_gate_w2[j, 1]], axis=1)], axis=0)
            gate_w1 = jnp.concatenate([gla_gate_w1[j, 0], gla_gate_w1[j, 1]], axis=1)
            mixer_ins = [_layer_operand("w_in", gla_in_w, j), small("gate_w1", gate_w1.astype(BF16)),
                         small("gate_w2", gate_w2.astype(BF16)),
                         small("gate_b", gla_gate_b[j].reshape(1, -1).astype(F32))]
            t = _token_stage_in("gla", h, h_meta, ffn1, row(norm_mix[i]), mixer_ins, seq)
            core = _gla_scan(bsz, seq, t)
            h = _token_stage_out("gla", core, t["h_out"], _layer_operand("w_out", gla_out_w, j), ffn2,
                                 {"r": t["r"], "head_gain": row(gla_head_norm[j])}, final_gain, n_real)
        else:
            mixer_ins = [_layer_operand("w_in", attn_in_w, j), small("q_gain", row(attn_q_norm[j])),
                         small("k_gain", row(attn_k_norm[j]))]
            t = _token_stage_in("attn", h, h_meta, ffn1, row(norm_mix[i]), mixer_ins, seq)
            core = _attn_core(bsz, seq, t, meta_queries=not last)
            h = _token_stage_out("attn", core, t["h_out"], _layer_operand("w_out", attn_out_w, j), ffn2, {},
                                 final_gain, n_real)
        h_meta = None
    return h.reshape(bsz, seq, d)
```

```python
import functools

import jax
import jax.numpy as jnp
from jax import lax
from jax.experimental import pallas as pl
from jax.experimental.pallas import tpu as pltpu

F32 = jnp.float32
BF16 = jnp.bfloat16

D_MODEL = 1024
N_META = 16
GRID_W = 64
D_FF = 2816
NORM_EPS = 1e-6
MACARON_WEIGHT = 0.5
N_MIXERS = 2

GLA_HEADS = 4
GLA_DK = D_MODEL // 2 // GLA_HEADS
GLA_DV = D_MODEL // GLA_HEADS
GLA_QK_W = GLA_HEADS * GLA_DK
GLA_V_W = GLA_HEADS * GLA_DV
GLA_GATE_TAU = 16.0
GLA_CHUNK = 64

ATT_Q_HEADS = 8
ATT_KV_HEADS = 2
ATT_HEAD_DIM = D_MODEL // ATT_Q_HEADS
ATT_GROUP = ATT_Q_HEADS // ATT_KV_HEADS
ATT_Q_W = ATT_Q_HEADS * ATT_HEAD_DIM
ATT_KV_W = ATT_KV_HEADS * ATT_HEAD_DIM
ROPE_THETA = 10000.0
ROPE_AXIS_DIM = ATT_HEAD_DIM // 2
LOG2_E = 1.4426950408889634

LANES = 128
ROW_TILE = 256
STEP_TILES = 2
STEP_ROWS = STEP_TILES * ROW_TILE
STAGE_LAG = 5
FF_CHUNK = 256
GLA_BLOCK = 512
GLA_BATCHES = 2
ATT_Q_TILE = 512
ATT_KV_BLOCK = 512
CAST_STEPS = 32
BF16_SUBLANES = 16
V7X_VMEM_BYTES = 64 * 1024 * 1024
VMEM_LIMIT = V7X_VMEM_BYTES - 8 * 1024 * 1024


def _params(semantics):
    return pltpu.CompilerParams(dimension_semantics=semantics, vmem_limit_bytes=VMEM_LIMIT)


def _const_spec(shape):
    nd = len(shape)
    return pl.BlockSpec(shape, lambda *_: (0,) * nd, pipeline_mode=pl.Buffered(1))


def _rms(x, gain):
    ms = jnp.mean(x * x, axis=-1, keepdims=True)
    return x * lax.rsqrt(ms + NORM_EPS) * gain


def _dot(a, b):
    return jnp.dot(a, b, preferred_element_type=F32)


def _dot_nt(a, b):
    return lax.dot_general(a, b, (((1,), (1,)), ((), ())), preferred_element_type=F32)


def _call(body, grid, ins, outs, scratch, semantics, name):
    keys = [k for k, _, _ in ins] + [k for k, _, _ in outs] + [k for k, _ in scratch]

    def kernel_body(*refs):
        body(dict(zip(keys, refs, strict=True)))

    results = pl.pallas_call(
        kernel_body,
        grid=grid,
        in_specs=[s for _, _, s in ins],
        out_specs=[s for _, _, s in outs],
        out_shape=[a for _, a, _ in outs],
        scratch_shapes=[s for _, s in scratch],
        compiler_params=_params(semantics),
        name=name,
    )(*[a for _, a, _ in ins])
    return dict(zip([k for k, _, _ in outs], results, strict=True))


def _row_spec(width, clamp_steps=None, lag=0):
    if clamp_steps is None and lag == 0:
        return pl.BlockSpec((STEP_ROWS, width), lambda i: (i, 0))
    hi = clamp_steps - 1 if clamp_steps is not None else 2 ** 30
    return pl.BlockSpec((STEP_ROWS, width), lambda i: (jnp.clip(i - lag, 0, hi), 0))


def _tile_rows(tile):
    return slice(tile * ROW_TILE, (tile + 1) * ROW_TILE)


def _meta_tile(tile, real, meta_ref, n_real_steps):
    if tile > 0 or meta_ref is None:
        return real
    return jnp.where(pl.program_id(0) >= n_real_steps, meta_ref[...], real)


def _swiglu_pieces(env, refs, a_ref):
    def norm():
        env["xn"] = _rms(env["x"], refs["ffn_gain"][...]).astype(BF16)

    def chunk(c):
        cols = slice(c * FF_CHUNK, (c + 1) * FF_CHUNK)
        g = _dot(env["xn"], refs["w_gate"][:, cols])
        u = _dot(env["xn"], refs["w_up"][:, cols])
        a_ref[:, cols] = (g * jax.nn.sigmoid(g) * u).astype(BF16)

    def down():
        env["y"] = env["x"] + MACARON_WEIGHT * _dot(a_ref[...], refs["w_down"][...])

    return [norm] + [functools.partial(chunk, c) for c in range(D_FF // FF_CHUNK)] + [down]


def _interleave(pipelines, lag):
    n = max(len(p) for p in pipelines)
    for step in range(n + lag * (len(pipelines) - 1)):
        for i, p in enumerate(pipelines):
            k = step - i * lag
            if 0 <= k < len(p):
                p[k]()


def _gla_in_pieces(env, refs, rows):
    win_ref = refs["w_in"]

    def norm():
        env["yn"] = _rms(env["y"], refs["mix_gain"][...]).astype(BF16)

    def q():
        refs["q"][rows, :] = (_dot(env["yn"], win_ref[:, 0:GLA_QK_W]) * (GLA_DK ** -0.5)).astype(BF16)

    def k():
        refs["k"][rows, :] = _dot(env["yn"], win_ref[:, GLA_QK_W:2 * GLA_QK_W]).astype(BF16)

    def v(half):
        cols = slice(half * GLA_QK_W, (half + 1) * GLA_QK_W)
        refs["v"][rows, cols] = _dot(env["yn"], win_ref[:, 2 * GLA_QK_W + half * GLA_QK_W:
                                                        2 * GLA_QK_W + (half + 1) * GLA_QK_W]).astype(BF16)

    def r(half):
        cols = slice(half * GLA_QK_W, (half + 1) * GLA_QK_W)
        base = 2 * GLA_QK_W + GLA_V_W
        refs["r"][rows, cols] = _dot(env["yn"], win_ref[:, base + half * GLA_QK_W:
                                                        base + (half + 1) * GLA_QK_W]).astype(BF16)

    def gates():
        t = _dot(env["yn"], refs["gate_w1"][...]).astype(BF16)
        z = _dot(t, refs["gate_w2"][...]) + refs["gate_b"][...]
        logg = (jnp.minimum(z, 0.0) - jnp.log(1.0 + jnp.exp(-jnp.abs(z)))) * (1.0 / GLA_GATE_TAU)
        refs["lg_fwd"][rows, :] = logg[:, 0:GLA_QK_W]
        refs["lg_rev"][rows, :] = logg[:, GLA_QK_W:]

    return norm, [gates, q, k, functools.partial(v, 0), functools.partial(v, 1),
                  functools.partial(r, 0), functools.partial(r, 1)]


def _gla_out(o, r, h, refs):
    x = h
    for hd in range(GLA_HEADS):
        vs = slice(hd * GLA_DV, (hd + 1) * GLA_DV)
        r_h = r[:, vs].astype(F32)
        y = (_rms(o[:, vs], refs["head_gain"][...]) * (r_h * jax.nn.sigmoid(r_h))).astype(BF16)
        x = x + _dot(y, refs["w_out"][vs, :])
    return x


def _rope(x, cos, sin_signed, first_half):
    partner = jnp.where(first_half, pltpu.roll(x, LANES - ROPE_AXIS_DIM // 2, 1),
                        pltpu.roll(x, ROPE_AXIS_DIM // 2, 1))
    return x * cos + partner * sin_signed


def _attn_in_pieces(env, refs, rows):
    win_ref = refs["w_in"]

    def norm():
        env["yn"] = _rms(env["y"], refs["mix_gain"][...]).astype(BF16)

    def rope_setup():
        cos = refs["cos"][rows, :]
        lane = lax.broadcasted_iota(jnp.int32, cos.shape, 1)
        env["rope"] = (cos, refs["sin"][rows, :], (lane % ROPE_AXIS_DIM) < (ROPE_AXIS_DIM // 2))

    def q(pair):
        cols = slice(2 * pair * ATT_HEAD_DIM, (2 * pair + 2) * ATT_HEAD_DIM)
        qq = _dot(env["yn"], win_ref[:, cols])
        for i in range(2):
            hs = slice(i * ATT_HEAD_DIM, (i + 1) * ATT_HEAD_DIM)
            qh = _rope(_rms(qq[:, hs], refs["q_gain"][...]), *env["rope"]) * (ATT_HEAD_DIM ** -0.5 * LOG2_E)
            refs["q"][rows, (2 * pair + i) * ATT_HEAD_DIM:(2 * pair + i + 1) * ATT_HEAD_DIM] = qh.astype(BF16)

    def kv():
        k = _dot(env["yn"], win_ref[:, ATT_Q_W:ATT_Q_W + ATT_KV_W])
        v = _dot(env["yn"], win_ref[:, ATT_Q_W + ATT_KV_W:]).astype(BF16)
        ones = jnp.ones((v.shape[0], ATT_HEAD_DIM), BF16)
        for hd in range(ATT_KV_HEADS):
            hs = slice(hd * ATT_HEAD_DIM, (hd + 1) * ATT_HEAD_DIM)
            kh = _rope(_rms(k[:, hs], refs["k_gain"][...]), *env["rope"])
            refs["k"][rows, hs] = kh.astype(BF16)
            refs["kt"][hs, rows] = kh.astype(BF16).T
            refs["v"][rows, 2 * hd * ATT_HEAD_DIM:(2 * hd + 1) * ATT_HEAD_DIM] = v[:, hs]
            refs["v"][rows, (2 * hd + 1) * ATT_HEAD_DIM:(2 * hd + 2) * ATT_HEAD_DIM] = ones

    return norm, [rope_setup] + [functools.partial(q, pair) for pair in range(ATT_Q_HEADS // 2)] + [kv]


def _rope_tables(seq):
    pos = jnp.arange(seq)
    inv_freq = ROPE_THETA ** (-jnp.arange(0, ROPE_AXIS_DIM, 2, dtype=F32) / ROPE_AXIS_DIM)
    ang_row = (pos // GRID_W).astype(F32)[:, None] * inv_freq[None, :]
    ang_col = (pos % GRID_W).astype(F32)[:, None] * inv_freq[None, :]
    ang = jnp.concatenate([ang_row, ang_row, ang_col, ang_col], axis=1)
    ang = jnp.concatenate([ang, jnp.zeros((STEP_ROWS, ATT_HEAD_DIM), F32)], axis=0)
    sign = jnp.tile(jnp.concatenate([-jnp.ones((ROPE_AXIS_DIM // 2,), F32),
                                     jnp.ones((ROPE_AXIS_DIM // 2,), F32)]), 2)
    return jnp.cos(ang), jnp.sin(ang) * sign[None, :]


def _act_scratch():
    return [(f"act{tile}", pltpu.VMEM((ROW_TILE, D_FF), BF16)) for tile in range(STEP_TILES)]


def _layer_operand(key, stack, layer):
    return (key, stack, pl.BlockSpec((None,) + stack.shape[1:], lambda *_: (layer, 0, 0),
                                     pipeline_mode=pl.Buffered(1)))


def _ffn_operands(gain, wg, wu, wd, layer):
    return [("ffn_gain", gain, _const_spec((1, D_MODEL))), _layer_operand("w_gate", wg, layer),
            _layer_operand("w_up", wu, layer), _layer_operand("w_down", wd, layer)]


def _to_bf16(weights):
    flat = [w.reshape(-1, w.shape[-1]) for w in weights]
    ins, outs = [], []
    for n, f in enumerate(flat):
        rows = f.shape[0] // CAST_STEPS
        assert f.shape[0] % CAST_STEPS == 0 and rows % BF16_SUBLANES == 0
        spec = pl.BlockSpec((rows, f.shape[1]), lambda i: (i, 0))
        ins.append((f"in{n}", f, spec))
        outs.append((f"out{n}", jax.ShapeDtypeStruct(f.shape, BF16), spec))

    def body(refs):
        for n in range(len(flat)):
            refs[f"out{n}"][...] = refs[f"in{n}"][...].astype(BF16)

    res = _call(body, (CAST_STEPS,), ins, outs, [], ("parallel",), "weights_to_bf16")
    return [res[f"out{n}"].reshape(w.shape) for n, w in enumerate(weights)]


def _stage_in_body(mixer, n_real_steps, refs):
    mixer_in = _gla_in_pieces if mixer == "gla" else _attn_in_pieces
    stash = refs.get("yn_stash")
    if stash is not None:
        @pl.when(pl.program_id(0) == 0)
        def _():
            stash[...] = jnp.zeros_like(stash)

        previous = [stash[_tile_rows(tile), :] for tile in range(STEP_TILES)]

    def half_step(tile):
        rows, env = _tile_rows(tile), {}
        norm, projections = mixer_in(env, refs, rows)

        def load():
            env["x"] = _meta_tile(tile, refs["h"][rows, :], refs.get("h_meta"), n_real_steps)

        def store():
            refs["h_out"][rows, :] = env["y"]
            norm()
            if stash is not None:
                stash[rows, :] = env["yn"]

        pieces = [load] + _swiglu_pieces(env, refs, refs[f"act{tile}"]) + [store]
        return pieces if stash is not None else pieces + projections

    def late_projections(tile):
        return mixer_in({"yn": previous[tile]}, refs, _tile_rows(tile))[1]

    if stash is None:
        _interleave([half_step(tile) for tile in range(STEP_TILES)], STAGE_LAG)
    else:
        _interleave([p(tile) for tile in range(STEP_TILES) for p in (half_step, late_projections)], STAGE_LAG)


def _token_stage_in(mixer, h, h_meta, ffn, mix_gain, mixer_ins, seq):
    n_real_steps = h.shape[0] // STEP_ROWS
    n_rows = h.shape[0] + (h_meta.shape[0] if h_meta is not None else 0)
    n_steps = pl.cdiv(n_rows, STEP_ROWS)
    lag = 1 if mixer == "attn" else 0
    ins = [("h", h, _row_spec(D_MODEL, n_real_steps if h_meta is not None else n_steps))]
    if h_meta is not None:
        assert h_meta.shape[0] == ROW_TILE
        ins.append(("h_meta", h_meta, _const_spec((ROW_TILE, D_MODEL))))
    ins += _ffn_operands(*ffn) + [("mix_gain", mix_gain, _const_spec((1, D_MODEL)))] + mixer_ins
    shp = lambda w, dt: jax.ShapeDtypeStruct((n_rows, w), dt)
    outs = [("h_out", shp(D_MODEL, F32), _row_spec(D_MODEL, n_steps))]
    late = lambda w: _row_spec(w, lag=lag)
    if mixer == "gla":
        outs += [("q", shp(GLA_QK_W, BF16), late(GLA_QK_W)), ("k", shp(GLA_QK_W, BF16), late(GLA_QK_W)),
                 ("v", shp(GLA_V_W, BF16), late(GLA_V_W)), ("r", shp(GLA_V_W, BF16), late(GLA_V_W)),
                 ("lg_fwd", shp(GLA_QK_W, F32), late(GLA_QK_W)),
                 ("lg_rev", shp(GLA_QK_W, F32), late(GLA_QK_W))]
    else:
        n_seq_steps = seq // STEP_ROWS
        real_steps = (n_rows - ROW_TILE) // STEP_ROWS
        table = pl.BlockSpec((STEP_ROWS, ATT_HEAD_DIM),
                             lambda i: (jnp.where(i - lag < real_steps, jnp.maximum(i - lag, 0) % n_seq_steps,
                                                  n_seq_steps), 0))
        cos, sin = _rope_tables(seq)
        ins += [("cos", cos, table), ("sin", sin, table)]
        outs += [("q", shp(ATT_Q_W, BF16), late(ATT_Q_W)), ("k", shp(ATT_KV_W, BF16), late(ATT_KV_W)),
                 ("kt", jax.ShapeDtypeStruct((ATT_KV_W, n_rows), BF16),
                  pl.BlockSpec((ATT_KV_W, STEP_ROWS), lambda i: (0, jnp.maximum(i - lag, 0)))),
                 ("v", shp(2 * ATT_KV_W, BF16), late(2 * ATT_KV_W))]
    scratch = _act_scratch() + ([("yn_stash", pltpu.VMEM((STEP_ROWS, D_MODEL), BF16))] if lag else [])
    return _call(functools.partial(_stage_in_body, mixer, n_real_steps), (n_steps + lag,), ins, outs,
                 scratch, ("arbitrary",), f"token_stage_in_{mixer}")


def _stage_out_body(mixer, n_real_steps, final_norm, refs):
    def pipeline(tile):
        rows, env = _tile_rows(tile), {}

        def mix_out():
            h = refs["h"][rows, :]
            if mixer == "gla":
                o = (_meta_tile(tile, refs["o_fwd"][rows, :], refs.get("o_fwd_meta"), n_real_steps)
                     + _meta_tile(tile, refs["o_rev"][rows, :], refs.get("o_rev_meta"), n_real_steps))
                env["x"] = _gla_out(o, refs["r"][rows, :], h, refs)
            else:
                o = _meta_tile(tile, refs["o"][rows, :], refs.get("o_meta"), n_real_steps)
                env["x"] = h + _dot(o, refs["w_out"][...])

        def store():
            refs["h_out"][rows, :] = _rms(env["y"], refs["final_gain"][...]) if final_norm else env["y"]

        return [mix_out] + _swiglu_pieces(env, refs, refs[f"act{tile}"]) + [store]

    _interleave([pipeline(tile) for tile in range(STEP_TILES)], STAGE_LAG)


def _token_stage_out(mixer, core, h, w_out, ffn, extra, final_gain, n_real):
    last = final_gain is not None
    n_real_steps = n_real // STEP_ROWS
    n_rows = n_real if last else h.shape[0]
    has_meta = not last
    ins = [("h", h, _row_spec(D_MODEL))]
    if mixer == "gla":
        ins += [("o_fwd", core["o_fwd"], _row_spec(GLA_V_W, n_real_steps)),
                ("o_rev", core["o_rev"], _row_spec(GLA_V_W, n_real_steps)),
                ("r", extra["r"], _row_spec(GLA_V_W)), ("head_gain", extra["head_gain"], _const_spec((1, GLA_DV)))]
        if has_meta:
            ins += [("o_fwd_meta", core["o_fwd_meta"], _const_spec((ROW_TILE, GLA_V_W))),
                    ("o_rev_meta", core["o_rev_meta"], _const_spec((ROW_TILE, GLA_V_W)))]
    else:
        ins += [("o", core["o"], _row_spec(ATT_Q_W, n_real_steps))]
        if has_meta:
            ins += [("o_meta", core["o_meta"], _const_spec((ROW_TILE, ATT_Q_W)))]
    ins += [w_out] + _ffn_operands(*ffn)
    if last:
        ins.append(("final_gain", final_gain, _const_spec((1, D_MODEL))))
    outs = [("h_out", jax.ShapeDtypeStruct((n_rows, D_MODEL), F32), _row_spec(D_MODEL))]
    return _call(functools.partial(_stage_out_body, mixer, n_real_steps, last), (pl.cdiv(n_rows, STEP_ROWS),),
                 ins, outs, _act_scratch(), ("parallel",), f"token_stage_out_{mixer}")["h_out"]


def _gla_chunks_local(items):
    t_len = items[0][1].shape[0]
    row = lax.broadcasted_iota(jnp.int32, (t_len, t_len), 0)
    col = lax.broadcasted_iota(jnp.int32, (t_len, t_len), 1)
    row3 = lax.broadcasted_iota(jnp.int32, (t_len, 3 * t_len), 0)
    col3 = lax.broadcasted_iota(jnp.int32, (t_len, 3 * t_len), 1) % t_len
    mask = {False: col <= row, True: col >= row}
    mask3 = {False: (col3 <= row3).astype(BF16), True: (col3 >= row3).astype(BF16)}
    pieces = []
    for _, _, _, _, lg in items:
        hi = lg.astype(BF16)
        rest = lg - hi.astype(F32)
        mid = rest.astype(BF16)
        pieces.append(jnp.concatenate([hi, mid, (rest - mid.astype(F32)).astype(BF16)], axis=0))
    bs = [_dot(mask3[item[0]], p) for item, p in zip(items, pieces)]
    scaled = []
    for (rev, q, k, _, _), b in zip(items, bs):
        b_tot = b[0:1, :] if rev else b[t_len - 1:t_len, :]
        scaled.append(((q * jnp.exp(b)).astype(BF16), (k * jnp.exp(-b)).astype(BF16),
                       (k * jnp.exp(b_tot - b)).astype(BF16), jnp.exp(b_tot)))
    heads = [slice(h * GLA_DK, (h + 1) * GLA_DK) for h in range(GLA_HEADS)]
    scores = [[_dot_nt(qd[:, ks], ki[:, ks]) for ks in heads] for qd, ki, _, _ in scaled]
    lhs = [[jnp.concatenate([jnp.where(mask[item[0]], s, 0.0).astype(BF16), ke[:, ks].T], axis=0)
            for s, ks in zip(ss, heads)] for item, ss, (_, _, ke, _) in zip(items, scores, scaled)]
    local = [[_dot(l, item[3][:, h * GLA_DV:(h + 1) * GLA_DV]) for h, l in enumerate(ls)]
             for item, ls in zip(items, lhs)]
    return [(qd, dec, loc) for (qd, _, _, dec), loc in zip(scaled, local)]


def _gla_chunks_carry(chunks, states):
    t_len = chunks[0][0].shape[0]
    heads = [slice(h * GLA_DK, (h + 1) * GLA_DK) for h in range(GLA_HEADS)]
    carried = [[_dot(qd[:, ks], st.astype(BF16)) for ks, st in zip(heads, sts)]
               for (qd, _, _), sts in zip(chunks, states)]
    outs = [jnp.concatenate([loc[0:t_len, :] + c for loc, c in zip(local, cs)], axis=1)
            for (_, _, local), cs in zip(chunks, carried)]
    dec_cols = [[jnp.broadcast_to(dec[:, ks], (GLA_DK, GLA_DK)).T for ks in heads] for _, dec, _ in chunks]
    new_states = [[jnp.concatenate([dc] * (GLA_DV // GLA_DK), axis=1) * st + loc[t_len:, :]
                   for dc, st, loc in zip(dcs, sts, local)]
                  for dcs, sts, (_, _, local) in zip(dec_cols, states, chunks)]
    return outs, new_states


def _gla_scan_body(refs):
    j = pl.program_id(1)
    n_chunks = GLA_BLOCK // GLA_CHUNK
    pad = GLA_CHUNK - N_META
    streams = [(slot, rev) for slot in range(GLA_BATCHES) for rev in (False, True)]
    tag = lambda slot, rev: ("_rev" if rev else "_fwd") + str(slot)

    @pl.when(j == 0)
    def _():
        for s in streams:
            refs["state" + tag(*s)][...] = jnp.zeros_like(refs["state" + tag(*s)])

    def load_states(s):
        return [refs["state" + tag(*s)][h] for h in range(GLA_HEADS)]

    def store_states(s, states):
        for h in range(GLA_HEADS):
            refs["state" + tag(*s)][h] = states[h]

    def meta_chunks(rev):
        front = lambda ref: jnp.concatenate([jnp.zeros((pad, ref.shape[1]), ref.dtype), ref[...]], axis=0)
        for slot in range(GLA_BATCHES):
            m = "_meta" + str(slot)
            (qd, dec, local), = _gla_chunks_local([(rev, front(refs["q" + m]), front(refs["k" + m]),
                                                    front(refs["v" + m]),
                                                    front(refs[("lg_rev" if rev else "lg_fwd") + m]))])
            (o,), (states,) = _gla_chunks_carry([(qd, dec, local)], [load_states((slot, rev))])
            store_states((slot, rev), states)
            refs["o_rev_meta" if rev else "o_fwd_meta"][slot] = o[pad:, :]

    pl.when(j == 0)(lambda: meta_chunks(False))

    rows = lambda c: slice(c * GLA_CHUNK, (c + 1) * GLA_CHUNK)
    states = {s: load_states(s) for s in streams}
    for ci in range(n_chunks):
        items = []
        for s in streams:
            c = n_chunks - 1 - ci if s[1] else ci
            items.append((s[1], refs["q" + tag(*s)][rows(c), :], refs["k" + tag(*s)][rows(c), :],
                          refs["v" + tag(*s)][rows(c), :], refs["lg" + tag(*s)][rows(c), :]))
        outs, new_states = _gla_chunks_carry(_gla_chunks_local(items), [states[s] for s in streams])
        for s, o, st in zip(streams, outs, new_states):
            c = n_chunks - 1 - ci if s[1] else ci
            states[s] = st
            refs["o_rev" if s[1] else "o_fwd"][s[0], rows(c), :] = o
    for s in streams:
        store_states(s, states[s])

    pl.when(j == pl.num_programs(1) - 1)(lambda: meta_chunks(True))


def _gla_scan(bsz, seq, t):
    assert bsz % GLA_BATCHES == 0
    n_blk = seq // GLA_BLOCK
    meta_blk0 = bsz * seq // N_META
    ins, scratch = [], []
    for slot in range(GLA_BATCHES):
        batch = lambda p, slot=slot: p * GLA_BATCHES + slot
        fwd = lambda w, batch=batch: pl.BlockSpec((GLA_BLOCK, w), lambda p, j: (batch(p) * n_blk + j, 0))
        rev = lambda w, batch=batch: pl.BlockSpec((GLA_BLOCK, w), lambda p, j: (batch(p) * n_blk + n_blk - 1 - j, 0))
        meta = lambda w, batch=batch: pl.BlockSpec((N_META, w), lambda p, j: (meta_blk0 + batch(p), 0))
        for sfx, spec in (("_fwd", fwd), ("_rev", rev)):
            sfx_slot = sfx + str(slot)
            ins += [("q" + sfx_slot, t["q"], spec(GLA_QK_W)), ("k" + sfx_slot, t["k"], spec(GLA_QK_W)),
                    ("v" + sfx_slot, t["v"], spec(GLA_V_W)), ("lg" + sfx_slot, t["lg" + sfx], spec(GLA_QK_W))]
            scratch.append(("state" + sfx_slot, pltpu.VMEM((GLA_HEADS, GLA_DK, GLA_DV), F32)))
        m = "_meta" + str(slot)
        ins += [("q" + m, t["q"], meta(GLA_QK_W)), ("k" + m, t["k"], meta(GLA_QK_W)), ("v" + m, t["v"], meta(GLA_V_W)),
                ("lg_fwd" + m, t["lg_fwd"], meta(GLA_QK_W)), ("lg_rev" + m, t["lg_rev"], meta(GLA_QK_W))]
    real_shape = jax.ShapeDtypeStruct((bsz, seq, GLA_V_W), F32)
    meta_shape = jax.ShapeDtypeStruct((bsz, N_META, GLA_V_W), F32)
    meta_out = pl.BlockSpec((GLA_BATCHES, N_META, GLA_V_W), lambda p, j: (p, 0, 0))
    outs = [("o_fwd", real_shape, pl.BlockSpec((GLA_BATCHES, GLA_BLOCK, GLA_V_W), lambda p, j: (p, j, 0))),
            ("o_rev", real_shape, pl.BlockSpec((GLA_BATCHES, GLA_BLOCK, GLA_V_W), lambda p, j: (p, n_blk - 1 - j, 0))),
            ("o_fwd_meta", meta_shape, meta_out), ("o_rev_meta", meta_shape, meta_out)]
    res = _call(_gla_scan_body, (bsz // GLA_BATCHES, n_blk), ins, outs, scratch, ("parallel", "arbitrary"), "gla_scan")
    return {"o_fwd": res["o_fwd"].reshape(bsz * seq, GLA_V_W), "o_rev": res["o_rev"].reshape(bsz * seq, GLA_V_W),
            "o_fwd_meta": res["o_fwd_meta"].reshape(bsz * N_META, GLA_V_W),
            "o_rev_meta": res["o_rev_meta"].reshape(bsz * N_META, GLA_V_W)}


def _attn_scores_block(q, kt_ref, kb, s_ref, mx):
    cols = slice(kb * ATT_KV_BLOCK, (kb + 1) * ATT_KV_BLOCK)
    s = _dot(q, kt_ref[:, cols])
    s_ref[kb] = s
    tiles = [s[:, c * LANES:(c + 1) * LANES] for c in range(ATT_KV_BLOCK // LANES)]
    while len(tiles) > 1:
        tiles = [jnp.maximum(tiles[i], tiles[i + 1]) for i in range(0, len(tiles), 2)]
    return tiles[0] if mx is None else jnp.maximum(mx, tiles[0])


def _attn_scores_finish(sm, mx, sm_ref, m_ref):
    sm_ref[...] = sm
    m = jnp.maximum(jnp.max(mx, axis=1, keepdims=True), jnp.max(sm, axis=1, keepdims=True))
    m_ref[...] = jnp.broadcast_to(m, m_ref.shape)


def _attn_weighted_start(sm_ref, m_ref, vm):
    m = m_ref[...]
    return m, _dot(jnp.exp2(sm_ref[...] - m[:, 0:N_META]).astype(BF16), vm)


def _attn_weighted_block(s_ref, kb, m, v_ref, acc):
    rows = slice(kb * ATT_KV_BLOCK, (kb + 1) * ATT_KV_BLOCK)
    p = [jnp.exp2(s_ref[kb, :, c * LANES:(c + 1) * LANES] - m) for c in range(ATT_KV_BLOCK // LANES)]
    return acc + _dot(jnp.concatenate(p, axis=1).astype(BF16), v_ref[rows, :])


def _attn_core_body(n_kv_blocks, n_q, n_tiles, meta_queries, refs):
    t = pl.program_id(0)
    kt_ref, v_ref = refs["kt"], refs["v"]
    km = refs["k_meta"][...]
    vm = refs["v_meta"][...]
    s_ref, sm_ref, m_ref = refs["s"], refs["sm"], refs["m"]

    def stack(ref):
        return jnp.concatenate([ref[:, g * ATT_HEAD_DIM:(g + 1) * ATT_HEAD_DIM] for g in range(ATT_GROUP)], axis=0)

    def unstack(acc, ref):
        o = acc[:, 0:ATT_HEAD_DIM] / acc[:, ATT_HEAD_DIM:]
        rows = ref.shape[0]
        for g in range(ATT_GROUP):
            ref[:, g * ATT_HEAD_DIM:(g + 1) * ATT_HEAD_DIM] = o[g * rows:(g + 1) * rows, :].astype(BF16)

    @pl.when(t == 0)
    def _():
        for ref in (s_ref, sm_ref, m_ref):
            ref[...] = jnp.zeros_like(ref)

    q = stack(refs["q"])
    sm = _dot_nt(q, km)
    m, acc = _attn_weighted_start(sm_ref, m_ref, vm)
    mx = None
    for kb in range(n_kv_blocks):
        acc = _attn_weighted_block(s_ref, kb, m, v_ref, acc)
        mx = _attn_scores_block(q, kt_ref, kb, s_ref, mx)
    _attn_scores_finish(sm, mx, sm_ref, m_ref)
    unstack(acc, refs["o"])

    if meta_queries:
        @pl.when(jnp.minimum(t, n_tiles - 1) % n_q == n_q - 1)
        def _():
            sq_ref, smq_ref, mq_ref = refs["sq"], refs["smq"], refs["mq"]
            q = stack(refs["q_meta"])
            mx = None
            for kb in range(n_kv_blocks):
                mx = _attn_scores_block(q, kt_ref, kb, sq_ref, mx)
            _attn_scores_finish(_dot_nt(q, km), mx, smq_ref, mq_ref)
            m, acc = _attn_weighted_start(smq_ref, mq_ref, vm)
            for kb in range(n_kv_blocks):
                acc = _attn_weighted_block(sq_ref, kb, m, v_ref, acc)
            unstack(acc, refs["o_meta"])


def _attn_core(bsz, seq, t, meta_queries):
    n_q = seq // ATT_Q_TILE
    assert n_q >= 2
    n_tiles = bsz * ATT_KV_HEADS * n_q
    meta_blk0 = bsz * seq // N_META
    n_kv_blocks = seq // ATT_KV_BLOCK
    group_w = ATT_GROUP * ATT_HEAD_DIM

    def coords(tile):
        return tile // (ATT_KV_HEADS * n_q), (tile // n_q) % ATT_KV_HEADS, tile % n_q

    def spec(shape, which, fn):
        tile = (lambda s: jnp.minimum(s, n_tiles - 1)) if which == "cur" else (lambda s: jnp.maximum(s - 1, 0))
        return pl.BlockSpec(shape, lambda s: fn(*coords(tile(s))))

    def stash(sfx, rows):
        return [("s" + sfx, pltpu.VMEM((n_kv_blocks, rows, ATT_KV_BLOCK), F32)),
                ("sm" + sfx, pltpu.VMEM((rows, N_META), F32)), ("m" + sfx, pltpu.VMEM((rows, LANES), F32))]

    ins = [("q", t["q"], spec((ATT_Q_TILE, group_w), "cur", lambda b, kh, i: (b * n_q + i, kh))),
           ("kt", t["kt"], spec((ATT_HEAD_DIM, seq), "cur", lambda b, kh, i: (kh, b))),
           ("k_meta", t["k"], spec((N_META, ATT_HEAD_DIM), "cur", lambda b, kh, i: (meta_blk0 + b, kh))),
           ("v", t["v"], spec((seq, 2 * ATT_HEAD_DIM), "prev", lambda b, kh, i: (b, kh))),
           ("v_meta", t["v"], spec((N_META, 2 * ATT_HEAD_DIM), "prev", lambda b, kh, i: (meta_blk0 + b, kh)))]
    outs = [("o", jax.ShapeDtypeStruct((bsz * seq, ATT_Q_W), BF16),
             spec((ATT_Q_TILE, group_w), "prev", lambda b, kh, i: (b * n_q + i, kh)))]
    scratch = stash("", ATT_GROUP * ATT_Q_TILE)
    if meta_queries:
        ins.append(("q_meta", t["q"], spec((N_META, group_w), "cur", lambda b, kh, i: (meta_blk0 + b, kh))))
        outs.append(("o_meta", jax.ShapeDtypeStruct((bsz * N_META, ATT_Q_W), BF16),
                     spec((N_META, group_w), "cur", lambda b, kh, i: (b, kh))))
        scratch += stash("q", ATT_GROUP * N_META)
    return _call(functools.partial(_attn_core_body, n_kv_blocks, n_q, n_tiles, meta_queries), (n_tiles + 1,),
                 ins, outs, scratch, ("arbitrary",), "attn_core")


def kernel(x, meta_tokens, norm_ffn1, ffn1_w_gate, ffn1_w_up, ffn1_w_down, norm_mix, gla_w_in, gla_gate_w1, gla_gate_w2, gla_gate_b, gla_head_norm, gla_w_out, attn_w_in, attn_q_norm, attn_k_norm, attn_w_out, norm_ffn2, ffn2_w_gate, ffn2_w_up, ffn2_w_down, norm_final):
    bsz, seq, d = x.shape
    depth = norm_ffn1.shape[0]
    assert d == D_MODEL and bsz * N_META == ROW_TILE and seq % STEP_ROWS == 0 and seq % ATT_KV_BLOCK == 0
    n_real = bsz * seq
    row = lambda g: g.reshape(1, -1).astype(F32)
    small = lambda key, arr: (key, arr, _const_spec(arr.shape))

    h = x.reshape(n_real, d)
    h_meta = jnp.broadcast_to(meta_tokens.astype(x.dtype)[None], (bsz, N_META, d)).reshape(bsz * N_META, d)
    (ffn1_wg, ffn1_wu, ffn1_wd, ffn2_wg, ffn2_wu, ffn2_wd, gla_in_w, gla_out_w, attn_in_w, attn_out_w) = _to_bf16(
        [ffn1_w_gate, ffn1_w_up, ffn1_w_down, ffn2_w_gate, ffn2_w_up, ffn2_w_down,
         gla_w_in, gla_w_out, attn_w_in, attn_w_out])

    for i in range(depth):
        j = i // N_MIXERS
        last = i == depth - 1
        ffn1 = (row(norm_ffn1[i]), ffn1_wg, ffn1_wu, ffn1_wd, i)
        ffn2 = (row(norm_ffn2[i]), ffn2_wg, ffn2_wu, ffn2_wd, i)
        final_gain = row(norm_final) if last else None
        if i % N_MIXERS == 0:
            zero = jnp.zeros_like(gla_gate_w2[j, 0])
            gate_w2 = jnp.concatenate([jnp.concatenate([gla_gate_w2[j, 0], zero], axis=1),
                                       jnp.concatenate([zero, gla_gate_w2[j, 1]], axis=1)], axis=0)
            gate_w1 = jnp.concatenate([gla_gate_w1[j, 0], gla_gate_w1[j, 1]], axis=1)
            mixer_ins = [_layer_operand("w_in", gla_in_w, j), small("gate_w1", gate_w1.astype(BF16)),
                         small("gate_w2", gate_w2.astype(BF16)),
                         small("gate_b", gla_gate_b[j].reshape(1, -1).astype(F32))]
            t = _token_stage_in("gla", h, h_meta, ffn1, row(norm_mix[i]), mixer_ins, seq)
            core = _gla_scan(bsz, seq, t)
            h = _token_stage_out("gla", core, t["h_out"], _layer_operand("w_out", gla_out_w, j), ffn2,
                                 {"r": t["r"], "head_gain": row(gla_head_norm[j])}, final_gain, n_real)
        else:
            mixer_ins = [_layer_operand("w_in", attn_in_w, j), small("q_gain", row(attn_q_norm[j])),
                         small("k_gain", row(attn_k_norm[j]))]
            t = _token_stage_in("attn", h, h_meta, ffn1, row(norm_mix[i]), mixer_ins, seq)
            core = _attn_core(bsz, seq, t, meta_queries=not last)
            h = _token_stage_out("attn", core, t["h_out"], _layer_operand("w_out", attn_out_w, j), ffn2, {},
                                 final_gain, n_real)
        h_meta = None
    return h.reshape(bsz, seq, d)
```
